```python
import jax, jax.numpy as jnp
from jax import lax
import numpy as np

D_MODEL = 1024
BATCH = 1
SEQ = 16384
DEPTH = 2
DEC_BATCH = 32
DEC_SEQ = 1
PAST_LEN = 16384
PAGE_SIZE = 128

FOX_HEADS = 8
HEAD_DIM = 64
FOX_WIDTH = FOX_HEADS * HEAD_DIM
CONF_CH = D_MODEL - FOX_WIDTH
CONF_CONV_WIDTH = 31
SC_CH = D_MODEL
SC_CONV_WIDTH = 3
D_FF = 4 * D_MODEL
Q_BLOCK = 128
EPS = 1e-6
FORGET_BIAS_INIT = 3.0
ATTN_SCALE = HEAD_DIM ** -0.5
IN_A_WIDTH = 3 * FOX_WIDTH + FOX_HEADS + 2 * CONF_CH
IN_C_WIDTH = 3 * SC_CH

kernel_name = "fox_conformer_shortconv_adaln_decoder_step"


def rmsnorm(x, g):
    xf = x.astype(jnp.float32)
    y = xf * lax.rsqrt(jnp.mean(xf * xf, axis=-1, keepdims=True) + EPS)
    return (y * g.astype(jnp.float32)).astype(x.dtype)


def layernorm(x, g, b):
    xf = x.astype(jnp.float32)
    mu = jnp.mean(xf, axis=-1, keepdims=True)
    var = jnp.mean(jnp.square(xf - mu), axis=-1, keepdims=True)
    y = (xf - mu) * lax.rsqrt(var + EPS) * g.astype(jnp.float32) + b.astype(jnp.float32)
    return y.astype(x.dtype)


def ada_params(c, w_ada, b_ada):
    mod = jax.nn.silu(c) @ w_ada + b_ada
    return jnp.split(mod[:, None, :], 6, axis=-1)


def modulate(x, g, shift, scale):
    return rmsnorm(x, g) * (1.0 + scale) + shift


def causal_dwconv(u, w, prev):
    full = jnp.concatenate([prev.astype(u.dtype), u], axis=1)
    y = lax.conv_general_dilated(full, w[:, None, :].astype(u.dtype), window_strides=(1,),
                                 padding='VALID', dimension_numbers=('NWC', 'WIO', 'NWC'),
                                 feature_group_count=u.shape[-1])
    new_prev = full[:, full.shape[1] - (w.shape[0] - 1):]
    return y, new_prev


def sq_relu_mlp(h, w_up, w_down):
    a = jax.nn.relu(h @ w_up)
    return (a * a) @ w_down


def fox_prompt(q, k, v, logf):
    B, T, H, Dh = q.shape
    nb = T // Q_BLOCK
    F = jnp.cumsum(logf, axis=1).transpose(0, 2, 1)
    key_pos = jnp.arange(T)

    def block(i):
        start = i * Q_BLOCK
        qi = lax.dynamic_slice_in_dim(q, start, Q_BLOCK, axis=1)
        Fi = lax.dynamic_slice_in_dim(F, start, Q_BLOCK, axis=2)
        s = jnp.einsum('bqhd,bkhd->bhqk', qi, k, preferred_element_type=jnp.float32) * ATTN_SCALE
        s = s + Fi[..., :, None] - F[..., None, :]
        mask = key_pos[None, :] <= (start + jnp.arange(Q_BLOCK))[:, None]
        p = jax.nn.softmax(jnp.where(mask, s, -jnp.inf), axis=-1)
        return jnp.einsum('bhqk,bkhd->bqhd', p.astype(v.dtype), v, preferred_element_type=jnp.float32)

    out = lax.map(block, jnp.arange(nb))
    return out.transpose(1, 0, 2, 3, 4).reshape(B, T, H, Dh).astype(q.dtype)


def fox_sample(q, k, v, logf, cache_k, cache_v, cache_logf, page_table):
    Bd, S, H, Dh = q.shape
    past = page_table.shape[1] * cache_k.shape[1]
    k_all = jnp.concatenate([cache_k[page_table].reshape(Bd, past, H, Dh).astype(k.dtype), k], axis=1)
    v_all = jnp.concatenate([cache_v[page_table].reshape(Bd, past, H, Dh).astype(v.dtype), v], axis=1)
    lf_all = jnp.concatenate([cache_logf[page_table].reshape(Bd, past, H).astype(jnp.float32), logf], axis=1)
    F = jnp.cumsum(lf_all, axis=1).transpose(0, 2, 1)
    Fq = F[:, :, past:]
    s = jnp.einsum('bqhd,bkhd->bhqk', q, k_all, preferred_element_type=jnp.float32) * ATTN_SCALE
    s = s + Fq[..., :, None] - F[..., None, :]
    mask = jnp.arange(past + S)[None, :] <= (past + jnp.arange(S))[:, None]
    p = jax.nn.softmax(jnp.where(mask, s, -jnp.inf), axis=-1)
    out = jnp.einsum('bhqk,bkhd->bqhd', p.astype(v_all.dtype), v_all, preferred_element_type=jnp.float32)
    return out.astype(q.dtype)


def mixer_fox_conformer(h, conf_prev, attend, w_in, b_f, w_dw, b_dw, ln_g, ln_b, w_out):
    B, T, _ = h.shape
    proj = h @ w_in
    q, k, v, fg, glu = jnp.split(proj, [FOX_WIDTH, 2 * FOX_WIDTH, 3 * FOX_WIDTH, 3 * FOX_WIDTH + FOX_HEADS], axis=-1)
    q = q.reshape(B, T, FOX_HEADS, HEAD_DIM)
    k = k.reshape(B, T, FOX_HEADS, HEAD_DIM)
    v = v.reshape(B, T, FOX_HEADS, HEAD_DIM)
    logf = jax.nn.log_sigmoid((fg + b_f).astype(jnp.float32))
    attn = attend(q, k, v, logf).reshape(B, T, FOX_WIDTH)
    a, g = jnp.split(glu, 2, axis=-1)
    u = a * jax.nn.sigmoid(g)
    y, conf_new = causal_dwconv(u, w_dw, conf_prev)
    y = jax.nn.silu(layernorm(y + b_dw, ln_g, ln_b))
    out = jnp.concatenate([attn, y], axis=-1) @ w_out
    return out, (k, v, logf, conf_new)


def mixer_short_conv(h, sc_prev, w_in, w_dw, w_out):
    b_gate, c_gate, xt = jnp.split(h @ w_in, 3, axis=-1)
    y, sc_new = causal_dwconv(c_gate * xt, w_dw, sc_prev)
    return (b_gate * y) @ w_out, sc_new


def residual_block(x, c, w_ada, b_ada, g_mix, g_mlp, w_up, w_down, mixer):
    sh_m, sc_m, gt_m, sh_f, sc_f, gt_f = ada_params(c, w_ada, b_ada)
    mix_out, new_state = mixer(modulate(x, g_mix, sh_m, sc_m))
    x = x + gt_m * mix_out
    x = x + gt_f * sq_relu_mlp(modulate(x, g_mlp, sh_f, sc_f), w_up, w_down)
    return x, new_state


def run_trunk(x, c, attend, conf_prev, sc_prev, w_even, w_odd, final_norm):
    states = []
    for layer in range(DEPTH):
        if layer % 2 == 0:
            (w_ada, b_ada, g_mix, g_mlp, w_in, b_f, w_dw, b_dw, ln_g, ln_b, w_out, w_up, w_down) = w_even
            mixer = lambda h: mixer_fox_conformer(h, conf_prev, attend, w_in, b_f, w_dw, b_dw, ln_g, ln_b, w_out)
        else:
            (w_ada, b_ada, g_mix, g_mlp, w_in, w_dw, w_out, w_up, w_down) = w_odd
            mixer = lambda h: mixer_short_conv(h, sc_prev, w_in, w_dw, w_out)
        x, st = residual_block(x, c, w_ada, b_ada, g_mix, g_mlp, w_up, w_down, mixer)
        states.append(st)
    k_new, v_new, logf_new, conf_new = states[0]
    sc_new = states[1]
    return rmsnorm(x, final_norm), k_new, v_new, logf_new, conf_new, sc_new


def _normal(k, shape, scale):
    return scale * jax.random.normal(k, shape, jnp.float32)


def setup_inputs(seed: int = 0) -> dict:
    key = jax.random.key(seed)
    ks = jax.random.split(key, 40)
    n_pages = PAST_LEN // PAGE_SIZE
    n_used = DEC_BATCH * n_pages
    n_phys = n_used + max(n_used // 4, 1)
    page_table = jax.random.permutation(ks[0], n_phys)[:n_used].reshape(DEC_BATCH, n_pages).astype(jnp.int32)
    d_in = D_MODEL ** -0.5
    return {
        "x_prompt": _normal(ks[1], (BATCH, SEQ, D_MODEL), 1.0),
        "x_sample": _normal(ks[2], (DEC_BATCH, DEC_SEQ, D_MODEL), 1.0),
        "cache_k": _normal(ks[3], (n_phys, PAGE_SIZE, FOX_HEADS, HEAD_DIM), 1.0),
        "cache_v": _normal(ks[4], (n_phys, PAGE_SIZE, FOX_HEADS, HEAD_DIM), 1.0),
        "cache_logf": jax.nn.log_sigmoid(FORGET_BIAS_INIT + _normal(ks[5], (n_phys, PAGE_SIZE, FOX_HEADS), 0.5)),
        "state_conformer_conv": _normal(ks[6], (DEC_BATCH, CONF_CONV_WIDTH - 1, CONF_CH), 1.0),
        "state_short_conv": _normal(ks[7], (DEC_BATCH, SC_CONV_WIDTH - 1, SC_CH), 1.0),
        "page_table": page_table,
        "c_prompt": _normal(ks[8], (BATCH, D_MODEL), 1.0),
        "c_sample": _normal(ks[9], (DEC_BATCH, D_MODEL), 1.0),
        "l0_w_ada": _normal(ks[10], (D_MODEL, 6 * D_MODEL), 0.5 * d_in),
        "l0_b_ada": _normal(ks[11], (6 * D_MODEL,), 0.02),
        "l0_norm_mix": 1.0 + _normal(ks[12], (D_MODEL,), 0.02),
        "l0_norm_mlp": 1.0 + _normal(ks[13], (D_MODEL,), 0.02),
        "l0_w_in": _normal(ks[14], (D_MODEL, IN_A_WIDTH), d_in),
        "l0_b_forget": FORGET_BIAS_INIT + _normal(ks[15], (FOX_HEADS,), 0.1),
        "l0_w_dw": _normal(ks[16], (CONF_CONV_WIDTH, CONF_CH), CONF_CONV_WIDTH ** -0.5),
        "l0_b_dw": _normal(ks[17], (CONF_CH,), 0.02),
        "l0_conv_ln_g": 1.0 + _normal(ks[18], (CONF_CH,), 0.02),
        "l0_conv_ln_b": _normal(ks[19], (CONF_CH,), 0.02),
        "l0_w_out": _normal(ks[20], (D_MODEL, D_MODEL), d_in),
        "l0_w_up": _normal(ks[21], (D_MODEL, D_FF), d_in),
        "l0_w_down": _normal(ks[22], (D_FF, D_MODEL), D_FF ** -0.5),
        "l1_w_ada": _normal(ks[23], (D_MODEL, 6 * D_MODEL), 0.5 * d_in),
        "l1_b_ada": _normal(ks[24], (6 * D_MODEL,), 0.02),
        "l1_norm_mix": 1.0 + _normal(ks[25], (D_MODEL,), 0.02),
        "l1_norm_mlp": 1.0 + _normal(ks[26], (D_MODEL,), 0.02),
        "l1_w_in": _normal(ks[27], (D_MODEL, IN_C_WIDTH), d_in),
        "l1_w_dw": _normal(ks[28], (SC_CONV_WIDTH, SC_CH), SC_CONV_WIDTH ** -0.5),
        "l1_w_out": _normal(ks[29], (SC_CH, D_MODEL), SC_CH ** -0.5),
        "l1_w_up": _normal(ks[30], (D_MODEL, D_FF), d_in),
        "l1_w_down": _normal(ks[31], (D_FF, D_MODEL), D_FF ** -0.5),
        "final_norm": 1.0 + _normal(ks[32], (D_MODEL,), 0.02),
    }


def reference(x_prompt, x_sample, cache_k, cache_v, cache_logf, state_conformer_conv, state_short_conv,
              page_table, c_prompt, c_sample,
              l0_w_ada, l0_b_ada, l0_norm_mix, l0_norm_mlp, l0_w_in, l0_b_forget, l0_w_dw, l0_b_dw,
              l0_conv_ln_g, l0_conv_ln_b, l0_w_out, l0_w_up, l0_w_down,
              l1_w_ada, l1_b_ada, l1_norm_mix, l1_norm_mlp, l1_w_in, l1_w_dw, l1_w_out, l1_w_up, l1_w_down,
              final_norm):
    w_even = (l0_w_ada, l0_b_ada, l0_norm_mix, l0_norm_mlp, l0_w_in, l0_b_forget, l0_w_dw, l0_b_dw,
              l0_conv_ln_g, l0_conv_ln_b, l0_w_out, l0_w_up, l0_w_down)
    w_odd = (l1_w_ada, l1_b_ada, l1_norm_mix, l1_norm_mlp, l1_w_in, l1_w_dw, l1_w_out, l1_w_up, l1_w_down)

    bp = x_prompt.shape[0]
    conf0 = jnp.zeros((bp, CONF_CONV_WIDTH - 1, CONF_CH), x_prompt.dtype)
    sc0 = jnp.zeros((bp, SC_CONV_WIDTH - 1, SC_CH), x_prompt.dtype)
    y_prompt, k_p, v_p, lf_p, conf_p, sc_p = run_trunk(
        x_prompt, c_prompt, fox_prompt, conf0, sc0, w_even, w_odd, final_norm)

    attend_s = lambda q, k, v, lf: fox_sample(q, k, v, lf, cache_k, cache_v, cache_logf, page_table)
    y_sample, k_s, v_s, lf_s, conf_s, sc_s = run_trunk(
        x_sample, c_sample, attend_s, state_conformer_conv, state_short_conv, w_even, w_odd, final_norm)

    return (y_prompt, y_sample, k_p, v_p, lf_p, conf_p, sc_p, k_s, v_s, lf_s, conf_s, sc_s)
```

```python
import functools

import jax
import jax.numpy as jnp
from jax import lax
from jax.experimental import pallas as pl
from jax.experimental.pallas import tpu as pltpu

F32 = jnp.float32
BF16 = jnp.bfloat16

D_MODEL = 1024
FOX_HEADS = 8
HEAD_DIM = 64
FOX_WIDTH = FOX_HEADS * HEAD_DIM
CONF_CH = D_MODEL - FOX_WIDTH
CONF_CONV_WIDTH = 31
SC_CONV_WIDTH = 3
D_FF = 4 * D_MODEL
PAGE_SIZE = 128
EPS = 1e-6
ATTN_SCALE = HEAD_DIM ** -0.5

V7X_LANES = 128
V7X_SUBLANES = 8
V7X_VMEM_BYTES = 64 * 1024 * 1024
VMEM_LIMIT = 52 * 1024 * 1024

ROW_TILE = 512
CONV_TILE = 256
CONV_CHUNK = 64
CONV_HALO = 32
ATTN_TILE = 256
PAGES_PER_STEP = 8


def _const_spec(shape):
    nd = len(shape)
    return pl.BlockSpec(shape, lambda *_: (0,) * nd, pipeline_mode=pl.Buffered(1))


def _params(*sem):
    return pltpu.CompilerParams(dimension_semantics=sem, vmem_limit_bytes=VMEM_LIMIT)


def _modnorm(x, g, shift, scale):
    ms = jnp.mean(x * x, axis=-1, keepdims=True)
    y = x * lax.rsqrt(ms + EPS)
    return (y * g) * (1.0 + scale) + shift


def _split3(x):
    hi = x.astype(BF16).astype(F32)
    r = x - hi
    mid = r.astype(BF16).astype(F32)
    lo = (r - mid).astype(BF16).astype(F32)
    return hi, mid, lo


def _ada_kernel(c_ref, w_ref, b_ref, o_ref):
    c = c_ref[...]
    s = (c * jax.nn.sigmoid(c)).astype(BF16)
    o_ref[...] = jnp.dot(s, w_ref[...].astype(BF16), preferred_element_type=F32) + b_ref[...]


def _ada(c_all, w_ada, b_ada):
    m = c_all.shape[0]
    n = w_ada.shape[1]
    bn = 1536
    return pl.pallas_call(
        _ada_kernel,
        out_shape=jax.ShapeDtypeStruct((m, n), F32),
        grid=(n // bn,),
        in_specs=[pl.BlockSpec((m, D_MODEL), lambda j: (0, 0)),
                  pl.BlockSpec((D_MODEL, bn), lambda j: (0, j)),
                  pl.BlockSpec((1, bn), lambda j: (0, j))],
        out_specs=pl.BlockSpec((m, bn), lambda j: (0, j)),
        compiler_params=_params("arbitrary"),
        name="ada",
    )(c_all, w_ada, b_ada.reshape(1, n))


def _proj0_kernel(x_ref, sh_ref, sc_ref, g_ref, wqkv_ref, wf_ref, bf_ref, wglu_ref,
                  q_ref, k_ref, v_ref, kb_ref, vb_ref, lf_ref, u_ref, *rest, with_cumsum):
    x = x_ref[...]
    h = _modnorm(x, g_ref[...], sh_ref[...], sc_ref[...]).astype(BF16)
    qkv = jnp.dot(h, wqkv_ref[...], preferred_element_type=F32)
    k = qkv[:, FOX_WIDTH:2 * FOX_WIDTH]
    v = qkv[:, 2 * FOX_WIDTH:]
    q_ref[...] = (qkv[:, :FOX_WIDTH] * ATTN_SCALE).astype(BF16)
    k_ref[...] = k
    v_ref[...] = v
    kb_ref[...] = k.astype(BF16)
    vb_ref[...] = v.astype(BF16)

    fg = jnp.dot(h, wf_ref[...], preferred_element_type=F32) + bf_ref[...]
    lf = jnp.minimum(fg, 0.0) - jnp.log1p(jnp.exp(-jnp.abs(fg)))
    lf_ref[...] = lf[:, :FOX_HEADS]

    glu = jnp.dot(h, wglu_ref[...], preferred_element_type=F32)
    u_ref[...] = glu[:, :CONF_CH] * jax.nn.sigmoid(glu[:, CONF_CH:])

    if with_cumsum:
        ft_ref, tri_ref, carry_ref = rest
        tm = x.shape[0]
        i = pl.program_id(0)

        @pl.when(i == 0)
        def _():
            carry_ref[...] = jnp.zeros_like(carry_ref)
            r = lax.broadcasted_iota(jnp.int32, (tm, tm), 0)
            c = lax.broadcasted_iota(jnp.int32, (tm, tm), 1)
            tri_ref[...] = (r <= c).astype(BF16)

        lft = lf.T[:FOX_HEADS]
        parts = jnp.concatenate(_split3(lft), axis=0).astype(BF16)
        cs = jnp.dot(parts, tri_ref[...], preferred_element_type=F32)
        ft = cs[0:8] + cs[8:16] + cs[16:24] + carry_ref[...]
        ft_ref[...] = ft
        carry_ref[...] = ft[:, tm - 1:tm]


def _proj0(x, mod, g, w, with_cumsum):
    t = x.shape[0]
    tm = min(ROW_TILE, t)
    mm = mod.shape[0]
    mrow = (lambda i: (0, 0)) if mm == 1 else (lambda i: (i, 0))
    mrow1 = (lambda i: (0, 1)) if mm == 1 else (lambda i: (i, 1))
    bm = 1 if mm == 1 else tm
    row = lambda i: (i, 0)
    out_shape = [jax.ShapeDtypeStruct((t, FOX_WIDTH), BF16),
                 jax.ShapeDtypeStruct((t, FOX_WIDTH), F32),
                 jax.ShapeDtypeStruct((t, FOX_WIDTH), F32),
                 jax.ShapeDtypeStruct((t, FOX_WIDTH), BF16),
                 jax.ShapeDtypeStruct((t, FOX_WIDTH), BF16),
                 jax.ShapeDtypeStruct((t, FOX_HEADS), F32),
                 jax.ShapeDtypeStruct((t, CONF_CH), F32)]
    out_specs = [pl.BlockSpec((tm, FOX_WIDTH), row)] * 5 + [
        pl.BlockSpec((tm, FOX_HEADS), row), pl.BlockSpec((tm, CONF_CH), row)]
    scratch = []
    if with_cumsum:
        out_shape.append(jax.ShapeDtypeStruct((FOX_HEADS, t), F32))
        out_specs.append(pl.BlockSpec((FOX_HEADS, tm), lambda i: (0, i)))
        scratch = [pltpu.VMEM((tm, tm), BF16), pltpu.VMEM((FOX_HEADS, 1), F32)]
    return pl.pallas_call(
        functools.partial(_proj0_kernel, with_cumsum=with_cumsum),
        out_shape=out_shape,
        grid=(t // tm,),
        in_specs=[pl.BlockSpec((tm, D_MODEL), row),
                  pl.BlockSpec((bm, D_MODEL), mrow),
                  pl.BlockSpec((bm, D_MODEL), mrow1),
                  _const_spec((1, D_MODEL)),
                  _const_spec(w["qkv"].shape),
                  _const_spec(w["f"].shape),
                  _const_spec((1, V7X_LANES)),
                  _const_spec(w["glu"].shape)],
        out_specs=out_specs,
        scratch_shapes=scratch,
        compiler_params=_params("arbitrary"),
        name="proj0",
    )(x, mod, mod, g, w["qkv"], w["f"], w["bf"], w["glu"])


def _ln_silu(y, ln_g, ln_b):
    mu = jnp.mean(y, axis=-1, keepdims=True)
    d = y - mu
    var = jnp.mean(d * d, axis=-1, keepdims=True)
    z = d * lax.rsqrt(var + EPS) * ln_g + ln_b
    return z * jax.nn.sigmoid(z)


def _conv0_kernel(u_ref, up_ref, wdw_ref, bdw_ref, lng_ref, lnb_ref, y_ref, buf_ref):
    i = pl.program_id(0)
    tm = u_ref.shape[0]
    buf_ref[0:CONV_HALO, :] = jnp.where(i == 0, 0.0, up_ref[...])
    buf_ref[CONV_HALO:, :] = u_ref[...]
    first = CONV_HALO - (CONF_CONV_WIDTH - 1)
    for c0 in range(0, tm, CONV_CHUNK):
        acc = jnp.zeros((CONV_CHUNK, CONF_CH), F32)
        for kk in range(CONF_CONV_WIDTH):
            acc = acc + wdw_ref[kk:kk + 1, :] * buf_ref[pl.ds(first + c0 + kk, CONV_CHUNK), :]
        z = _ln_silu(acc + bdw_ref[...], lng_ref[...], lnb_ref[...])
        y_ref[c0:c0 + CONV_CHUNK, :] = z.astype(BF16)


def _conv0_prompt(u, w_dw, b_dw, ln_g, ln_b):
    t = u.shape[0]
    tm = CONV_TILE
    per = tm // CONV_HALO
    return pl.pallas_call(
        _conv0_kernel,
        out_shape=jax.ShapeDtypeStruct((t, CONF_CH), BF16),
        grid=(t // tm,),
        in_specs=[pl.BlockSpec((tm, CONF_CH), lambda i: (i, 0)),
                  pl.BlockSpec((CONV_HALO, CONF_CH), lambda i: (jnp.maximum(i * per - 1, 0), 0)),
                  _const_spec((CONF_CONV_WIDTH, CONF_CH)),
                  _const_spec((1, CONF_CH)), _const_spec((1, CONF_CH)), _const_spec((1, CONF_CH))],
        out_specs=pl.BlockSpec((tm, CONF_CH), lambda i: (i, 0)),
        scratch_shapes=[pltpu.VMEM((tm + CONV_HALO, CONF_CH), F32)],
        compiler_params=_params("arbitrary"),
        name="conv0",
    )(u, u, w_dw, b_dw.reshape(1, -1), ln_g.reshape(1, -1), ln_b.reshape(1, -1))


def _conv0s_kernel(st_ref, u_ref, wdw_ref, bdw_ref, lng_ref, lnb_ref, y_ref):
    acc = wdw_ref[CONF_CONV_WIDTH - 1:CONF_CONV_WIDTH, :] * u_ref[...]
    for kk in range(CONF_CONV_WIDTH - 1):
        acc = acc + wdw_ref[kk:kk + 1, :] * st_ref[kk]
    z = _ln_silu(acc + bdw_ref[...], lng_ref[...], lnb_ref[...])
    y_ref[...] = z.astype(BF16)


def _conv0_sample(state_t, u, w_dw, b_dw, ln_g, ln_b):
    b = u.shape[0]
    return pl.pallas_call(
        _conv0s_kernel,
        out_shape=jax.ShapeDtypeStruct((b, CONF_CH), BF16),
        name="conv0s",
    )(state_t, u, w_dw, b_dw.reshape(1, -1), ln_g.reshape(1, -1), ln_b.reshape(1, -1))


def _attn_kernel(q_ref, kb_ref, vb_ref, fcol_ref, frow_ref, o_ref, m_ref, l_ref, acc_ref):
    i = pl.program_id(1)
    tq = q_ref.shape[0]
    lane = lax.broadcasted_iota(jnp.int32, (tq, V7X_LANES), 1)
    lo = lane < HEAD_DIM
    q = q_ref[...]
    zero = jnp.zeros_like(q)
    qh = (jnp.where(lo, q, zero), jnp.where(lo, zero, q))
    fcol = fcol_ref[0]

    m_ref[...] = jnp.full_like(m_ref, -jnp.inf)
    l_ref[...] = jnp.zeros_like(l_ref)
    acc_ref[...] = jnp.zeros_like(acc_ref)

    def block(j, masked):
        start = pl.multiple_of(j * tq, tq)
        kblk = kb_ref[pl.ds(start, tq), :]
        vblk = vb_ref[pl.ds(start, tq), :]
        frow = frow_ref[0, j]
        alphas, pvs = [], []
        for hh in range(2):
            s = lax.dot_general(qh[hh], kblk, (((1,), (1,)), ((), ())), preferred_element_type=F32)
            s = (s + fcol[:, hh:hh + 1]) - frow[hh:hh + 1, :]
            if masked:
                r = lax.broadcasted_iota(jnp.int32, s.shape, 0)
                c = lax.broadcasted_iota(jnp.int32, s.shape, 1)
                s = jnp.where(c <= r, s, -jnp.inf)
            m_prev = m_ref[hh]
            m_new = jnp.maximum(m_prev, jnp.max(s, axis=1, keepdims=True))
            alpha = jnp.exp(m_prev - m_new)
            p = jnp.exp(s - m_new)
            l_ref[hh] = alpha * l_ref[hh] + jnp.sum(p, axis=1, keepdims=True)
            m_ref[hh] = m_new
            alphas.append(alpha)
            pvs.append(jnp.dot(p.astype(BF16), vblk, preferred_element_type=F32))
        acc_ref[...] = (acc_ref[...] * jnp.where(lo, alphas[0], alphas[1])
                        + jnp.where(lo, pvs[0], pvs[1]))

    def body(j, carry):
        block(j, False)
        return carry

    lax.fori_loop(0, i, body, 0)
    block(i, True)
    o_ref[...] = (acc_ref[...] / jnp.where(lo, l_ref[0], l_ref[1])).astype(BF16)


def _attn_prompt(q, kb, vb, ft):
    t = q.shape[0]
    tq = ATTN_TILE
    npair = FOX_HEADS // 2
    nblk = t // tq
    fcol = ft.reshape(npair, 2, t).transpose(0, 2, 1)
    frow = ft.reshape(npair, 2, nblk, tq).transpose(0, 2, 1, 3)
    return pl.pallas_call(
        _attn_kernel,
        out_shape=jax.ShapeDtypeStruct((t, FOX_WIDTH), BF16),
        grid=(npair, t // tq),
        in_specs=[pl.BlockSpec((tq, V7X_LANES), lambda p, i: (i, p)),
                  pl.BlockSpec((t, V7X_LANES), lambda p, i: (0, p)),
                  pl.BlockSpec((t, V7X_LANES), lambda p, i: (0, p)),
                  pl.BlockSpec((1, tq, 2), lambda p, i: (p, i, 0)),
                  pl.BlockSpec((1, nblk, 2, tq), lambda p, i: (p, 0, 0, 0))],
        out_specs=pl.BlockSpec((tq, V7X_LANES), lambda p, i: (i, p)),
        scratch_shapes=[pltpu.VMEM((2, tq, 1), F32), pltpu.VMEM((2, tq, 1), F32),
                        pltpu.VMEM((tq, V7X_LANES), F32)],
        compiler_params=_params("arbitrary", "arbitrary"),
        name="attn_prompt",
    )(q, kb, vb, fcol, frow)


def _expand_exact(x, e):
    return sum(jnp.dot(p.astype(BF16), e, preferred_element_type=F32) for p in _split3(x))


def _attn_s_kernel(pt_ref, q_ref, kn_ref, vn_ref, lfn_ref, *rest):
    del pt_ref
    npg = PAGES_PER_STEP
    k_refs = rest[:npg]
    v_refs = rest[npg:2 * npg]
    lf_refs = rest[2 * npg:3 * npg]
    o_ref, qbd_ref, e_ref, w_ref, lfp_ref, m_ref, l_ref, run_ref, acc_ref = rest[3 * npg:]
    g = pl.program_id(1)
    lanes = V7X_LANES

    def fold(s, v, first):
        m_new = jnp.max(s, axis=0, keepdims=True)
        if not first:
            m_prev = m_ref[...]
            m_new = jnp.maximum(m_prev, m_new)
        p = jnp.exp(s - m_new)
        pe = jnp.dot(p.astype(BF16), e_ref[...], preferred_element_type=F32)
        pv = pe * v
        pv8 = pv.reshape(-1, V7X_SUBLANES, FOX_WIDTH).sum(axis=0)
        psum = jnp.sum(p, axis=0, keepdims=True)
        if first:
            l_ref[...] = psum
            acc_ref[...] = pv8
        else:
            alpha = jnp.exp(m_prev - m_new)
            l_ref[...] = alpha * l_ref[...] + psum
            a8 = _expand_exact(jnp.broadcast_to(alpha, (V7X_SUBLANES, lanes)), e_ref[...])
            acc_ref[...] = acc_ref[...] * a8 + pv8
        m_ref[...] = m_new

    @pl.when(g == 0)
    def _():
        hrow = lax.broadcasted_iota(jnp.int32, (lanes, FOX_WIDTH), 0)
        ccol = lax.broadcasted_iota(jnp.int32, (lanes, FOX_WIDTH), 1)
        sel = hrow == ccol // HEAD_DIM
        e_ref[...] = sel.astype(BF16)
        qbd_ref[...] = jnp.where(sel, jnp.broadcast_to(q_ref[0], (lanes, FOX_WIDTH)), 0.0).astype(BF16)
        tr = lax.broadcasted_iota(jnp.int32, (PAGE_SIZE, PAGE_SIZE), 0)
        tc = lax.broadcasted_iota(jnp.int32, (PAGE_SIZE, PAGE_SIZE), 1)
        w_ref[...] = (tc > tr).astype(BF16)
        lfp_ref[...] = jnp.zeros_like(lfp_ref)
        run_ref[...] = lfn_ref[0]
        kn =jnp.broadcast_to(kn_ref[0], (V7X_SUBLANES, FOX_WIDTH)).astype(BF16)
        s = lax.dot_general(kn, qbd_ref[...], (((1,), (1,)), ((), ())), preferred_element_type=F32)
        row = lax.broadcasted_iota(jnp.int32, s.shape, 0)
        s = jnp.where(row == 0, s, -jnp.inf)
        fold(s, jnp.broadcast_to(vn_ref[0], (V7X_SUBLANES, FOX_WIDTH)), True)

    for i in range(npg):
        kb = k_refs[i][0].astype(BF16)
        s = lax.dot_general(kb, qbd_ref[...], (((1,), (1,)), ((), ())), preferred_element_type=F32)
        lfp_ref[:, 0:FOX_HEADS] = lf_refs[i][0]
        lf = lfp_ref[...]
        suf =sum(jnp.dot(w_ref[...], p.astype(BF16), preferred_element_type=F32) for p in _split3(lf))
        run = run_ref[...]
        fold(s + (suf + run), v_refs[i][0], False)
        run_ref[...] = run + suf[0:1, :] + lf[0:1, :]

    @pl.when(g == pl.num_programs(1) - 1)
    def _():
        l8 = _expand_exact(jnp.broadcast_to(l_ref[...], (V7X_SUBLANES, lanes)), e_ref[...])
        o_ref[0] = (jnp.sum(acc_ref[...], axis=0, keepdims=True) / l8[0:1]).astype(BF16)


def _attn_sample(q, k_new, v_new, lf_new, cache_k, cache_v, cache_logf, page_table):
    b = q.shape[0]
    n_pages = page_table.shape[1]
    npg = PAGES_PER_STEP
    nsteps = n_pages // npg
    n_phys = cache_k.shape[0]
    ck = cache_k.reshape(n_phys, PAGE_SIZE, FOX_WIDTH)
    cv = cache_v.reshape(n_phys, PAGE_SIZE, FOX_WIDTH)

    def page_map(slot):
        return lambda bi, g, pt: (pt[bi, n_pages - 1 - (g * npg + slot)], 0, 0)

    seq = lambda bi, g, pt: (bi, 0, 0)
    in_specs = [pl.BlockSpec((1, 1, FOX_WIDTH), seq)] * 3 + [pl.BlockSpec((1, 1, V7X_LANES), seq)]
    in_specs += [pl.BlockSpec((1, PAGE_SIZE, FOX_WIDTH), page_map(s)) for s in range(npg)]
    in_specs += [pl.BlockSpec((1, PAGE_SIZE, FOX_WIDTH), page_map(s)) for s in range(npg)]
    in_specs += [pl.BlockSpec((1, PAGE_SIZE, FOX_HEADS), page_map(s)) for s in range(npg)]
    grid_spec = pltpu.PrefetchScalarGridSpec(
        num_scalar_prefetch=1,
        grid=(b, nsteps),
        in_specs=in_specs,
        out_specs=pl.BlockSpec((1, 1, FOX_WIDTH), seq),
        scratch_shapes=[pltpu.VMEM((V7X_LANES, FOX_WIDTH), BF16),
                        pltpu.VMEM((V7X_LANES, FOX_WIDTH), BF16),
                        pltpu.VMEM((PAGE_SIZE, PAGE_SIZE), BF16),
                        pltpu.VMEM((PAGE_SIZE, V7X_LANES), F32),
                        pltpu.VMEM((1, V7X_LANES), F32),
                        pltpu.VMEM((1, V7X_LANES), F32),
                        pltpu.VMEM((1, V7X_LANES), F32),
                        pltpu.VMEM((V7X_SUBLANES, FOX_WIDTH), F32)])
    lfn = jnp.pad(lf_new, ((0, 0), (0, V7X_LANES - FOX_HEADS))).reshape(b, 1, V7X_LANES)
    out = pl.pallas_call(
        _attn_s_kernel,
        out_shape=jax.ShapeDtypeStruct((b, 1, FOX_WIDTH), BF16),
        grid_spec=grid_spec,
        compiler_params=_params("arbitrary", "arbitrary"),
        name="attn_sample",
    )(page_table, q.astype(F32).reshape(b, 1, -1), k_new.reshape(b, 1, -1), v_new.reshape(b, 1, -1), lfn,
      *([ck] * npg), *([cv] * npg), *([cache_logf] * npg))
    return out.reshape(b, FOX_WIDTH)


def _post_kernel(*refs, n_mix, final):
    x_ref = refs[0]
    mix_refs = refs[1:1 + 2 * n_mix]
    gm_ref, shf_ref, scf_ref, gf_ref, g_ref, wup_ref, wdn_ref = refs[1 + 2 * n_mix:8 + 2 * n_mix]
    rest = refs[8 + 2 * n_mix:]
    if final:
        gfin_ref, o_ref = rest
    else:
        (o_ref,) = rest
    mix = None
    for a in range(n_mix):
        t = jnp.dot(mix_refs[2 * a][...], mix_refs[2 * a + 1][...], preferred_element_type=F32)
        mix = t if mix is None else mix + t
    x1 = x_ref[...] + gm_ref[...] * mix
    h = _modnorm(x1, g_ref[...], shf_ref[...], scf_ref[...]).astype(BF16)
    fchunk = 1024
    acc = None
    for f0 in range(0, D_FF, fchunk):
        a = jnp.maximum(jnp.dot(h, wup_ref[:, f0:f0 + fchunk], preferred_element_type=F32), 0.0)
        t = jnp.dot((a * a).astype(BF16), wdn_ref[f0:f0 + fchunk, :], preferred_element_type=F32)
        acc = t if acc is None else acc + t
    x2 = x1 + gf_ref[...] * acc
    if final:
        ms = jnp.mean(x2 * x2, axis=-1, keepdims=True)
        x2 = x2 * lax.rsqrt(ms + EPS) * gfin_ref[...]
    o_ref[...] = x2


def _post(x, mixes, mod, g_mlp, w_up, w_down, final_g=None):
    t = x.shape[0]
    tm = min(ROW_TILE, t)
    mm = mod.shape[0]
    bm = 1 if mm == 1 else tm
    row = lambda i: (i, 0)

    def mcol(c):
        return (lambda i: (0, c)) if mm == 1 else (lambda i: (i, c))

    in_specs = [pl.BlockSpec((tm, D_MODEL), row)]
    args = [x]
    for a, w in mixes:
        in_specs += [pl.BlockSpec((tm, a.shape[1]), row), _const_spec(w.shape)]
        args += [a, w]
    in_specs += [pl.BlockSpec((bm, D_MODEL), mcol(2)), pl.BlockSpec((bm, D_MODEL), mcol(3)),
                 pl.BlockSpec((bm, D_MODEL), mcol(4)), pl.BlockSpec((bm, D_MODEL), mcol(5)),
                 _const_spec((1, D_MODEL)), _const_spec(w_up.shape), _const_spec(w_down.shape)]
    args += [mod, mod, mod, mod, g_mlp, w_up, w_down]
    if final_g is not None:
        in_specs.append(_const_spec((1, D_MODEL)))
        args.append(final_g)
    return pl.pallas_call(
        functools.partial(_post_kernel, n_mix=len(mixes), final=final_g is not None),
        out_shape=jax.ShapeDtypeStruct((t, D_MODEL), F32),
        grid=(t // tm,),
        in_specs=in_specs,
        out_specs=pl.BlockSpec((tm, D_MODEL), row),
        compiler_params=_params("arbitrary"),
        name="post",
    )(*args)


def _mix1_kernel(x_ref, sh_ref, sc_ref, g_ref, win_ref, wdw_ref, *rest, per_row_state):
    x = x_ref[...]
    tm = x.shape[0]
    h = _modnorm(x, g_ref[...], sh_ref[...], sc_ref[...]).astype(BF16)
    proj = jnp.dot(h, win_ref[...], preferred_element_type=F32)
    b_gate = proj[:, :D_MODEL]
    cx = proj[:, D_MODEL:2 * D_MODEL] * proj[:, 2 * D_MODEL:]
    w0, w1, w2 = wdw_ref[0:1, :], wdw_ref[1:2, :], wdw_ref[2:3, :]
    if per_row_state:
        s0_ref, s1_ref, by_ref, cx_ref = rest
        y = w0 * s0_ref[...] + w1 * s1_ref[...] + w2 * cx
        cx_ref[...] = cx
    else:
        by_ref, tail_ref, buf_ref = rest
        i = pl.program_id(0)
        pad = V7X_SUBLANES

        @pl.when(i == 0)
        def _():
            buf_ref[0:pad, :] = jnp.zeros((pad, D_MODEL), F32)

        buf_ref[pad:, :] = cx
        y = (w0 * buf_ref[pl.ds(pad - 2, tm), :] + w1 * buf_ref[pl.ds(pad - 1, tm), :] + w2 * cx)
        tail = cx[tm - pad:, :]
        buf_ref[0:pad, :] = tail
        tail_ref[...] = tail
    by_ref[...] = (b_gate * y).astype(BF16)


def _mix1(x, mod, g, w_in, w_dw, state=None):
    t = x.shape[0]
    tm = min(ROW_TILE, t)
    mm = mod.shape[0]
    bm = 1 if mm == 1 else tm
    row = lambda i: (i, 0)
    mcol = lambda c: (lambda i: (0, c)) if mm == 1 else (lambda i: (i, c))
    in_specs = [pl.BlockSpec((tm, D_MODEL), row),
                pl.BlockSpec((bm, D_MODEL), mcol(0)), pl.BlockSpec((bm, D_MODEL), mcol(1)),
                _const_spec((1, D_MODEL)), _const_spec(w_in.shape), _const_spec(w_dw.shape)]
    args = [x, mod, mod, g, w_in, w_dw]
    if state is not None:
        in_specs += [pl.BlockSpec((tm, D_MODEL), row)] * 2
        args += [state[0], state[1]]
        out_shape = [jax.ShapeDtypeStruct((t, D_MODEL), BF16), jax.ShapeDtypeStruct((t, D_MODEL), F32)]
        out_specs = [pl.BlockSpec((tm, D_MODEL), row)] * 2
        scratch = []
    else:
        nt = t // tm
        out_shape = [jax.ShapeDtypeStruct((t, D_MODEL), BF16),
                     jax.ShapeDtypeStruct((nt * V7X_SUBLANES, D_MODEL), F32)]
        out_specs = [pl.BlockSpec((tm, D_MODEL), row), pl.BlockSpec((V7X_SUBLANES, D_MODEL), row)]
        scratch = [pltpu.VMEM((tm + V7X_SUBLANES, D_MODEL), F32)]
    return pl.pallas_call(
        functools.partial(_mix1_kernel, per_row_state=state is not None),
        out_shape=out_shape,
        grid=(t // tm,),
        in_specs=in_specs,
        out_specs=out_specs,
        scratch_shapes=scratch,
        compiler_params=_params("arbitrary"),
        name="mix1",
    )(*args)


def kernel(x_prompt, x_sample, cache_k, cache_v, cache_logf, state_conformer_conv, state_short_conv, page_table, c_prompt, c_sample, l0_w_ada, l0_b_ada, l0_norm_mix, l0_norm_mlp, l0_w_in, l0_b_forget, l0_w_dw, l0_b_dw, l0_conv_ln_g, l0_conv_ln_b, l0_w_out, l0_w_up, l0_w_down, l1_w_ada, l1_b_ada, l1_norm_mix, l1_norm_mlp, l1_w_in, l1_w_dw, l1_w_out, l1_w_up, l1_w_down, final_norm):
    bp, t, d = x_prompt.shape
    bs = x_sample.shape[0]
    assert bp == 1 and d == D_MODEL and x_sample.shape[1] == 1

    w0 = {
        "qkv": l0_w_in[:, :3 * FOX_WIDTH].astype(BF16),
        "f": jnp.pad(l0_w_in[:, 3 * FOX_WIDTH:3 * FOX_WIDTH + FOX_HEADS],
                     ((0, 0), (0, V7X_LANES - FOX_HEADS))).astype(BF16),
        "bf": jnp.pad(l0_b_forget, (0, V7X_LANES - FOX_HEADS)).reshape(1, V7X_LANES),
        "glu": l0_w_in[:, 3 * FOX_WIDTH + FOX_HEADS:].astype(BF16),
    }
    w0_out_attn = l0_w_out[:FOX_WIDTH].astype(BF16)
    w0_out_conv = l0_w_out[FOX_WIDTH:].astype(BF16)
    w0_up, w0_down = l0_w_up.astype(BF16), l0_w_down.astype(BF16)
    w1_in, w1_out = l1_w_in.astype(BF16), l1_w_out.astype(BF16)
    w1_up, w1_down = l1_w_up.astype(BF16), l1_w_down.astype(BF16)
    g0_mix, g0_mlp = l0_norm_mix.reshape(1, d), l0_norm_mlp.reshape(1, d)
    g1_mix, g1_mlp = l1_norm_mix.reshape(1, d), l1_norm_mlp.reshape(1, d)
    gfin = final_norm.reshape(1, d)

    c_all = jnp.concatenate([c_prompt, c_sample], axis=0)
    mod0 = _ada(c_all, l0_w_ada, l0_b_ada)
    mod1 = _ada(c_all, l1_w_ada, l1_b_ada)
    mod0_p, mod0_s = mod0[:1], mod0[1:]
    mod1_p, mod1_s = mod1[:1], mod1[1:]

    xp = x_prompt.reshape(t, d)
    q_p, k_p, v_p, kb_p, vb_p, lf_p, u_p, ft_p = _proj0(xp, mod0_p, g0_mix, w0, True)
    yc_p = _conv0_prompt(u_p, l0_w_dw, l0_b_dw, l0_conv_ln_g, l0_conv_ln_b)
    at_p = _attn_prompt(q_p, kb_p, vb_p, ft_p)
    x1_p = _post(xp, [(at_p, w0_out_attn), (yc_p, w0_out_conv)], mod0_p, g0_mlp, w0_up, w0_down)
    by_p, tail_p = _mix1(x1_p, mod1_p, g1_mix, w1_in, l1_w_dw)
    y_p = _post(x1_p, [(by_p, w1_out)], mod1_p, g1_mlp, w1_up, w1_down, gfin)

    xs = x_sample.reshape(bs, d)
    q_s, k_s, v_s, _, _, lf_s, u_s = _proj0(xs, mod0_s, g0_mix, w0, False)
    yc_s = _conv0_sample(state_conformer_conv.transpose(1, 0, 2), u_s,
                         l0_w_dw, l0_b_dw, l0_conv_ln_g, l0_conv_ln_b)
    at_s = _attn_sample(q_s, k_s, v_s, lf_s, cache_k, cache_v, cache_logf, page_table)
    x1_s = _post(xs, [(at_s, w0_out_attn), (yc_s, w0_out_conv)], mod0_s, g0_mlp, w0_up, w0_down)
    by_s, cx_s = _mix1(x1_s, mod1_s, g1_mix, w1_in, l1_w_dw,
                       state=(state_short_conv[:, 0], state_short_conv[:, 1]))
    y_s = _post(x1_s, [(by_s, w1_out)], mod1_s, g1_mlp, w1_up, w1_down, gfin)

    hs = (FOX_HEADS, HEAD_DIM)
    return (y_p.reshape(1, t, d), y_s.reshape(bs, 1, d),
            k_p.reshape(1, t, *hs), v_p.reshape(1, t, *hs), lf_p.reshape(1, t, FOX_HEADS),
            u_p[t - (CONF_CONV_WIDTH - 1):].reshape(1, CONF_CONV_WIDTH - 1, CONF_CH),
            tail_p[-(SC_CONV_WIDTH - 1):].reshape(1, SC_CONV_WIDTH - 1, d),
            k_s.reshape(bs, 1, *hs), v_s.reshape(bs, 1, *hs), lf_s.reshape(bs, 1, FOX_HEADS),
            jnp.concatenate([state_conformer_conv[:, 1:], u_s[:, None, :]], axis=1),
            jnp.stack([state_short_conv[:, 1], cx_s], axis=1))
```

```python
import functools

import jax
import jax.numpy as jnp
from jax import lax
from jax.experimental import pallas as pl
from jax.experimental.pallas import tpu as pltpu

F32 = jnp.float32
BF16 = jnp.bfloat16

D_MODEL = 1024
FOX_HEADS = 8
HEAD_DIM = 64
FOX_WIDTH = FOX_HEADS * HEAD_DIM
CONF_CH = D_MODEL - FOX_WIDTH
CONF_CONV_WIDTH = 31
SC_CONV_WIDTH = 3
D_FF = 4 * D_MODEL
PAGE_SIZE = 128
EPS = 1e-6
ATTN_SCALE = HEAD_DIM ** -0.5

V7X_LANES = 128
V7X_SUBLANES = 8
V7X_VMEM_BYTES = 64 * 1024 * 1024
VMEM_LIMIT = 52 * 1024 * 1024

ROW_TILE = 512
CONV_TILE = 256
CONV_CHUNK = 64
CONV_HALO = 32
ATTN_TILE = 256
AUG = V7X_LANES
PAGES_PER_STEP = 8


def _const_spec(shape):
    nd = len(shape)
    return pl.BlockSpec(shape, lambda *_: (0,) * nd, pipeline_mode=pl.Buffered(1))


def _params(*sem):
    return pltpu.CompilerParams(dimension_semantics=sem, vmem_limit_bytes=VMEM_LIMIT)


def _modnorm(x, g, shift, scale):
    ms = jnp.mean(x * x, axis=-1, keepdims=True)
    y = x * lax.rsqrt(ms + EPS)
    return (y * g) * (1.0 + scale) + shift


def _split3(x):
    hi = x.astype(BF16).astype(F32)
    r = x - hi
    mid = r.astype(BF16).astype(F32)
    lo = (r - mid).astype(BF16).astype(F32)
    return hi, mid, lo


def _ada_kernel(c_ref, w_ref, b_ref, o_ref):
    c = c_ref[...]
    s = (c * jax.nn.sigmoid(c)).astype(BF16)
    o_ref[...] = jnp.dot(s, w_ref[...].astype(BF16), preferred_element_type=F32) + b_ref[...]


def _ada(c_all, w_ada, b_ada):
    m = c_all.shape[0]
    n = w_ada.shape[1]
    bn = 1536
    return pl.pallas_call(
        _ada_kernel,
        out_shape=jax.ShapeDtypeStruct((m, n), F32),
        grid=(n // bn,),
        in_specs=[pl.BlockSpec((m, D_MODEL), lambda j: (0, 0)),
                  pl.BlockSpec((D_MODEL, bn), lambda j: (0, j)),
                  pl.BlockSpec((1, bn), lambda j: (0, j))],
        out_specs=pl.BlockSpec((m, bn), lambda j: (0, j)),
        compiler_params=_params("arbitrary"),
        name="ada",
    )(c_all, w_ada, b_ada.reshape(1, n))


def _bias_placement():
    import numpy as np
    pq = np.zeros((V7X_LANES, FOX_HEADS * AUG), np.float32)
    pk = np.zeros((V7X_LANES, FOX_HEADS * AUG), np.float32)
    for h in range(FOX_HEADS):
        for p in range(3):
            pq[p * FOX_HEADS + h, h * AUG + HEAD_DIM + p] = 1.0
            pk[p * FOX_HEADS + h, h * AUG + HEAD_DIM + 3 + p] = -1.0
            pq[3 * FOX_HEADS, h * AUG + HEAD_DIM + 3 + p] = 1.0
            pk[3 * FOX_HEADS, h * AUG + HEAD_DIM + p] = 1.0
    return jnp.asarray(pq, BF16), jnp.asarray(pk, BF16)


def _proj0_kernel(*refs, prompt):
    x_ref, sh_ref, sc_ref, g_ref, wq_ref, wkv_ref, wf_ref, bf_ref, wglu_ref = refs[:9]
    x = x_ref[...]
    h = _modnorm(x, g_ref[...], sh_ref[...], sc_ref[...]).astype(BF16)
    kv = jnp.dot(h, wkv_ref[...], preferred_element_type=F32)
    k = kv[:, :FOX_WIDTH]
    v = kv[:, FOX_WIDTH:]
    fg = jnp.dot(h, wf_ref[...], preferred_element_type=F32) + bf_ref[...]
    lf = jnp.minimum(fg, 0.0) - jnp.log1p(jnp.exp(-jnp.abs(fg)))
    glu = jnp.dot(h, wglu_ref[...], preferred_element_type=F32)
    u = glu[:, :CONF_CH] * jax.nn.sigmoid(glu[:, CONF_CH:])
    qp = jnp.dot(h, wq_ref[...], preferred_element_type=F32)

    if not prompt:
        q_ref, k_ref, v_ref, lf_ref, u_ref = refs[9:]
        q_ref[...] = qp
    else:
        wk_ref, pq_ref, pk_ref, qa_ref, ka_ref, vt_ref, k_ref, v_ref, lf_ref, u_ref, tri_ref, carry_ref = refs[9:]
        tm = x.shape[0]
        tk = vt_ref.shape[2]
        i = pl.program_id(0)

        @pl.when(i == 0)
        def _():
            carry_ref[...] = jnp.zeros_like(carry_ref)
            r = lax.broadcasted_iota(jnp.int32, (tm, tm), 0)
            c = lax.broadcasted_iota(jnp.int32, (tm, tm), 1)
            tri_ref[...] = (r <= c).astype(BF16)

        lft = lf.T[:FOX_HEADS]
        parts = jnp.concatenate(_split3(lft), axis=0).astype(BF16)
        cs = jnp.dot(parts, tri_ref[...], preferred_element_type=F32)
        ft = cs[0:8] + cs[8:16] + cs[16:24] + carry_ref[...]
        carry_ref[...] = ft[:, tm - 1:tm]

        pieces_t = jnp.concatenate(
            list(_split3(ft)) + [jnp.ones((FOX_HEADS, tm), F32),
                                 jnp.zeros((V7X_LANES - 4 * FOX_HEADS, tm), F32)], axis=0)
        pieces = pieces_t.T.astype(BF16)
        qa_ref[...] = (qp + jnp.dot(pieces, pq_ref[...], preferred_element_type=F32)).astype(BF16)
        kp = jnp.dot(h, wk_ref[...], preferred_element_type=F32)
        ka_ref[...] = (kp + jnp.dot(pieces, pk_ref[...], preferred_element_type=F32)).astype(BF16)
        vtt = v.T.astype(BF16)
        for c0 in range(tm // tk):
            vt_ref[c0] = vtt[:, c0 * tk:(c0 + 1) * tk]

    k_ref[...] = k
    v_ref[...] = v
    lf_ref[...] = lf[:, :FOX_HEADS]
    u_ref[...] = u


def _proj0(x, mod, g, w, prompt):
    t = x.shape[0]
    tm = min(ROW_TILE, t)
    mm = mod.shape[0]
    mrow = (lambda i: (0, 0)) if mm == 1 else (lambda i: (i, 0))
    mrow1 = (lambda i: (0, 1)) if mm == 1 else (lambda i: (i, 1))
    bm = 1 if mm == 1 else tm
    row = lambda i: (i, 0)
    wide = FOX_HEADS * AUG
    in_specs = [pl.BlockSpec((tm, D_MODEL), row),
                pl.BlockSpec((bm, D_MODEL), mrow),
                pl.BlockSpec((bm, D_MODEL), mrow1),
                _const_spec((1, D_MODEL)),
                _const_spec(w["q"].shape),
                _const_spec(w["kv"].shape),
                _const_spec(w["f"].shape),
                _const_spec((1, V7X_LANES)),
                _const_spec(w["glu"].shape)]
    args = [x, mod, mod, g, w["q"], w["kv"], w["f"], w["bf"], w["glu"]]
    tail_shape = [jax.ShapeDtypeStruct((t, FOX_WIDTH), F32), jax.ShapeDtypeStruct((t, FOX_WIDTH), F32),
                  jax.ShapeDtypeStruct((t, FOX_HEADS), F32), jax.ShapeDtypeStruct((t, CONF_CH), F32)]
    tail_specs = [pl.BlockSpec((tm, FOX_WIDTH), row), pl.BlockSpec((tm, FOX_WIDTH), row),
                  pl.BlockSpec((tm, FOX_HEADS), row), pl.BlockSpec((tm, CONF_CH), row)]
    if prompt:
        tk = ATTN_TILE
        pq, pk = _bias_placement()
        in_specs += [_const_spec(w["k"].shape), _const_spec(pq.shape), _const_spec(pk.shape)]
        args += [w["k"], pq, pk]
        out_shape = [jax.ShapeDtypeStruct((t, wide), BF16), jax.ShapeDtypeStruct((t, wide), BF16),
                     jax.ShapeDtypeStruct((t // tk, FOX_WIDTH, tk), BF16)] + tail_shape
        out_specs = [pl.BlockSpec((tm, wide), row), pl.BlockSpec((tm, wide), row),
                     pl.BlockSpec((tm // tk, FOX_WIDTH, tk), lambda i: (i, 0, 0))] + tail_specs
        scratch = [pltpu.VMEM((tm, tm), BF16), pltpu.VMEM((FOX_HEADS, 1), F32)]
    else:
        out_shape = [jax.ShapeDtypeStruct((t, wide), F32)] + tail_shape
        out_specs = [pl.BlockSpec((tm, wide), row)] + tail_specs
        scratch = []
    return pl.pallas_call(
        functools.partial(_proj0_kernel, prompt=prompt),
        out_shape=out_shape,
        grid=(t // tm,),
        in_specs=in_specs,
        out_specs=out_specs,
        scratch_shapes=scratch,
        compiler_params=_params("arbitrary"),
        name="proj0",
    )(*args)


def _ln_silu(y, ln_g, ln_b):
    mu = jnp.mean(y, axis=-1, keepdims=True)
    d = y - mu
    var = jnp.mean(d * d, axis=-1, keepdims=True)
    z = d * lax.rsqrt(var + EPS) * ln_g + ln_b
    return z * jax.nn.sigmoid(z)


def _conv0_kernel(u_ref, up_ref, wdw_ref, bdw_ref, lng_ref, lnb_ref, y_ref, buf_ref):
    i = pl.program_id(0)
    tm = u_ref.shape[0]
    buf_ref[0:CONV_HALO, :] = jnp.where(i == 0, 0.0, up_ref[...])
    buf_ref[CONV_HALO:, :] = u_ref[...]
    first = CONV_HALO - (CONF_CONV_WIDTH - 1)
    for c0 in range(0, tm, CONV_CHUNK):
        acc = jnp.zeros((CONV_CHUNK, CONF_CH), F32)
        for kk in range(CONF_CONV_WIDTH):
            acc = acc + wdw_ref[kk:kk + 1, :] * buf_ref[pl.ds(first + c0 + kk, CONV_CHUNK), :]
        z = _ln_silu(acc + bdw_ref[...], lng_ref[...], lnb_ref[...])
        y_ref[c0:c0 + CONV_CHUNK, :] = z.astype(BF16)


def _conv0_prompt(u, w_dw, b_dw, ln_g, ln_b):
    t = u.shape[0]
    tm = CONV_TILE
    per = tm // CONV_HALO
    return pl.pallas_call(
        _conv0_kernel,
        out_shape=jax.ShapeDtypeStruct((t, CONF_CH), BF16),
        grid=(t // tm,),
        in_specs=[pl.BlockSpec((tm, CONF_CH), lambda i: (i, 0)),
                  pl.BlockSpec((CONV_HALO, CONF_CH), lambda i: (jnp.maximum(i * per - 1, 0), 0)),
                  _const_spec((CONF_CONV_WIDTH, CONF_CH)),
                  _const_spec((1, CONF_CH)), _const_spec((1, CONF_CH)), _const_spec((1, CONF_CH))],
        out_specs=pl.BlockSpec((tm, CONF_CH), lambda i: (i, 0)),
        scratch_shapes=[pltpu.VMEM((tm + CONV_HALO, CONF_CH), F32)],
        compiler_params=_params("arbitrary"),
        name="conv0",
    )(u, u, w_dw, b_dw.reshape(1, -1), ln_g.reshape(1, -1), ln_b.reshape(1, -1))


def _conv0s_kernel(st_ref, u_ref, wdw_ref, bdw_ref, lng_ref, lnb_ref, y_ref):
    acc = wdw_ref[CONF_CONV_WIDTH - 1:CONF_CONV_WIDTH, :] * u_ref[...]
    for kk in range(CONF_CONV_WIDTH - 1):
        acc = acc + wdw_ref[kk:kk + 1, :] * st_ref[kk]
    z = _ln_silu(acc + bdw_ref[...], lng_ref[...], lnb_ref[...])
    y_ref[...] = z.astype(BF16)


def _conv0_sample(state_t, u, w_dw, b_dw, ln_g, ln_b):
    b = u.shape[0]
    return pl.pallas_call(
        _conv0s_kernel,
        out_shape=jax.ShapeDtypeStruct((b, CONF_CH), BF16),
        name="conv0s",
    )(state_t, u, w_dw, b_dw.reshape(1, -1), ln_g.reshape(1, -1), ln_b.reshape(1, -1))


def _attn_kernel(qa_ref, ka_ref, vt_ref, o_ref, s_ref):
    i = pl.program_id(1)
    tq = qa_ref.shape[0]
    heads = (0, 1)
    qa = [qa_ref[:, hh * AUG:(hh + 1) * AUG] for hh in heads]

    def scores(j):
        start = pl.multiple_of(j * tq, tq)
        return [lax.dot_general(ka_ref[pl.ds(start, tq), hh * AUG:(hh + 1) * AUG], qa[hh],
                                (((1,), (1,)), ((), ())), preferred_element_type=F32)
                for hh in heads]

    def fold(j, carry, masked):
        out = []
        for hh in heads:
            m_prev, l_prev, acc = carry[hh]
            st = s_ref[j % 2, hh]
            if masked:
                r = lax.broadcasted_iota(jnp.int32, st.shape, 0)
                c = lax.broadcasted_iota(jnp.int32, st.shape, 1)
                st = jnp.where(r <= c, st, -jnp.inf)
            m_new = jnp.maximum(m_prev, jnp.max(st, axis=0, keepdims=True))
            alpha = jnp.exp(m_prev - m_new)
            pt = jnp.exp(st - m_new)
            l_new = alpha * l_prev + jnp.sum(pt, axis=0, keepdims=True)
            vt = vt_ref[j, hh * HEAD_DIM:(hh + 1) * HEAD_DIM, :]
            acc = acc * alpha + jnp.dot(vt, pt.astype(BF16), preferred_element_type=F32)
            out.append((m_new, l_new, acc))
        return tuple(out)

    def body(j, carry):
        nxt = scores(j + 1)
        carry = fold(j, carry, False)
        for hh in heads:
            s_ref[(j + 1) % 2, hh] = nxt[hh]
        return carry

    first = scores(0)
    for hh in heads:
        s_ref[0, hh] = first[hh]
    init = tuple((jnp.full((1, tq), -jnp.inf, F32), jnp.zeros((1, tq), F32),
                  jnp.zeros((HEAD_DIM, tq), F32)) for _ in heads)
    carry = lax.fori_loop(0, i, body, init)
    carry = fold(i, carry, True)
    ot = jnp.concatenate([acc / l for (_, l, acc) in carry], axis=0)
    o_ref[...] = ot.T.astype(BF16)


def _attn_prompt(qa, ka, vt):
    t = qa.shape[0]
    tq = ATTN_TILE
    npair = FOX_HEADS // 2
    nblk = t // tq
    return pl.pallas_call(
        _attn_kernel,
        out_shape=jax.ShapeDtypeStruct((t, FOX_WIDTH), BF16),
        grid=(npair, nblk),
        in_specs=[pl.BlockSpec((tq, 2 * AUG), lambda p, i: (i, p)),
                  pl.BlockSpec((t, 2 * AUG), lambda p, i: (0, p)),
                  pl.BlockSpec((nblk, 2 * HEAD_DIM, tq), lambda p, i: (0, p, 0))],
        out_specs=pl.BlockSpec((tq, 2 * HEAD_DIM), lambda p, i: (i, p)),
        scratch_shapes=[pltpu.VMEM((2, 2, tq, tq), F32)],
        compiler_params=_params("arbitrary", "arbitrary"),
        name="attn_prompt",
    )(qa, ka, vt)


def _expand_exact(x, e):
    return sum(jnp.dot(p.astype(BF16), e, preferred_element_type=F32) for p in _split3(x))


def _attn_s_kernel(pt_ref, q_ref, kn_ref, vn_ref, lfn_ref, *rest):
    del pt_ref
    npg = PAGES_PER_STEP
    k_refs = rest[:npg]
    v_refs = rest[npg:2 * npg]
    lf_refs = rest[2 * npg:3 * npg]
    o_ref, qbd_ref, e_ref, w_ref, lfp_ref, m_ref, l_ref, run_ref, acc_ref = rest[3 * npg:]
    g = pl.program_id(1)
    lanes = V7X_LANES

    def fold(s, v, first):
        m_new = jnp.max(s, axis=0, keepdims=True)
        if not first:
            m_prev = m_ref[...]
            m_new = jnp.maximum(m_prev, m_new)
        p = jnp.exp(s - m_new)
        pe = jnp.dot(p.astype(BF16), e_ref[...], preferred_element_type=F32)
        pv = pe * v
        pv8 = pv.reshape(-1, V7X_SUBLANES, FOX_WIDTH).sum(axis=0)
        psum = jnp.sum(p, axis=0, keepdims=True)
        if first:
            l_ref[...] = psum
            acc_ref[...] = pv8
        else:
            alpha = jnp.exp(m_prev - m_new)
            l_ref[...] = alpha * l_ref[...] + psum
            a8 = _expand_exact(jnp.broadcast_to(alpha, (V7X_SUBLANES, lanes)), e_ref[...])
            acc_ref[...] = acc_ref[...] * a8 + pv8
        m_ref[...] = m_new

    @pl.when(g == 0)
    def _():
        hrow = lax.broadcasted_iota(jnp.int32, (lanes, FOX_WIDTH), 0)
        ccol = lax.broadcasted_iota(jnp.int32, (lanes, FOX_WIDTH), 1)
        sel = hrow == ccol // HEAD_DIM
        e_ref[...] = sel.astype(BF16)
        qbd_ref[...] = jnp.where(sel, jnp.broadcast_to(q_ref[0], (lanes, FOX_WIDTH)), 0.0).astype(BF16)
        tr = lax.broadcasted_iota(jnp.int32, (PAGE_SIZE, PAGE_SIZE), 0)
        tc = lax.broadcasted_iota(jnp.int32, (PAGE_SIZE, PAGE_SIZE), 1)
        w_ref[...] = (tc > tr).astype(BF16)
        lfp_ref[...] = jnp.zeros_like(lfp_ref)
        run_ref[...] = lfn_ref[0]
        kn =jnp.broadcast_to(kn_ref[0], (V7X_SUBLANES, FOX_WIDTH)).astype(BF16)
        s = lax.dot_general(kn, qbd_ref[...], (((1,), (1,)), ((), ())), preferred_element_type=F32)
        row = lax.broadcasted_iota(jnp.int32, s.shape, 0)
        s = jnp.where(row == 0, s, -jnp.inf)
        fold(s, jnp.broadcast_to(vn_ref[0], (V7X_SUBLANES, FOX_WIDTH)), True)

    for i in range(npg):
        kb = k_refs[i][0].astype(BF16)
        s = lax.dot_general(kb, qbd_ref[...], (((1,), (1,)), ((), ())), preferred_element_type=F32)
        lfp_ref[:, 0:FOX_HEADS] = lf_refs[i][0]
        lf = lfp_ref[...]
        suf =sum(jnp.dot(w_ref[...], p.astype(BF16), preferred_element_type=F32) for p in _split3(lf))
        run = run_ref[...]
        fold(s + (suf + run), v_refs[i][0], False)
        run_ref[...] = run + suf[0:1, :] + lf[0:1, :]

    @pl.when(g == pl.num_programs(1) - 1)
    def _():
        l8 = _expand_exact(jnp.broadcast_to(l_ref[...], (V7X_SUBLANES, lanes)), e_ref[...])
        o_ref[0] = (jnp.sum(acc_ref[...], axis=0, keepdims=True) / l8[0:1]).astype(BF16)


def _attn_sample(q, k_new, v_new, lf_new, cache_k, cache_v, cache_logf, page_table):
    b = q.shape[0]
    n_pages = page_table.shape[1]
    npg = PAGES_PER_STEP
    nsteps = n_pages // npg
    n_phys = cache_k.shape[0]
    ck = cache_k.reshape(n_phys, PAGE_SIZE, FOX_WIDTH)
    cv = cache_v.reshape(n_phys, PAGE_SIZE, FOX_WIDTH)

    def page_map(slot):
        return lambda bi, g, pt: (pt[bi, n_pages - 1 - (g * npg + slot)], 0, 0)

    seq = lambda bi, g, pt: (bi, 0, 0)
    in_specs = [pl.BlockSpec((1, 1, FOX_WIDTH), seq)] * 3 + [pl.BlockSpec((1, 1, V7X_LANES), seq)]
    in_specs += [pl.BlockSpec((1, PAGE_SIZE, FOX_WIDTH), page_map(s)) for s in range(npg)]
    in_specs += [pl.BlockSpec((1, PAGE_SIZE, FOX_WIDTH), page_map(s)) for s in range(npg)]
    in_specs += [pl.BlockSpec((1, PAGE_SIZE, FOX_HEADS), page_map(s)) for s in range(npg)]
    grid_spec = pltpu.PrefetchScalarGridSpec(
        num_scalar_prefetch=1,
        grid=(b, nsteps),
        in_specs=in_specs,
        out_specs=pl.BlockSpec((1, 1, FOX_WIDTH), seq),
        scratch_shapes=[pltpu.VMEM((V7X_LANES, FOX_WIDTH), BF16),
                        pltpu.VMEM((V7X_LANES, FOX_WIDTH), BF16),
                        pltpu.VMEM((PAGE_SIZE, PAGE_SIZE), BF16),
                        pltpu.VMEM((PAGE_SIZE, V7X_LANES), F32),
                        pltpu.VMEM((1, V7X_LANES), F32),
                        pltpu.VMEM((1, V7X_LANES), F32),
                        pltpu.VMEM((1, V7X_LANES), F32),
                        pltpu.VMEM((V7X_SUBLANES, FOX_WIDTH), F32)])
    lfn = jnp.pad(lf_new, ((0, 0), (0, V7X_LANES - FOX_HEADS))).reshape(b, 1, V7X_LANES)
    out = pl.pallas_call(
        _attn_s_kernel,
        out_shape=jax.ShapeDtypeStruct((b, 1, FOX_WIDTH), BF16),
        grid_spec=grid_spec,
        compiler_params=_params("arbitrary", "arbitrary"),
        name="attn_sample",
    )(page_table, q.astype(F32).reshape(b, 1, -1), k_new.reshape(b, 1, -1), v_new.reshape(b, 1, -1), lfn,
      *([ck] * npg), *([cv] * npg), *([cache_logf] * npg))
    return out.reshape(b, FOX_WIDTH)


def _post_kernel(*refs, n_mix, final):
    x_ref = refs[0]
    mix_refs = refs[1:1 + 2 * n_mix]
    gm_ref, shf_ref, scf_ref, gf_ref, g_ref, wup_ref, wdn_ref = refs[1 + 2 * n_mix:8 + 2 * n_mix]
    rest = refs[8 + 2 * n_mix:]
    if final:
        gfin_ref, o_ref = rest
    else:
        (o_ref,) = rest
    mix = None
    for a in range(n_mix):
        t = jnp.dot(mix_refs[2 * a][...], mix_refs[2 * a + 1][...], preferred_element_type=F32)
        mix = t if mix is None else mix + t
    x1 = x_ref[...] + gm_ref[...] * mix
    h = _modnorm(x1, g_ref[...], shf_ref[...], scf_ref[...]).astype(BF16)
    fchunk = 1024
    acc = None
    for f0 in range(0, D_FF, fchunk):
        a = jnp.maximum(jnp.dot(h, wup_ref[:, f0:f0 + fchunk], preferred_element_type=F32), 0.0)
        t = jnp.dot((a * a).astype(BF16), wdn_ref[f0:f0 + fchunk, :], preferred_element_type=F32)
        acc = t if acc is None else acc + t
    x2 = x1 + gf_ref[...] * acc
    if final:
        ms = jnp.mean(x2 * x2, axis=-1, keepdims=True)
        x2 = x2 * lax.rsqrt(ms + EPS) * gfin_ref[...]
    o_ref[...] = x2


def _post(x, mixes, mod, g_mlp, w_up, w_down, final_g=None):
    t = x.shape[0]
    tm = min(ROW_TILE, t)
    mm = mod.shape[0]
    bm = 1 if mm == 1 else tm
    row = lambda i: (i, 0)

    def mcol(c):
        return (lambda i: (0, c)) if mm == 1 else (lambda i: (i, c))

    in_specs = [pl.BlockSpec((tm, D_MODEL), row)]
    args = [x]
    for a, w in mixes:
        in_specs += [pl.BlockSpec((tm, a.shape[1]), row), _const_spec(w.shape)]
        args += [a, w]
    in_specs += [pl.BlockSpec((bm, D_MODEL), mcol(2)), pl.BlockSpec((bm, D_MODEL), mcol(3)),
                 pl.BlockSpec((bm, D_MODEL), mcol(4)), pl.BlockSpec((bm, D_MODEL), mcol(5)),
                 _const_spec((1, D_MODEL)), _const_spec(w_up.shape), _const_spec(w_down.shape)]
    args += [mod, mod, mod, mod, g_mlp, w_up, w_down]
    if final_g is not None:
        in_specs.append(_const_spec((1, D_MODEL)))
        args.append(final_g)
    return pl.pallas_call(
        functools.partial(_post_kernel, n_mix=len(mixes), final=final_g is not None),
        out_shape=jax.ShapeDtypeStruct((t, D_MODEL), F32),
        grid=(t // tm,),
        in_specs=in_specs,
        out_specs=pl.BlockSpec((tm, D_MODEL), row),
        compiler_params=_params("arbitrary"),
        name="post",
    )(*args)


def _mix1_kernel(x_ref, sh_ref, sc_ref, g_ref, win_ref, wdw_ref, *rest, per_row_state):
    x = x_ref[...]
    tm = x.shape[0]
    h = _modnorm(x, g_ref[...], sh_ref[...], sc_ref[...]).astype(BF16)
    proj = jnp.dot(h, win_ref[...], preferred_element_type=F32)
    b_gate = proj[:, :D_MODEL]
    cx = proj[:, D_MODEL:2 * D_MODEL] * proj[:, 2 * D_MODEL:]
    w0, w1, w2 = wdw_ref[0:1, :], wdw_ref[1:2, :], wdw_ref[2:3, :]
    if per_row_state:
        s0_ref, s1_ref, by_ref, cx_ref = rest
        y = w0 * s0_ref[...] + w1 * s1_ref[...] + w2 * cx
        cx_ref[...] = cx
    else:
        by_ref, tail_ref, buf_ref = rest
        i = pl.program_id(0)
        pad = V7X_SUBLANES

        @pl.when(i == 0)
        def _():
            buf_ref[0:pad, :] = jnp.zeros((pad, D_MODEL), F32)

        buf_ref[pad:, :] = cx
        y = (w0 * buf_ref[pl.ds(pad - 2, tm), :] + w1 * buf_ref[pl.ds(pad - 1, tm), :] + w2 * cx)
        tail = cx[tm - pad:, :]
        buf_ref[0:pad, :] = tail
        tail_ref[...] = tail
    by_ref[...] = (b_gate * y).astype(BF16)


def _mix1(x, mod, g, w_in, w_dw, state=None):
    t = x.shape[0]
    tm = min(ROW_TILE, t)
    mm = mod.shape[0]
    bm = 1 if mm == 1 else tm
    row = lambda i: (i, 0)
    mcol = lambda c: (lambda i: (0, c)) if mm == 1 else (lambda i: (i, c))
    in_specs = [pl.BlockSpec((tm, D_MODEL), row),
                pl.BlockSpec((bm, D_MODEL), mcol(0)), pl.BlockSpec((bm, D_MODEL), mcol(1)),
                _const_spec((1, D_MODEL)), _const_spec(w_in.shape), _const_spec(w_dw.shape)]
    args = [x, mod, mod, g, w_in, w_dw]
    if state is not None:
        in_specs += [pl.BlockSpec((tm, D_MODEL), row)] * 2
        args += [state[0], state[1]]
        out_shape = [jax.ShapeDtypeStruct((t, D_MODEL), BF16), jax.ShapeDtypeStruct((t, D_MODEL), F32)]
        out_specs = [pl.BlockSpec((tm, D_MODEL), row)] * 2
        scratch = []
    else:
        nt = t // tm
        out_shape = [jax.ShapeDtypeStruct((t, D_MODEL), BF16),
                     jax.ShapeDtypeStruct((nt * V7X_SUBLANES, D_MODEL), F32)]
        out_specs = [pl.BlockSpec((tm, D_MODEL), row), pl.BlockSpec((V7X_SUBLANES, D_MODEL), row)]
        scratch = [pltpu.VMEM((tm + V7X_SUBLANES, D_MODEL), F32)]
    return pl.pallas_call(
        functools.partial(_mix1_kernel, per_row_state=state is not None),
        out_shape=out_shape,
        grid=(t // tm,),
        in_specs=in_specs,
        out_specs=out_specs,
        scratch_shapes=scratch,
        compiler_params=_params("arbitrary"),
        name="mix1",
    )(*args)


def kernel(x_prompt, x_sample, cache_k, cache_v, cache_logf, state_conformer_conv, state_short_conv, page_table, c_prompt, c_sample, l0_w_ada, l0_b_ada, l0_norm_mix, l0_norm_mlp, l0_w_in, l0_b_forget, l0_w_dw, l0_b_dw, l0_conv_ln_g, l0_conv_ln_b, l0_w_out, l0_w_up, l0_w_down, l1_w_ada, l1_b_ada, l1_norm_mix, l1_norm_mlp, l1_w_in, l1_w_dw, l1_w_out, l1_w_up, l1_w_down, final_norm):
    bp, t, d = x_prompt.shape
    bs = x_sample.shape[0]
    assert bp == 1 and d == D_MODEL and x_sample.shape[1] == 1

    def per_head_padded(wcols, scale):
        w3 = (wcols * scale).reshape(d, FOX_HEADS, HEAD_DIM)
        return jnp.pad(w3, ((0, 0), (0, 0), (0, AUG - HEAD_DIM))).reshape(d, FOX_HEADS * AUG).astype(BF16)

    w0 = {
        "q": per_head_padded(l0_w_in[:, :FOX_WIDTH], ATTN_SCALE),
        "k": per_head_padded(l0_w_in[:, FOX_WIDTH:2 * FOX_WIDTH], 1.0),
        "kv": l0_w_in[:, FOX_WIDTH:3 * FOX_WIDTH].astype(BF16),
        "f":jnp.pad(l0_w_in[:, 3 * FOX_WIDTH:3 * FOX_WIDTH + FOX_HEADS],
                     ((0, 0), (0, V7X_LANES - FOX_HEADS))).astype(BF16),
        "bf": jnp.pad(l0_b_forget, (0, V7X_LANES - FOX_HEADS)).reshape(1, V7X_LANES),
        "glu": l0_w_in[:, 3 * FOX_WIDTH + FOX_HEADS:].astype(BF16),
    }
    w0_out_attn = l0_w_out[:FOX_WIDTH].astype(BF16)
    w0_out_conv = l0_w_out[FOX_WIDTH:].astype(BF16)
    w0_up, w0_down = l0_w_up.astype(BF16), l0_w_down.astype(BF16)
    w1_in, w1_out = l1_w_in.astype(BF16), l1_w_out.astype(BF16)
    w1_up, w1_down = l1_w_up.astype(BF16), l1_w_down.astype(BF16)
    g0_mix, g0_mlp = l0_norm_mix.reshape(1, d), l0_norm_mlp.reshape(1, d)
    g1_mix, g1_mlp = l1_norm_mix.reshape(1, d), l1_norm_mlp.reshape(1, d)
    gfin = final_norm.reshape(1, d)

    c_all = jnp.concatenate([c_prompt, c_sample], axis=0)
    mod0 = _ada(c_all, l0_w_ada, l0_b_ada)
    mod1 = _ada(c_all, l1_w_ada, l1_b_ada)
    mod0_p, mod0_s = mod0[:1], mod0[1:]
    mod1_p, mod1_s = mod1[:1], mod1[1:]

    xp = x_prompt.reshape(t, d)
    qa_p, ka_p, vt_p, k_p, v_p, lf_p, u_p = _proj0(xp, mod0_p, g0_mix, w0, True)
    yc_p = _conv0_prompt(u_p, l0_w_dw, l0_b_dw, l0_conv_ln_g, l0_conv_ln_b)
    at_p = _attn_prompt(qa_p, ka_p, vt_p)
    x1_p = _post(xp, [(at_p, w0_out_attn), (yc_p, w0_out_conv)], mod0_p, g0_mlp, w0_up, w0_down)
    by_p, tail_p = _mix1(x1_p, mod1_p, g1_mix, w1_in, l1_w_dw)
    y_p = _post(x1_p, [(by_p, w1_out)], mod1_p, g1_mlp, w1_up, w1_down, gfin)

    xs = x_sample.reshape(bs, d)
    qp_s, k_s, v_s, lf_s, u_s = _proj0(xs, mod0_s, g0_mix, w0, False)
    q_s = qp_s.reshape(bs, FOX_HEADS, AUG)[:, :, :HEAD_DIM].reshape(bs, FOX_WIDTH)
    yc_s = _conv0_sample(state_conformer_conv.transpose(1, 0, 2), u_s,
                         l0_w_dw, l0_b_dw, l0_conv_ln_g, l0_conv_ln_b)
    at_s = _attn_sample(q_s, k_s, v_s, lf_s, cache_k, cache_v, cache_logf, page_table)
    x1_s = _post(xs, [(at_s, w0_out_attn), (yc_s, w0_out_conv)], mod0_s, g0_mlp, w0_up, w0_down)
    by_s, cx_s = _mix1(x1_s, mod1_s, g1_mix, w1_in, l1_w_dw,
                       state=(state_short_conv[:, 0], state_short_conv[:, 1]))
    y_s = _post(x1_s, [(by_s, w1_out)], mod1_s, g1_mlp, w1_up, w1_down, gfin)

    hs = (FOX_HEADS, HEAD_DIM)
    return (y_p.reshape(1, t, d), y_s.reshape(bs, 1, d),
            k_p.reshape(1, t, *hs), v_p.reshape(1, t, *hs), lf_p.reshape(1, t, FOX_HEADS),
            u_p[t - (CONF_CONV_WIDTH - 1):].reshape(1, CONF_CONV_WIDTH - 1, CONF_CH),
            tail_p[-(SC_CONV_WIDTH - 1):].reshape(1, SC_CONV_WIDTH - 1, d),
            k_s.reshape(bs, 1, *hs), v_s.reshape(bs, 1, *hs), lf_s.reshape(bs, 1, FOX_HEADS),
            jnp.concatenate([state_conformer_conv[:, 1:], u_s[:, None, :]], axis=1),
            jnp.stack([state_short_conv[:, 1], cx_s], axis=1))
```

```python
import functools

import jax
import jax.numpy as jnp
from jax import lax
from jax.experimental import pallas as pl
from jax.experimental.pallas import tpu as pltpu

F32 = jnp.float32
BF16 = jnp.bfloat16

D_MODEL = 1024
FOX_HEADS = 8
HEAD_DIM = 64
FOX_WIDTH = FOX_HEADS * HEAD_DIM
CONF_CH = D_MODEL - FOX_WIDTH
CONF_CONV_WIDTH = 31
SC_CONV_WIDTH = 3
D_FF = 4 * D_MODEL
PAGE_SIZE = 128
EPS = 1e-6
ATTN_SCALE = HEAD_DIM ** -0.5
LOG2E = 1.4426950408889634

V7X_LANES = 128
V7X_SUBLANES = 8
V7X_VMEM_BYTES = 64 * 1024 * 1024
VMEM_LIMIT = 52 * 1024 * 1024

ROW_TILE = 512
CONV_TILE = 256
CONV_CHUNK = 64
CONV_HALO = 32
ATTN_TILE = 256
AUG = V7X_LANES
ATTN_HEADS = 2
PAGES_PER_STEP = 8


def _const_spec(shape):
    nd = len(shape)
    return pl.BlockSpec(shape, lambda *_: (0,) * nd, pipeline_mode=pl.Buffered(1))


def _params(*sem):
    return pltpu.CompilerParams(dimension_semantics=sem, vmem_limit_bytes=VMEM_LIMIT)


def _modnorm(x, g, shift, scale):
    ms = jnp.mean(x * x, axis=-1, keepdims=True)
    y = x * lax.rsqrt(ms + EPS)
    return (y * g) * (1.0 + scale) + shift


def _split3(x):
    hi = x.astype(BF16).astype(F32)
    r = x - hi
    mid = r.astype(BF16).astype(F32)
    lo = (r - mid).astype(BF16).astype(F32)
    return hi, mid, lo


def _ada_kernel(c_ref, w_ref, b_ref, o_ref):
    c = c_ref[...]
    s = (c * jax.nn.sigmoid(c)).astype(BF16)
    o_ref[...] = jnp.dot(s, w_ref[...].astype(BF16), preferred_element_type=F32) + b_ref[...]


def _ada(c_all, w_ada, b_ada):
    m = c_all.shape[0]
    n = w_ada.shape[1]
    bn = 1536
    return pl.pallas_call(
        _ada_kernel,
        out_shape=jax.ShapeDtypeStruct((m, n), F32),
        grid=(n // bn,),
        in_specs=[pl.BlockSpec((m, D_MODEL), lambda j: (0, 0)),
                  pl.BlockSpec((D_MODEL, bn), lambda j: (0, j)),
                  pl.BlockSpec((1, bn), lambda j: (0, j))],
        out_specs=pl.BlockSpec((m, bn), lambda j: (0, j)),
        compiler_params=_params("arbitrary"),
        name="ada",
    )(c_all, w_ada, b_ada.reshape(1, n))


def _bias_placement():
    import numpy as np
    pq = np.zeros((V7X_LANES, FOX_HEADS * AUG), np.float32)
    pk = np.zeros((V7X_LANES, FOX_HEADS * AUG), np.float32)
    for h in range(FOX_HEADS):
        for p in range(3):
            pq[p * FOX_HEADS + h, h * AUG + HEAD_DIM + p] = 1.0
            pk[p * FOX_HEADS + h, h * AUG + HEAD_DIM + 3 + p] = -1.0
            pq[3 * FOX_HEADS, h * AUG + HEAD_DIM + 3 + p] = 1.0
            pk[3 * FOX_HEADS, h * AUG + HEAD_DIM + p] = 1.0
    return jnp.asarray(pq, BF16), jnp.asarray(pk, BF16)


def _proj0_kernel(*refs, prompt):
    x_ref, sh_ref, sc_ref, g_ref, wq_ref, wkv_ref, wf_ref, bf_ref, wglu_ref = refs[:9]
    x = x_ref[...]
    h = _modnorm(x, g_ref[...], sh_ref[...], sc_ref[...]).astype(BF16)
    kv = jnp.dot(h, wkv_ref[...], preferred_element_type=F32)
    k = kv[:, :FOX_WIDTH]
    v = kv[:, FOX_WIDTH:]
    fg = jnp.dot(h, wf_ref[...], preferred_element_type=F32) + bf_ref[...]
    lf = jnp.minimum(fg, 0.0) - jnp.log1p(jnp.exp(-jnp.abs(fg)))
    glu = jnp.dot(h, wglu_ref[...], preferred_element_type=F32)
    u = glu[:, :CONF_CH] * jax.nn.sigmoid(glu[:, CONF_CH:])
    qp = jnp.dot(h, wq_ref[...], preferred_element_type=F32)

    if not prompt:
        q_ref, k_ref, v_ref, lf_ref, u_ref = refs[9:]
        q_ref[...] = qp
    else:
        wk_ref, pq_ref, pk_ref, qa_ref, ka_ref, vt_ref, k_ref, v_ref, lf_ref, u_ref, tri_ref, carry_ref = refs[9:]
        tm = x.shape[0]
        tk = vt_ref.shape[2]
        i = pl.program_id(0)

        @pl.when(i == 0)
        def _():
            carry_ref[...] = jnp.zeros_like(carry_ref)
            r = lax.broadcasted_iota(jnp.int32, (tm, tm), 0)
            c = lax.broadcasted_iota(jnp.int32, (tm, tm), 1)
            tri_ref[...] = (r <= c).astype(BF16)

        lft = lf.T[:FOX_HEADS]
        parts = jnp.concatenate(_split3(lft), axis=0).astype(BF16)
        cs = jnp.dot(parts, tri_ref[...], preferred_element_type=F32)
        ft = cs[0:8] + cs[8:16] + cs[16:24] + carry_ref[...]
        carry_ref[...] = ft[:, tm - 1:tm]

        pieces_t = jnp.concatenate(
            list(_split3(ft * LOG2E)) + [jnp.ones((FOX_HEADS, tm), F32),
                                 jnp.zeros((V7X_LANES - 4 * FOX_HEADS, tm), F32)], axis=0)
        pieces = pieces_t.T.astype(BF16)
        qa_ref[...] = (qp + jnp.dot(pieces, pq_ref[...], preferred_element_type=F32)).astype(BF16)
        kp = jnp.dot(h, wk_ref[...], preferred_element_type=F32)
        ka_ref[...] = (kp + jnp.dot(pieces, pk_ref[...], preferred_element_type=F32)).astype(BF16)
        vtt = v.T.astype(BF16)
        for c0 in range(tm // tk):
            vt_ref[c0] = vtt[:, c0 * tk:(c0 + 1) * tk]

    k_ref[...] = k
    v_ref[...] = v
    lf_ref[...] = lf[:, :FOX_HEADS]
    u_ref[...] = u


def _proj0(x, mod, g, w, prompt):
    t = x.shape[0]
    tm = min(ROW_TILE, t)
    mm = mod.shape[0]
    mrow = (lambda i: (0, 0)) if mm == 1 else (lambda i: (i, 0))
    mrow1 = (lambda i: (0, 1)) if mm == 1 else (lambda i: (i, 1))
    bm = 1 if mm == 1 else tm
    row = lambda i: (i, 0)
    wide = FOX_HEADS * AUG
    in_specs = [pl.BlockSpec((tm, D_MODEL), row),
                pl.BlockSpec((bm, D_MODEL), mrow),
                pl.BlockSpec((bm, D_MODEL), mrow1),
                _const_spec((1, D_MODEL)),
                _const_spec(w["q"].shape),
                _const_spec(w["kv"].shape),
                _const_spec(w["f"].shape),
                _const_spec((1, V7X_LANES)),
                _const_spec(w["glu"].shape)]
    args = [x, mod, mod, g, w["q"], w["kv"], w["f"], w["bf"], w["glu"]]
    tail_shape = [jax.ShapeDtypeStruct((t, FOX_WIDTH), F32), jax.ShapeDtypeStruct((t, FOX_WIDTH), F32),
                  jax.ShapeDtypeStruct((t, FOX_HEADS), F32), jax.ShapeDtypeStruct((t, CONF_CH), F32)]
    tail_specs = [pl.BlockSpec((tm, FOX_WIDTH), row), pl.BlockSpec((tm, FOX_WIDTH), row),
                  pl.BlockSpec((tm, FOX_HEADS), row), pl.BlockSpec((tm, CONF_CH), row)]
    if prompt:
        tk = ATTN_TILE
        pq, pk = _bias_placement()
        in_specs += [_const_spec(w["k"].shape), _const_spec(pq.shape), _const_spec(pk.shape)]
        args += [w["k"], pq, pk]
        out_shape = [jax.ShapeDtypeStruct((t, wide), BF16), jax.ShapeDtypeStruct((t, wide), BF16),
                     jax.ShapeDtypeStruct((t // tk, FOX_WIDTH, tk), BF16)] + tail_shape
        out_specs = [pl.BlockSpec((tm, wide), row), pl.BlockSpec((tm, wide), row),
                     pl.BlockSpec((tm // tk, FOX_WIDTH, tk), lambda i: (i, 0, 0))] + tail_specs
        scratch = [pltpu.VMEM((tm, tm), BF16), pltpu.VMEM((FOX_HEADS, 1), F32)]
    else:
        out_shape = [jax.ShapeDtypeStruct((t, wide), F32)] + tail_shape
        out_specs = [pl.BlockSpec((tm, wide), row)] + tail_specs
        scratch = []
    return pl.pallas_call(
        functools.partial(_proj0_kernel, prompt=prompt),
        out_shape=out_shape,
        grid=(t // tm,),
        in_specs=in_specs,
        out_specs=out_specs,
        scratch_shapes=scratch,
        compiler_params=_params("arbitrary"),
        name="proj0",
    )(*args)


def _ln_silu(y, ln_g, ln_b):
    mu = jnp.mean(y, axis=-1, keepdims=True)
    d = y - mu
    var = jnp.mean(d * d, axis=-1, keepdims=True)
    z = d * lax.rsqrt(var + EPS) * ln_g + ln_b
    return z * jax.nn.sigmoid(z)


def _conv0_kernel(u_ref, up_ref, wdw_ref, bdw_ref, lng_ref, lnb_ref, y_ref, buf_ref):
    i = pl.program_id(0)
    tm = u_ref.shape[0]
    buf_ref[0:CONV_HALO, :] = jnp.where(i == 0, 0.0, up_ref[...])
    buf_ref[CONV_HALO:, :] = u_ref[...]
    first = CONV_HALO - (CONF_CONV_WIDTH - 1)
    for c0 in range(0, tm, CONV_CHUNK):
        acc = jnp.zeros((CONV_CHUNK, CONF_CH), F32)
        for kk in range(CONF_CONV_WIDTH):
            acc = acc + wdw_ref[kk:kk + 1, :] * buf_ref[pl.ds(first + c0 + kk, CONV_CHUNK), :]
        z = _ln_silu(acc + bdw_ref[...], lng_ref[...], lnb_ref[...])
        y_ref[c0:c0 + CONV_CHUNK, :] = z.astype(BF16)


def _conv0_prompt(u, w_dw, b_dw, ln_g, ln_b):
    t = u.shape[0]
    tm = CONV_TILE
    per = tm // CONV_HALO
    return pl.pallas_call(
        _conv0_kernel,
        out_shape=jax.ShapeDtypeStruct((t, CONF_CH), BF16),
        grid=(t // tm,),
        in_specs=[pl.BlockSpec((tm, CONF_CH), lambda i: (i, 0)),
                  pl.BlockSpec((CONV_HALO, CONF_CH), lambda i: (jnp.maximum(i * per - 1, 0), 0)),
                  _const_spec((CONF_CONV_WIDTH, CONF_CH)),
                  _const_spec((1, CONF_CH)), _const_spec((1, CONF_CH)), _const_spec((1, CONF_CH))],
        out_specs=pl.BlockSpec((tm, CONF_CH), lambda i: (i, 0)),
        scratch_shapes=[pltpu.VMEM((tm + CONV_HALO, CONF_CH), F32)],
        compiler_params=_params("arbitrary"),
        name="conv0",
    )(u, u, w_dw, b_dw.reshape(1, -1), ln_g.reshape(1, -1), ln_b.reshape(1, -1))


def _conv0s_kernel(st_ref, u_ref, wdw_ref, bdw_ref, lng_ref, lnb_ref, y_ref):
    acc = wdw_ref[CONF_CONV_WIDTH - 1:CONF_CONV_WIDTH, :] * u_ref[...]
    for kk in range(CONF_CONV_WIDTH - 1):
        acc = acc + wdw_ref[kk:kk + 1, :] * st_ref[kk]
    z = _ln_silu(acc + bdw_ref[...], lng_ref[...], lnb_ref[...])
    y_ref[...] = z.astype(BF16)


def _conv0_sample(state_t, u, w_dw, b_dw, ln_g, ln_b):
    b = u.shape[0]
    return pl.pallas_call(
        _conv0s_kernel,
        out_shape=jax.ShapeDtypeStruct((b, CONF_CH), BF16),
        name="conv0s",
    )(state_t, u, w_dw, b_dw.reshape(1, -1), ln_g.reshape(1, -1), ln_b.reshape(1, -1))


def _attn_kernel(qa_ref, ka_ref, vt_ref, o_ref, s_ref):
    i = pl.program_id(1)
    tq = qa_ref.shape[0]
    heads = range(ATTN_HEADS)
    qa = [qa_ref[:, hh * AUG:(hh + 1) * AUG] for hh in heads]

    def scores(j):
        start = pl.multiple_of(j * tq, tq)
        return [lax.dot_general(ka_ref[pl.ds(start, tq), hh * AUG:(hh + 1) * AUG], qa[hh],
                                (((1,), (1,)), ((), ())), preferred_element_type=F32)
                for hh in heads]

    def fold(j, carry, masked):
        out = []
        for hh in heads:
            m_prev, l_prev, acc = carry[hh]
            st = s_ref[j % 2, hh]
            if masked:
                r = lax.broadcasted_iota(jnp.int32, st.shape, 0)
                c = lax.broadcasted_iota(jnp.int32, st.shape, 1)
                st = jnp.where(r <= c, st, -jnp.inf)
            m_new = jnp.maximum(m_prev, jnp.max(st, axis=0, keepdims=True))
            alpha = jnp.exp2(m_prev - m_new)
            pt = jnp.exp2(st - m_new)
            l_new = alpha * l_prev + jnp.sum(pt, axis=0, keepdims=True)
            vt = vt_ref[j, hh * HEAD_DIM:(hh + 1) * HEAD_DIM, :]
            acc = acc * alpha + jnp.dot(vt, pt.astype(BF16), preferred_element_type=F32)
            out.append((m_new, l_new, acc))
        return tuple(out)

    def body(j, carry):
        nxt = scores(j + 1)
        carry = fold(j, carry, False)
        for hh in heads:
            s_ref[(j + 1) % 2, hh] = nxt[hh]
        return carry

    first = scores(0)
    for hh in heads:
        s_ref[0, hh] = first[hh]
    init = tuple((jnp.full((1, tq), -jnp.inf, F32), jnp.zeros((1, tq), F32),
                  jnp.zeros((HEAD_DIM, tq), F32)) for _ in heads)
    carry = lax.fori_loop(0, i, body, init)
    carry = fold(i, carry, True)
    ot = jnp.concatenate([acc / l for (_, l, acc) in carry], axis=0)
    o_ref[...] = ot.T.astype(BF16)


def _attn_prompt(qa, ka, vt):
    t = qa.shape[0]
    tq = ATTN_TILE
    nh = ATTN_HEADS
    nblk = t // tq
    once = pl.Buffered(1)
    return pl.pallas_call(
        _attn_kernel,
        out_shape=jax.ShapeDtypeStruct((t, FOX_WIDTH), BF16),
        grid=(FOX_HEADS // nh, nblk),
        in_specs=[pl.BlockSpec((tq, nh * AUG), lambda p, i: (i, p)),
                  pl.BlockSpec((t, nh * AUG), lambda p, i: (0, p), pipeline_mode=once),
                  pl.BlockSpec((nblk, nh * HEAD_DIM, tq), lambda p, i: (0, p, 0), pipeline_mode=once)],
        out_specs=pl.BlockSpec((tq, nh * HEAD_DIM), lambda p, i: (i, p)),
        scratch_shapes=[pltpu.VMEM((2, nh, tq, tq), F32)],
        compiler_params=_params("arbitrary", "arbitrary"),
        name="attn_prompt",
    )(qa, ka, vt)


def _attn_s_kernel(pt_ref, q_ref, kn_ref, vn_ref, lfn_ref, *rest):
    del pt_ref
    npg = PAGES_PER_STEP
    k_refs = rest[:npg]
    v_refs = rest[npg:2 * npg]
    lf_refs = rest[2 * npg:3 * npg]
    (o_ref, w_ref, ones_ref, diag_ref, lfp_ref, pcs_ref, xb_ref, s_ref,
     m_ref, l_ref, run_ref, acc_ref) = rest[3 * npg:]
    g = pl.program_id(1)
    lanes = V7X_LANES
    rows = PAGE_SIZE * FOX_HEADS
    q3 = q_ref[0]
    zpad = jnp.zeros((FOX_HEADS, lanes - HEAD_DIM), F32)

    def lane_sum(x):
        return jnp.dot(x.astype(BF16), ones_ref[...], preferred_element_type=F32)

    @pl.when(g == 0)
    def _():
        tr = lax.broadcasted_iota(jnp.int32, (PAGE_SIZE, PAGE_SIZE), 0)
        tc = lax.broadcasted_iota(jnp.int32, (PAGE_SIZE, PAGE_SIZE), 1)
        w_ref[...] = (tc > tr).astype(BF16)
        br = lax.broadcasted_iota(jnp.int32, (lanes, lanes), 0)
        ones_ref[...] = (br < HEAD_DIM + 3 * FOX_HEADS).astype(BF16)
        dh = lax.broadcasted_iota(jnp.int32, (FOX_HEADS, lanes), 0)
        dl = lax.broadcasted_iota(jnp.int32, (FOX_HEADS, lanes), 1) - HEAD_DIM
        diag_ref[...] = ((dl >= 0) & (dl < 3 * FOX_HEADS) & (dl % FOX_HEADS == dh)).astype(F32)
        lfp_ref[...] = jnp.zeros_like(lfp_ref)
        run_ref[...] = lfn_ref[0]
        kq = jnp.concatenate([kn_ref[0] * q3, zpad], axis=1)
        m_ref[...] = lane_sum(jnp.concatenate([kq, kq], axis=0))[:FOX_HEADS]
        l_ref[...] = jnp.ones_like(l_ref)
        acc_ref[...] = vn_ref[0]

    run = run_ref[...]
    for i in range(npg):
        lfp_ref[i, :, 0:FOX_HEADS] = lf_refs[i][0]
        lf = lfp_ref[i]
        suf = sum(jnp.dot(w_ref[...], p.astype(BF16), preferred_element_type=F32) for p in _split3(lf))
        hi, mid, lo = _split3((suf + run) * LOG2E)
        pcs_ref[i] = (pltpu.roll(hi, HEAD_DIM, axis=1) + pltpu.roll(mid, HEAD_DIM + FOX_HEADS, axis=1)
                      + pltpu.roll(lo, HEAD_DIM + 2 * FOX_HEADS, axis=1))
        run = run + suf[0:1, :] + lf[0:1, :]
    run_ref[...] = run

    diag = diag_ref[...]
    for i in range(npg):
        for t in range(PAGE_SIZE):
            xb_ref[i, t] = jnp.broadcast_to(pcs_ref[i, t:t + 1, :], (FOX_HEADS, lanes)) * diag

    def fold_tokens(x):
        while x.shape[0] > 1:
            half = x.shape[0] // 2
            x = x[:half] + x[half:]
        return x[0]

    m_new = m_prev = m_ref[...]
    for i in range(npg):
        xb_ref[i, :, :, 0:HEAD_DIM] = k_refs[i][0] * q3[None]
        s = lane_sum(xb_ref[i].reshape(rows, lanes)).reshape(PAGE_SIZE, FOX_HEADS, lanes)
        s_ref[i] = s
        m_new = jnp.maximum(m_new, jnp.max(s, axis=0))
    alpha = jnp.exp2(m_prev - m_new)
    lsum = alpha * l_ref[...]
    acc = acc_ref[...] * alpha[:, :HEAD_DIM]
    for i in range(npg):
        p = jnp.exp2(s_ref[i] - m_new[None])
        lsum = lsum + fold_tokens(p)
        acc = acc + fold_tokens(p[:, :, :HEAD_DIM] * v_refs[i][0])
    l_ref[...] = lsum
    acc_ref[...] = acc
    m_ref[...] = m_new

    @pl.when(g == pl.num_programs(1) - 1)
    def _():
        o_ref[0] = acc_ref[...] / l_ref[:, :HEAD_DIM]


def _attn_sample(q, k_new, v_new, lf_new, cache_k, cache_v, cache_logf, page_table):
    b = q.shape[0]
    n_pages = page_table.shape[1]
    npg = PAGES_PER_STEP
    nsteps = n_pages // npg
    hd = (FOX_HEADS, HEAD_DIM)

    def page_map(nd):
        def for_slot(slot):
            return lambda bi, g, pt: (pt[bi, n_pages - 1 - (g * npg + slot)],) + (0,) * (nd - 1)
        return for_slot

    seq = lambda bi, g, pt: (bi, 0, 0)
    in_specs = [pl.BlockSpec((1,) + hd, seq)] * 3 + [pl.BlockSpec((1, 1, V7X_LANES), seq)]
    in_specs += [pl.BlockSpec((1, PAGE_SIZE) + hd, page_map(4)(s)) for s in range(npg)]
    in_specs += [pl.BlockSpec((1, PAGE_SIZE) + hd, page_map(4)(s)) for s in range(npg)]
    in_specs += [pl.BlockSpec((1, PAGE_SIZE, FOX_HEADS), page_map(3)(s)) for s in range(npg)]
    grid_spec = pltpu.PrefetchScalarGridSpec(
        num_scalar_prefetch=1,
        grid=(b, nsteps),
        in_specs=in_specs,
        out_specs=pl.BlockSpec((1,) + hd, seq),
        scratch_shapes=[pltpu.VMEM((PAGE_SIZE, PAGE_SIZE), BF16),
                        pltpu.VMEM((V7X_LANES, V7X_LANES), BF16),
                        pltpu.VMEM((FOX_HEADS, V7X_LANES), F32),
                        pltpu.VMEM((npg, PAGE_SIZE, V7X_LANES), F32),
                        pltpu.VMEM((npg, PAGE_SIZE, V7X_LANES), F32),
                        pltpu.VMEM((npg, PAGE_SIZE, FOX_HEADS, V7X_LANES), F32),
                        pltpu.VMEM((npg, PAGE_SIZE, FOX_HEADS, V7X_LANES), F32),
                        pltpu.VMEM((FOX_HEADS, V7X_LANES), F32),
                        pltpu.VMEM((FOX_HEADS, V7X_LANES), F32),
                        pltpu.VMEM((1, V7X_LANES), F32),
                        pltpu.VMEM(hd, F32)])
    lfn = jnp.pad(lf_new, ((0, 0), (0, V7X_LANES - FOX_HEADS))).reshape(b, 1, V7X_LANES)
    out = pl.pallas_call(
        _attn_s_kernel,
        out_shape=jax.ShapeDtypeStruct((b,) + hd, F32),
        grid_spec=grid_spec,
        compiler_params=_params("arbitrary", "arbitrary"),
        name="attn_sample",
    )(page_table, q.reshape((b,) + hd), k_new.reshape((b,) + hd), v_new.reshape((b,) + hd), lfn,
      *([cache_k] * npg), *([cache_v] * npg), *([cache_logf] * npg))
    return out.reshape(b, FOX_WIDTH).astype(BF16)


def _post_kernel(*refs, n_mix, final):
    x_ref = refs[0]
    mix_refs = refs[1:1 + 2 * n_mix]
    gm_ref, shf_ref, scf_ref, gf_ref, g_ref, wup_ref, wdn_ref = refs[1 + 2 * n_mix:8 + 2 * n_mix]
    rest = refs[8 + 2 * n_mix:]
    if final:
        gfin_ref, o_ref = rest
    else:
        (o_ref,) = rest
    mix = None
    for a in range(n_mix):
        t = jnp.dot(mix_refs[2 * a][...], mix_refs[2 * a + 1][...], preferred_element_type=F32)
        mix = t if mix is None else mix + t
    x1 = x_ref[...] + gm_ref[...] * mix
    h = _modnorm(x1, g_ref[...], shf_ref[...], scf_ref[...]).astype(BF16)
    fchunk = 1024
    acc = None
    for f0 in range(0, D_FF, fchunk):
        a = jnp.maximum(jnp.dot(h, wup_ref[:, f0:f0 + fchunk], preferred_element_type=F32), 0.0)
        t = jnp.dot((a * a).astype(BF16), wdn_ref[f0:f0 + fchunk, :], preferred_element_type=F32)
        acc = t if acc is None else acc + t
    x2 = x1 + gf_ref[...] * acc
    if final:
        ms = jnp.mean(x2 * x2, axis=-1, keepdims=True)
        x2 = x2 * lax.rsqrt(ms + EPS) * gfin_ref[...]
    o_ref[...] = x2


def _post(x, mixes, mod, g_mlp, w_up, w_down, final_g=None):
    t = x.shape[0]
    tm = min(ROW_TILE, t)
    mm = mod.shape[0]
    bm = 1 if mm == 1 else tm
    row = lambda i: (i, 0)

    def mcol(c):
        return (lambda i: (0, c)) if mm == 1 else (lambda i: (i, c))

    in_specs = [pl.BlockSpec((tm, D_MODEL), row)]
    args = [x]
    for a, w in mixes:
        in_specs += [pl.BlockSpec((tm, a.shape[1]), row), _const_spec(w.shape)]
        args += [a, w]
    in_specs += [pl.BlockSpec((bm, D_MODEL), mcol(2)), pl.BlockSpec((bm, D_MODEL), mcol(3)),
                 pl.BlockSpec((bm, D_MODEL), mcol(4)), pl.BlockSpec((bm, D_MODEL), mcol(5)),
                 _const_spec((1, D_MODEL)), _const_spec(w_up.shape), _const_spec(w_down.shape)]
    args += [mod, mod, mod, mod, g_mlp, w_up, w_down]
    if final_g is not None:
        in_specs.append(_const_spec((1, D_MODEL)))
        args.append(final_g)
    return pl.pallas_call(
        functools.partial(_post_kernel, n_mix=len(mixes), final=final_g is not None),
        out_shape=jax.ShapeDtypeStruct((t, D_MODEL), F32),
        grid=(t // tm,),
        in_specs=in_specs,
        out_specs=pl.BlockSpec((tm, D_MODEL), row),
        compiler_params=_params("arbitrary"),
        name="post",
    )(*args)


def _mix1_kernel(x_ref, sh_ref, sc_ref, g_ref, win_ref, wdw_ref, *rest, per_row_state):
    x = x_ref[...]
    tm = x.shape[0]
    h = _modnorm(x, g_ref[...], sh_ref[...], sc_ref[...]).astype(BF16)
    proj = jnp.dot(h, win_ref[...], preferred_element_type=F32)
    b_gate = proj[:, :D_MODEL]
    cx = proj[:, D_MODEL:2 * D_MODEL] * proj[:, 2 * D_MODEL:]
    w0, w1, w2 = wdw_ref[0:1, :], wdw_ref[1:2, :], wdw_ref[2:3, :]
    if per_row_state:
        s0_ref, s1_ref, by_ref, cx_ref = rest
        y = w0 * s0_ref[...] + w1 * s1_ref[...] + w2 * cx
        cx_ref[...] = cx
    else:
        by_ref, tail_ref, buf_ref = rest
        i = pl.program_id(0)
        pad = V7X_SUBLANES

        @pl.when(i == 0)
        def _():
            buf_ref[0:pad, :] = jnp.zeros((pad, D_MODEL), F32)

        buf_ref[pad:, :] = cx
        y = (w0 * buf_ref[pl.ds(pad - 2, tm), :] + w1 * buf_ref[pl.ds(pad - 1, tm), :] + w2 * cx)
        tail = cx[tm - pad:, :]
        buf_ref[0:pad, :] = tail
        tail_ref[...] = tail
    by_ref[...] = (b_gate * y).astype(BF16)


def _mix1(x, mod, g, w_in, w_dw, state=None):
    t = x.shape[0]
    tm = min(ROW_TILE, t)
    mm = mod.shape[0]
    bm = 1 if mm == 1 else tm
    row = lambda i: (i, 0)
    mcol = lambda c: (lambda i: (0, c)) if mm == 1 else (lambda i: (i, c))
    in_specs = [pl.BlockSpec((tm, D_MODEL), row),
                pl.BlockSpec((bm, D_MODEL), mcol(0)), pl.BlockSpec((bm, D_MODEL), mcol(1)),
                _const_spec((1, D_MODEL)), _const_spec(w_in.shape), _const_spec(w_dw.shape)]
    args = [x, mod, mod, g, w_in, w_dw]
    if state is not None:
        in_specs += [pl.BlockSpec((tm, D_MODEL), row)] * 2
        args += [state[0], state[1]]
        out_shape = [jax.ShapeDtypeStruct((t, D_MODEL), BF16), jax.ShapeDtypeStruct((t, D_MODEL), F32)]
        out_specs = [pl.BlockSpec((tm, D_MODEL), row)] * 2
        scratch = []
    else:
        nt = t // tm
        out_shape = [jax.ShapeDtypeStruct((t, D_MODEL), BF16),
                     jax.ShapeDtypeStruct((nt * V7X_SUBLANES, D_MODEL), F32)]
        out_specs = [pl.BlockSpec((tm, D_MODEL), row), pl.BlockSpec((V7X_SUBLANES, D_MODEL), row)]
        scratch = [pltpu.VMEM((tm + V7X_SUBLANES, D_MODEL), F32)]
    return pl.pallas_call(
        functools.partial(_mix1_kernel, per_row_state=state is not None),
        out_shape=out_shape,
        grid=(t // tm,),
        in_specs=in_specs,
        out_specs=out_specs,
        scratch_shapes=scratch,
        compiler_params=_params("arbitrary"),
        name="mix1",
    )(*args)


def kernel(x_prompt, x_sample, cache_k, cache_v, cache_logf, state_conformer_conv, state_short_conv, page_table, c_prompt, c_sample, l0_w_ada, l0_b_ada, l0_norm_mix, l0_norm_mlp, l0_w_in, l0_b_forget, l0_w_dw, l0_b_dw, l0_conv_ln_g, l0_conv_ln_b, l0_w_out, l0_w_up, l0_w_down, l1_w_ada, l1_b_ada, l1_norm_mix, l1_norm_mlp, l1_w_in, l1_w_dw, l1_w_out, l1_w_up, l1_w_down, final_norm):
    bp, t, d = x_prompt.shape
    bs = x_sample.shape[0]
    assert bp == 1 and d == D_MODEL and x_sample.shape[1] == 1

    def per_head_padded(wcols, scale):
        w3 = (wcols * scale).reshape(d, FOX_HEADS, HEAD_DIM)
        return jnp.pad(w3, ((0, 0), (0, 0), (0, AUG - HEAD_DIM))).reshape(d, FOX_HEADS * AUG).astype(BF16)

    w0 = {
        "q": per_head_padded(l0_w_in[:, :FOX_WIDTH], ATTN_SCALE * LOG2E),
        "k": per_head_padded(l0_w_in[:, FOX_WIDTH:2 * FOX_WIDTH], 1.0),
        "kv": l0_w_in[:, FOX_WIDTH:3 * FOX_WIDTH].astype(BF16),
        "f":jnp.pad(l0_w_in[:, 3 * FOX_WIDTH:3 * FOX_WIDTH + FOX_HEADS],
                     ((0, 0), (0, V7X_LANES - FOX_HEADS))).astype(BF16),
        "bf": jnp.pad(l0_b_forget, (0, V7X_LANES - FOX_HEADS)).reshape(1, V7X_LANES),
        "glu": l0_w_in[:, 3 * FOX_WIDTH + FOX_HEADS:].astype(BF16),
    }
    w0_out_attn = l0_w_out[:FOX_WIDTH].astype(BF16)
    w0_out_conv = l0_w_out[FOX_WIDTH:].astype(BF16)
    w0_up, w0_down = l0_w_up.astype(BF16), l0_w_down.astype(BF16)
    w1_in, w1_out = l1_w_in.astype(BF16), l1_w_out.astype(BF16)
    w1_up, w1_down = l1_w_up.astype(BF16), l1_w_down.astype(BF16)
    g0_mix, g0_mlp = l0_norm_mix.reshape(1, d), l0_norm_mlp.reshape(1, d)
    g1_mix, g1_mlp = l1_norm_mix.reshape(1, d), l1_norm_mlp.reshape(1, d)
    gfin = final_norm.reshape(1, d)

    c_all = jnp.concatenate([c_prompt, c_sample], axis=0)
    mod0 = _ada(c_all, l0_w_ada, l0_b_ada)
    mod1 = _ada(c_all, l1_w_ada, l1_b_ada)
    mod0_p, mod0_s = mod0[:1], mod0[1:]
    mod1_p, mod1_s = mod1[:1], mod1[1:]

    xp = x_prompt.reshape(t, d)
    qa_p, ka_p, vt_p, k_p, v_p, lf_p, u_p = _proj0(xp, mod0_p, g0_mix, w0, True)
    yc_p = _conv0_prompt(u_p, l0_w_dw, l0_b_dw, l0_conv_ln_g, l0_conv_ln_b)
    at_p = _attn_prompt(qa_p, ka_p, vt_p)
    x1_p = _post(xp, [(at_p, w0_out_attn), (yc_p, w0_out_conv)], mod0_p, g0_mlp, w0_up, w0_down)
    by_p, tail_p = _mix1(x1_p, mod1_p, g1_mix, w1_in, l1_w_dw)
    y_p = _post(x1_p, [(by_p, w1_out)], mod1_p, g1_mlp, w1_up, w1_down, gfin)

    xs = x_sample.reshape(bs, d)
    qp_s, k_s, v_s, lf_s, u_s = _proj0(xs, mod0_s, g0_mix, w0, False)
    q_s = qp_s.reshape(bs, FOX_HEADS, AUG)[:, :, :HEAD_DIM].reshape(bs, FOX_WIDTH)
    yc_s = _conv0_sample(state_conformer_conv.transpose(1, 0, 2), u_s,
                         l0_w_dw, l0_b_dw, l0_conv_ln_g, l0_conv_ln_b)
    at_s = _attn_sample(q_s, k_s, v_s, lf_s, cache_k, cache_v, cache_logf, page_table)
    x1_s = _post(xs, [(at_s, w0_out_attn), (yc_s, w0_out_conv)], mod0_s, g0_mlp, w0_up, w0_down)
    by_s, cx_s = _mix1(x1_s, mod1_s, g1_mix, w1_in, l1_w_dw,
                       state=(state_short_conv[:, 0], state_short_conv[:, 1]))
    y_s = _post(x1_s, [(by_s, w1_out)], mod1_s, g1_mlp, w1_up, w1_down, gfin)

    hs = (FOX_HEADS, HEAD_DIM)
    return (y_p.reshape(1, t, d), y_s.reshape(bs, 1, d),
            k_p.reshape(1, t, *hs), v_p.reshape(1, t, *hs), lf_p.reshape(1, t, FOX_HEADS),
            u_p[t - (CONF_CONV_WIDTH - 1):].reshape(1, CONF_CONV_WIDTH - 1, CONF_CH),
            tail_p[-(SC_CONV_WIDTH - 1):].reshape(1, SC_CONV_WIDTH - 1, d),
            k_s.reshape(bs, 1, *hs), v_s.reshape(bs, 1, *hs), lf_s.reshape(bs, 1, FOX_HEADS),
            jnp.concatenate([state_conformer_conv[:, 1:], u_s[:, None, :]], axis=1),
            jnp.stack([state_short_conv[:, 1], cx_s], axis=1))
```

```python
import functools

import jax
import jax.numpy as jnp
from jax import lax
from jax.experimental import pallas as pl
from jax.experimental.pallas import tpu as pltpu

F32 = jnp.float32
BF16 = jnp.bfloat16

D_MODEL = 1024
FOX_HEADS = 8
HEAD_DIM = 64
FOX_WIDTH = FOX_HEADS * HEAD_DIM
CONF_CH = D_MODEL - FOX_WIDTH
CONF_CONV_WIDTH = 31
SC_CONV_WIDTH = 3
D_FF = 4 * D_MODEL
PAGE_SIZE = 128
EPS = 1e-6
ATTN_SCALE = HEAD_DIM ** -0.5
LOG2E = 1.4426950408889634

V7X_LANES = 128
V7X_SUBLANES = 8
V7X_VMEM_BYTES = 64 * 1024 * 1024
VMEM_LIMIT = 52 * 1024 * 1024

ROW_TILE = 512
CONV_TILE = 256
CONV_CHUNK = 64
CONV_HALO = 32
ATTN_TILE = 256
AUG = V7X_LANES
ATTN_HEADS = 2
PAGES_PER_STEP = 8


def _const_spec(shape):
    nd = len(shape)
    return pl.BlockSpec(shape, lambda *_: (0,) * nd, pipeline_mode=pl.Buffered(1))


def _params(*sem):
    return pltpu.CompilerParams(dimension_semantics=sem, vmem_limit_bytes=VMEM_LIMIT)


def _modnorm(x, g, shift, scale):
    ms = jnp.mean(x * x, axis=-1, keepdims=True)
    y = x * lax.rsqrt(ms + EPS)
    return (y * g) * (1.0 + scale) + shift


def _split3(x):
    hi = x.astype(BF16).astype(F32)
    r = x - hi
    mid = r.astype(BF16).astype(F32)
    lo = (r - mid).astype(BF16).astype(F32)
    return hi, mid, lo


def _ada_kernel(c_ref, w_ref, b_ref, o_ref):
    c = c_ref[...]
    s = (c * jax.nn.sigmoid(c)).astype(BF16)
    o_ref[...] = jnp.dot(s, w_ref[...].astype(BF16), preferred_element_type=F32) + b_ref[...]


def _ada(c_all, w_ada, b_ada):
    m = c_all.shape[0]
    n = w_ada.shape[1]
    bn = 1536
    return pl.pallas_call(
        _ada_kernel,
        out_shape=jax.ShapeDtypeStruct((m, n), F32),
        grid=(n // bn,),
        in_specs=[pl.BlockSpec((m, D_MODEL), lambda j: (0, 0)),
                  pl.BlockSpec((D_MODEL, bn), lambda j: (0, j)),
                  pl.BlockSpec((1, bn), lambda j: (0, j))],
        out_specs=pl.BlockSpec((m, bn), lambda j: (0, j)),
        compiler_params=_params("arbitrary"),
        name="ada",
    )(c_all, w_ada, b_ada.reshape(1, n))


def _bias_placement():
    import numpy as np
    pq = np.zeros((V7X_LANES, FOX_HEADS * AUG), np.float32)
    pk = np.zeros((V7X_LANES, FOX_HEADS * AUG), np.float32)
    for h in range(FOX_HEADS):
        for p in range(3):
            pq[p * FOX_HEADS + h, h * AUG + HEAD_DIM + p] = 1.0
            pk[p * FOX_HEADS + h, h * AUG + HEAD_DIM + 3 + p] = -1.0
            pq[3 * FOX_HEADS, h * AUG + HEAD_DIM + 3 + p] = 1.0
            pk[3 * FOX_HEADS, h * AUG + HEAD_DIM + p] = 1.0
    return jnp.asarray(pq, BF16), jnp.asarray(pk, BF16)


def _proj0_kernel(*refs, prompt):
    x_ref, sh_ref, sc_ref, g_ref, wq_ref, wkv_ref, wf_ref, bf_ref, wglu_ref = refs[:9]
    x = x_ref[...]
    h = _modnorm(x, g_ref[...], sh_ref[...], sc_ref[...]).astype(BF16)
    kv = jnp.dot(h, wkv_ref[...], preferred_element_type=F32)
    k = kv[:, :FOX_WIDTH]
    v = kv[:, FOX_WIDTH:]
    fg = jnp.dot(h, wf_ref[...], preferred_element_type=F32) + bf_ref[...]
    lf = jnp.minimum(fg, 0.0) - jnp.log1p(jnp.exp(-jnp.abs(fg)))
    glu = jnp.dot(h, wglu_ref[...], preferred_element_type=F32)
    u = glu[:, :CONF_CH] * jax.nn.sigmoid(glu[:, CONF_CH:])
    qp = jnp.dot(h, wq_ref[...], preferred_element_type=F32)

    if not prompt:
        q_ref, k_ref, v_ref, lf_ref, u_ref = refs[9:]
        q_ref[...] = qp
    else:
        wk_ref, pq_ref, pk_ref, qa_ref, ka_ref, vt_ref, k_ref, v_ref, lf_ref, u_ref, tri_ref, carry_ref = refs[9:]
        tm = x.shape[0]
        tk = vt_ref.shape[2]
        i = pl.program_id(0)

        @pl.when(i == 0)
        def _():
            carry_ref[...] = jnp.zeros_like(carry_ref)
            r = lax.broadcasted_iota(jnp.int32, (tm, tm), 0)
            c = lax.broadcasted_iota(jnp.int32, (tm, tm), 1)
            tri_ref[...] = (r <= c).astype(BF16)

        lft = lf.T[:FOX_HEADS]
        parts = jnp.concatenate(_split3(lft), axis=0).astype(BF16)
        cs = jnp.dot(parts, tri_ref[...], preferred_element_type=F32)
        ft = cs[0:8] + cs[8:16] + cs[16:24] + carry_ref[...]
        carry_ref[...] = ft[:, tm - 1:tm]

        pieces_t = jnp.concatenate(
            list(_split3(ft * LOG2E)) + [jnp.ones((FOX_HEADS, tm), F32),
                                 jnp.zeros((V7X_LANES - 4 * FOX_HEADS, tm), F32)], axis=0)
        pieces = pieces_t.T.astype(BF16)
        qa_ref[...] = (qp + jnp.dot(pieces, pq_ref[...], preferred_element_type=F32)).T.astype(BF16)
        kp = jnp.dot(h, wk_ref[...], preferred_element_type=F32)
        ka_ref[...] = (kp + jnp.dot(pieces, pk_ref[...], preferred_element_type=F32)).astype(BF16)
        vtt = v.T.astype(BF16)
        for c0 in range(tm // tk):
            vt_ref[c0] = vtt[:, c0 * tk:(c0 + 1) * tk]

    k_ref[...] = k
    v_ref[...] = v
    lf_ref[...] = lf[:, :FOX_HEADS]
    u_ref[...] = u


def _proj0(x, mod, g, w, prompt):
    t = x.shape[0]
    tm = min(ROW_TILE, t)
    mm = mod.shape[0]
    mrow = (lambda i: (0, 0)) if mm == 1 else (lambda i: (i, 0))
    mrow1 = (lambda i: (0, 1)) if mm == 1 else (lambda i: (i, 1))
    bm = 1 if mm == 1 else tm
    row = lambda i: (i, 0)
    wide = FOX_HEADS * AUG
    in_specs = [pl.BlockSpec((tm, D_MODEL), row),
                pl.BlockSpec((bm, D_MODEL), mrow),
                pl.BlockSpec((bm, D_MODEL), mrow1),
                _const_spec((1, D_MODEL)),
                _const_spec(w["q"].shape),
                _const_spec(w["kv"].shape),
                _const_spec(w["f"].shape),
                _const_spec((1, V7X_LANES)),
                _const_spec(w["glu"].shape)]
    args = [x, mod, mod, g, w["q"], w["kv"], w["f"], w["bf"], w["glu"]]
    tail_shape = [jax.ShapeDtypeStruct((t, FOX_WIDTH), F32), jax.ShapeDtypeStruct((t, FOX_WIDTH), F32),
                  jax.ShapeDtypeStruct((t, FOX_HEADS), F32), jax.ShapeDtypeStruct((t, CONF_CH), F32)]
    tail_specs = [pl.BlockSpec((tm, FOX_WIDTH), row), pl.BlockSpec((tm, FOX_WIDTH), row),
                  pl.BlockSpec((tm, FOX_HEADS), row), pl.BlockSpec((tm, CONF_CH), row)]
    if prompt:
        tk = ATTN_TILE
        pq, pk = _bias_placement()
        in_specs += [_const_spec(w["k"].shape), _const_spec(pq.shape), _const_spec(pk.shape)]
        args += [w["k"], pq, pk]
        out_shape = [jax.ShapeDtypeStruct((wide, t), BF16), jax.ShapeDtypeStruct((t, wide), BF16),
                     jax.ShapeDtypeStruct((t // tk, FOX_WIDTH, tk), BF16)] + tail_shape
        out_specs = [pl.BlockSpec((wide, tm), lambda i: (0, i)), pl.BlockSpec((tm, wide), row),
                     pl.BlockSpec((tm // tk, FOX_WIDTH, tk), lambda i: (i, 0, 0))] + tail_specs
        scratch = [pltpu.VMEM((tm, tm), BF16), pltpu.VMEM((FOX_HEADS, 1), F32)]
    else:
        out_shape = [jax.ShapeDtypeStruct((t, wide), F32)] + tail_shape
        out_specs = [pl.BlockSpec((tm, wide), row)] + tail_specs
        scratch = []
    return pl.pallas_call(
        functools.partial(_proj0_kernel, prompt=prompt),
        out_shape=out_shape,
        grid=(t // tm,),
        in_specs=in_specs,
        out_specs=out_specs,
        scratch_shapes=scratch,
        compiler_params=_params("arbitrary"),
        name="proj0",
    )(*args)


def _ln_silu(y, ln_g, ln_b):
    mu = jnp.mean(y, axis=-1, keepdims=True)
    d = y - mu
    var = jnp.mean(d * d, axis=-1, keepdims=True)
    z = d * lax.rsqrt(var + EPS) * ln_g + ln_b
    return z * jax.nn.sigmoid(z)


def _conv0_kernel(u_ref, up_ref, wdw_ref, bdw_ref, lng_ref, lnb_ref, y_ref, buf_ref):
    i = pl.program_id(0)
    tm = u_ref.shape[0]
    buf_ref[0:CONV_HALO, :] = jnp.where(i == 0, 0.0, up_ref[...])
    buf_ref[CONV_HALO:, :] = u_ref[...]
    first = CONV_HALO - (CONF_CONV_WIDTH - 1)
    for c0 in range(0, tm, CONV_CHUNK):
        acc = jnp.zeros((CONV_CHUNK, CONF_CH), F32)
        for kk in range(CONF_CONV_WIDTH):
            acc = acc + wdw_ref[kk:kk + 1, :] * buf_ref[pl.ds(first + c0 + kk, CONV_CHUNK), :]
        z = _ln_silu(acc + bdw_ref[...], lng_ref[...], lnb_ref[...])
        y_ref[c0:c0 + CONV_CHUNK, :] = z.astype(BF16)


def _conv0_prompt(u, w_dw, b_dw, ln_g, ln_b):
    t = u.shape[0]
    tm = CONV_TILE
    per = tm // CONV_HALO
    return pl.pallas_call(
        _conv0_kernel,
        out_shape=jax.ShapeDtypeStruct((t, CONF_CH), BF16),
        grid=(t // tm,),
        in_specs=[pl.BlockSpec((tm, CONF_CH), lambda i: (i, 0)),
                  pl.BlockSpec((CONV_HALO, CONF_CH), lambda i: (jnp.maximum(i * per - 1, 0), 0)),
                  _const_spec((CONF_CONV_WIDTH, CONF_CH)),
                  _const_spec((1, CONF_CH)), _const_spec((1, CONF_CH)), _const_spec((1, CONF_CH))],
        out_specs=pl.BlockSpec((tm, CONF_CH), lambda i: (i, 0)),
        scratch_shapes=[pltpu.VMEM((tm + CONV_HALO, CONF_CH), F32)],
        compiler_params=_params("arbitrary"),
        name="conv0",
    )(u, u, w_dw, b_dw.reshape(1, -1), ln_g.reshape(1, -1), ln_b.reshape(1, -1))


def _conv0s_kernel(st_ref, u_ref, wdw_ref, bdw_ref, lng_ref, lnb_ref, y_ref):
    acc = wdw_ref[CONF_CONV_WIDTH - 1:CONF_CONV_WIDTH, :] * u_ref[...]
    for kk in range(CONF_CONV_WIDTH - 1):
        acc = acc + wdw_ref[kk:kk + 1, :] * st_ref[kk]
    z = _ln_silu(acc + bdw_ref[...], lng_ref[...], lnb_ref[...])
    y_ref[...] = z.astype(BF16)


def _conv0_sample(state_t, u, w_dw, b_dw, ln_g, ln_b):
    b = u.shape[0]
    return pl.pallas_call(
        _conv0s_kernel,
        out_shape=jax.ShapeDtypeStruct((b, CONF_CH), BF16),
        name="conv0s",
    )(state_t, u, w_dw, b_dw.reshape(1, -1), ln_g.reshape(1, -1), ln_b.reshape(1, -1))


def _attn_kernel(qa_ref, ka_ref, vt_ref, o_ref, s_ref):
    i = pl.program_id(1)
    tq = qa_ref.shape[1]
    heads = range(ATTN_HEADS)
    qa = [qa_ref[hh * AUG:(hh + 1) * AUG, :] for hh in heads]

    def scores(j):
        start = pl.multiple_of(j * tq, tq)
        return [jnp.dot(ka_ref[pl.ds(start, tq), hh * AUG:(hh + 1) * AUG], qa[hh],
                        preferred_element_type=F32)
                for hh in heads]

    def fold(j, slot, carry, masked):
        out = []
        for hh in heads:
            m_prev, l_prev, acc = carry[hh]
            st = s_ref[slot, hh]
            if masked:
                key = j * tq + lax.broadcasted_iota(jnp.int32, st.shape, 0)
                qry = i * tq + lax.broadcasted_iota(jnp.int32, st.shape, 1)
                st = jnp.where(key <= qry, st, -jnp.inf)
            m_new = jnp.maximum(m_prev, jnp.max(st, axis=0, keepdims=True))
            alpha = jnp.exp2(m_prev - m_new)
            pt = jnp.exp2(st - m_new)
            l_new = alpha * l_prev + jnp.sum(pt, axis=0, keepdims=True)
            vt = vt_ref[jnp.minimum(j, last), hh * HEAD_DIM:(hh + 1) * HEAD_DIM, :]
            acc = acc * alpha + jnp.dot(vt, pt.astype(BF16), preferred_element_type=F32)
            out.append((m_new, l_new, acc))
        return tuple(out)

    def step(j, slot, carry, masked):
        nxt = scores(jnp.minimum(j + 1, last))
        carry = fold(j, slot, carry, masked)
        for hh in heads:
            s_ref[1 - slot, hh] = nxt[hh]
        return carry

    def pair(p, carry):
        carry = step(2 * p, 0, carry, False)
        return step(2 * p + 1, 1, carry, False)

    last = vt_ref.shape[0] - 1
    first = scores(0)
    for hh in heads:
        s_ref[0, hh] = first[hh]
    init = tuple((jnp.full((1, tq), -jnp.inf, F32), jnp.zeros((1, tq), F32),
                  jnp.zeros((HEAD_DIM, tq), F32)) for _ in heads)
    npair = i // 2
    carry = lax.fori_loop(0, npair, pair, init)
    carry = step(2 * npair, 0, carry, True)
    carry = fold(2 * npair + 1, 1, carry, True)
    ot = jnp.concatenate([acc / l for (_, l, acc) in carry], axis=0)
    o_ref[...] = ot.T.astype(BF16)


def _attn_prompt(qa, ka, vt):
    t = ka.shape[0]
    tq = ATTN_TILE
    nh = ATTN_HEADS
    nblk = t // tq
    once = pl.Buffered(1)
    return pl.pallas_call(
        _attn_kernel,
        out_shape=jax.ShapeDtypeStruct((t, FOX_WIDTH), BF16),
        grid=(FOX_HEADS // nh, nblk),
        in_specs=[pl.BlockSpec((nh * AUG, tq), lambda p, i: (p, i)),
                  pl.BlockSpec((t, nh * AUG), lambda p, i: (0, p), pipeline_mode=once),
                  pl.BlockSpec((nblk, nh * HEAD_DIM, tq), lambda p, i: (0, p, 0), pipeline_mode=once)],
        out_specs=pl.BlockSpec((tq, nh * HEAD_DIM), lambda p, i: (i, p)),
        scratch_shapes=[pltpu.VMEM((2, nh, tq, tq), F32)],
        compiler_params=_params("arbitrary", "arbitrary"),
        name="attn_prompt",
    )(qa, ka, vt)


def _attn_s_kernel(pt_ref, q_ref, kn_ref, vn_ref, lfn_ref, *rest):
    del pt_ref
    npg = PAGES_PER_STEP
    k_refs = rest[:npg]
    v_refs = rest[npg:2 * npg]
    lf_refs = rest[2 * npg:3 * npg]
    o_ref, qb_ref, tri_ref, m_ref, l_ref, run_ref, acc_ref = rest[3 * npg:]
    g = pl.program_id(1)
    lanes = V7X_LANES
    heads = range(FOX_HEADS)
    row = lax.broadcasted_iota(jnp.int32, (FOX_HEADS, lanes), 0)
    lane = lax.broadcasted_iota(jnp.int32, (FOX_HEADS, lanes), 1)

    def rows_to_tile(vals):
        out = jnp.zeros((FOX_HEADS, lanes), F32)
        for h in heads:
            out = jnp.where(row == h, jnp.broadcast_to(vals[h], (FOX_HEADS, lanes)), out)
        return out

    @pl.when(g == 0)
    def _():
        tr = lax.broadcasted_iota(jnp.int32, (PAGE_SIZE, 2 * lanes), 0)
        tc = lax.broadcasted_iota(jnp.int32, (PAGE_SIZE, 2 * lanes), 1)
        tri_ref[...] = ((tr > tc) | (tc >= lanes)).astype(BF16)
        run_ref[...] = jnp.broadcast_to(lfn_ref[0], (FOX_HEADS, lanes))
        s_new = []
        for h in heads:
            qb_ref[h] = jnp.broadcast_to(q_ref[0, h], (HEAD_DIM, lanes))
            s_new.append(jnp.sum(q_ref[0, h] * kn_ref[0, h], axis=0, keepdims=True))
            acc_ref[h] = jnp.where(lane[:1] == 0, jnp.broadcast_to(vn_ref[0, h], (HEAD_DIM, lanes)), 0.0)
        m_ref[...] = rows_to_tile([jnp.broadcast_to(s, (1, lanes)) for s in s_new])
        l_ref[...] = jnp.ones_like(l_ref)

    parts = jnp.concatenate([p for i in range(npg) for p in _split3(lf_refs[i][0])], axis=0)
    cs = jnp.dot(parts.astype(BF16), tri_ref[...], preferred_element_type=F32)
    run = run_ref[...]
    scores = []
    for i in range(npg):
        c3 = cs[24 * i:24 * i + 8] + cs[24 * i + 8:24 * i + 16] + cs[24 * i + 16:24 * i + 24]
        qk = rows_to_tile([jnp.sum(k_refs[i][0, h] * qb_ref[h], axis=0, keepdims=True) for h in heads])
        scores.append(qk + (c3[:, :lanes] + run) * LOG2E)
        run = run + c3[:, lanes:]
    run_ref[...] = run

    m_prev = m_ref[...]
    m_blk = scores[0]
    for s in scores[1:]:
        m_blk = jnp.maximum(m_blk, s)
    m_new = jnp.maximum(m_prev, jnp.max(m_blk, axis=1, keepdims=True))
    alpha = jnp.exp2(m_prev - m_new)
    ps = [jnp.exp2(s - m_new) for s in scores]
    p_tot = ps[0]
    for p in ps[1:]:
        p_tot = p_tot + p
    l_ref[...] = alpha * l_ref[...] + jnp.sum(p_tot, axis=1, keepdims=True)
    m_ref[...] = m_new
    for h in heads:
        acc = acc_ref[h] * alpha[h:h + 1, :]
        for i in range(npg):
            acc = acc + v_refs[i][0, h] * ps[i][h:h + 1, :]
        acc_ref[h] = acc

    @pl.when(g == pl.num_programs(1) - 1)
    def _():
        for h in heads:
            o_ref[0, h] = jnp.sum(acc_ref[h], axis=1, keepdims=True) / l_ref[h:h + 1, 0:1]


def _attn_sample(q, k_new, v_new, lf_new, cache_k, cache_v, cache_logf, page_table):
    b = q.shape[0]
    n_pages = page_table.shape[1]
    npg = PAGES_PER_STEP
    nsteps = n_pages // npg
    col = (FOX_HEADS, HEAD_DIM, 1)
    ck = cache_k.transpose(0, 2, 3, 1)
    cv = cache_v.transpose(0, 2, 3, 1)
    clf = cache_logf.transpose(0, 2, 1)

    def page_map(nd):
        def for_slot(slot):
            return lambda bi, g, pt: (pt[bi, n_pages - 1 - (g * npg + slot)],) + (0,) * (nd - 1)
        return for_slot

    seq4 = lambda bi, g, pt: (bi, 0, 0, 0)
    page_blk = (1, FOX_HEADS, HEAD_DIM, PAGE_SIZE)
    in_specs = [pl.BlockSpec((1,) + col, seq4)] * 3 + [pl.BlockSpec((1, FOX_HEADS, 1), lambda bi, g, pt: (bi, 0, 0))]
    in_specs += [pl.BlockSpec(page_blk, page_map(4)(s)) for s in range(npg)]
    in_specs += [pl.BlockSpec(page_blk, page_map(4)(s)) for s in range(npg)]
    in_specs += [pl.BlockSpec((1, FOX_HEADS, PAGE_SIZE), page_map(3)(s)) for s in range(npg)]
    grid_spec = pltpu.PrefetchScalarGridSpec(
        num_scalar_prefetch=1,
        grid=(b, nsteps),
        in_specs=in_specs,
        out_specs=pl.BlockSpec((1,) + col, seq4),
        scratch_shapes=[pltpu.VMEM((FOX_HEADS, HEAD_DIM, V7X_LANES), F32),
                        pltpu.VMEM((PAGE_SIZE, 2 * V7X_LANES), BF16),
                        pltpu.VMEM((FOX_HEADS, V7X_LANES), F32),
                        pltpu.VMEM((FOX_HEADS, V7X_LANES), F32),
                        pltpu.VMEM((FOX_HEADS, V7X_LANES), F32),
                        pltpu.VMEM((FOX_HEADS, HEAD_DIM, V7X_LANES), F32)])
    out = pl.pallas_call(
        _attn_s_kernel,
        out_shape=jax.ShapeDtypeStruct((b,) + col, F32),
        grid_spec=grid_spec,
        compiler_params=_params("arbitrary", "arbitrary"),
        name="attn_sample",
    )(page_table, q.reshape((b,) + col), k_new.reshape((b,) + col), v_new.reshape((b,) + col),
      lf_new.reshape(b, FOX_HEADS, 1), *([ck] * npg), *([cv] * npg), *([clf] * npg))
    return out.reshape(b, FOX_WIDTH).astype(BF16)


def _post_kernel(*refs, n_mix, final):
    x_ref = refs[0]
    mix_refs = refs[1:1 + 2 * n_mix]
    gm_ref, shf_ref, scf_ref, gf_ref, g_ref, wup_ref, wdn_ref = refs[1 + 2 * n_mix:8 + 2 * n_mix]
    rest = refs[8 + 2 * n_mix:]
    if final:
        gfin_ref, o_ref = rest
    else:
        (o_ref,) = rest
    mix = None
    for a in range(n_mix):
        t = jnp.dot(mix_refs[2 * a][...], mix_refs[2 * a + 1][...], preferred_element_type=F32)
        mix = t if mix is None else mix + t
    x1 = x_ref[...] + gm_ref[...] * mix
    h = _modnorm(x1, g_ref[...], shf_ref[...], scf_ref[...]).astype(BF16)
    fchunk = 1024
    acc = None
    for f0 in range(0, D_FF, fchunk):
        a = jnp.maximum(jnp.dot(h, wup_ref[:, f0:f0 + fchunk], preferred_element_type=F32), 0.0)
        t = jnp.dot((a * a).astype(BF16), wdn_ref[f0:f0 + fchunk, :], preferred_element_type=F32)
        acc = t if acc is None else acc + t
    x2 = x1 + gf_ref[...] * acc
    if final:
        ms = jnp.mean(x2 * x2, axis=-1, keepdims=True)
        x2 = x2 * lax.rsqrt(ms + EPS) * gfin_ref[...]
    o_ref[...] = x2


def _post(x, mixes, mod, g_mlp, w_up, w_down, final_g=None):
    t = x.shape[0]
    tm = min(ROW_TILE, t)
    mm = mod.shape[0]
    bm = 1 if mm == 1 else tm
    row = lambda i: (i, 0)

    def mcol(c):
        return (lambda i: (0, c)) if mm == 1 else (lambda i: (i, c))

    in_specs = [pl.BlockSpec((tm, D_MODEL), row)]
    args = [x]
    for a, w in mixes:
        in_specs += [pl.BlockSpec((tm, a.shape[1]), row), _const_spec(w.shape)]
        args += [a, w]
    in_specs += [pl.BlockSpec((bm, D_MODEL), mcol(2)), pl.BlockSpec((bm, D_MODEL), mcol(3)),
                 pl.BlockSpec((bm, D_MODEL), mcol(4)), pl.BlockSpec((bm, D_MODEL), mcol(5)),
                 _const_spec((1, D_MODEL)), _const_spec(w_up.shape), _const_spec(w_down.shape)]
    args += [mod, mod, mod, mod, g_mlp, w_up, w_down]
    if final_g is not None:
        in_specs.append(_const_spec((1, D_MODEL)))
        args.append(final_g)
    return pl.pallas_call(
        functools.partial(_post_kernel, n_mix=len(mixes), final=final_g is not None),
        out_shape=jax.ShapeDtypeStruct((t, D_MODEL), F32),
        grid=(t // tm,),
        in_specs=in_specs,
        out_specs=pl.BlockSpec((tm, D_MODEL), row),
        compiler_params=_params("arbitrary"),
        name="post",
    )(*args)


def _mix1_kernel(x_ref, sh_ref, sc_ref, g_ref, win_ref, wdw_ref, *rest, per_row_state):
    x = x_ref[...]
    tm = x.shape[0]
    h = _modnorm(x, g_ref[...], sh_ref[...], sc_ref[...]).astype(BF16)
    proj = jnp.dot(h, win_ref[...], preferred_element_type=F32)
    b_gate = proj[:, :D_MODEL]
    cx = proj[:, D_MODEL:2 * D_MODEL] * proj[:, 2 * D_MODEL:]
    w0, w1, w2 = wdw_ref[0:1, :], wdw_ref[1:2, :], wdw_ref[2:3, :]
    if per_row_state:
        s0_ref, s1_ref, by_ref, cx_ref = rest
        y = w0 * s0_ref[...] + w1 * s1_ref[...] + w2 * cx
        cx_ref[...] = cx
    else:
        by_ref, tail_ref, buf_ref = rest
        i = pl.program_id(0)
        pad = V7X_SUBLANES

        @pl.when(i == 0)
        def _():
            buf_ref[0:pad, :] = jnp.zeros((pad, D_MODEL), F32)

        buf_ref[pad:, :] = cx
        y = (w0 * buf_ref[pl.ds(pad - 2, tm), :] + w1 * buf_ref[pl.ds(pad - 1, tm), :] + w2 * cx)
        tail = cx[tm - pad:, :]
        buf_ref[0:pad, :] = tail
        tail_ref[...] = tail
    by_ref[...] = (b_gate * y).astype(BF16)


def _mix1(x, mod, g, w_in, w_dw, state=None):
    t = x.shape[0]
    tm = min(ROW_TILE, t)
    mm = mod.shape[0]
    bm = 1 if mm == 1 else tm
    row = lambda i: (i, 0)
    mcol = lambda c: (lambda i: (0, c)) if mm == 1 else (lambda i: (i, c))
    in_specs = [pl.BlockSpec((tm, D_MODEL), row),
                pl.BlockSpec((bm, D_MODEL), mcol(0)), pl.BlockSpec((bm, D_MODEL), mcol(1)),
                _const_spec((1, D_MODEL)), _const_spec(w_in.shape), _const_spec(w_dw.shape)]
    args = [x, mod, mod, g, w_in, w_dw]
    if state is not None:
        in_specs += [pl.BlockSpec((tm, D_MODEL), row)] * 2
        args += [state[0], state[1]]
        out_shape = [jax.ShapeDtypeStruct((t, D_MODEL), BF16), jax.ShapeDtypeStruct((t, D_MODEL), F32)]
        out_specs = [pl.BlockSpec((tm, D_MODEL), row)] * 2
        scratch = []
    else:
        nt = t // tm
        out_shape = [jax.ShapeDtypeStruct((t, D_MODEL), BF16),
                     jax.ShapeDtypeStruct((nt * V7X_SUBLANES, D_MODEL), F32)]
        out_specs = [pl.BlockSpec((tm, D_MODEL), row), pl.BlockSpec((V7X_SUBLANES, D_MODEL), row)]
        scratch = [pltpu.VMEM((tm + V7X_SUBLANES, D_MODEL), F32)]
    return pl.pallas_call(
        functools.partial(_mix1_kernel, per_row_state=state is not None),
        out_shape=out_shape,
        grid=(t // tm,),
        in_specs=in_specs,
        out_specs=out_specs,
        scratch_shapes=scratch,
        compiler_params=_params("arbitrary"),
        name="mix1",
    )(*args)


def kernel(x_prompt, x_sample, cache_k, cache_v, cache_logf, state_conformer_conv, state_short_conv, page_table, c_prompt, c_sample, l0_w_ada, l0_b_ada, l0_norm_mix, l0_norm_mlp, l0_w_in, l0_b_forget, l0_w_dw, l0_b_dw, l0_conv_ln_g, l0_conv_ln_b, l0_w_out, l0_w_up, l0_w_down, l1_w_ada, l1_b_ada, l1_norm_mix, l1_norm_mlp, l1_w_in, l1_w_dw, l1_w_out, l1_w_up, l1_w_down, final_norm):
    bp, t, d = x_prompt.shape
    bs = x_sample.shape[0]
    assert bp == 1 and d == D_MODEL and x_sample.shape[1] == 1

    def per_head_padded(wcols, scale):
        w3 = (wcols * scale).reshape(d, FOX_HEADS, HEAD_DIM)
        return jnp.pad(w3, ((0, 0), (0, 0), (0, AUG - HEAD_DIM))).reshape(d, FOX_HEADS * AUG).astype(BF16)

    w0 = {
        "q": per_head_padded(l0_w_in[:, :FOX_WIDTH], ATTN_SCALE * LOG2E),
        "k": per_head_padded(l0_w_in[:, FOX_WIDTH:2 * FOX_WIDTH], 1.0),
        "kv": l0_w_in[:, FOX_WIDTH:3 * FOX_WIDTH].astype(BF16),
        "f":jnp.pad(l0_w_in[:, 3 * FOX_WIDTH:3 * FOX_WIDTH + FOX_HEADS],
                     ((0, 0), (0, V7X_LANES - FOX_HEADS))).astype(BF16),
        "bf": jnp.pad(l0_b_forget, (0, V7X_LANES - FOX_HEADS)).reshape(1, V7X_LANES),
        "glu": l0_w_in[:, 3 * FOX_WIDTH + FOX_HEADS:].astype(BF16),
    }
    w0_out_attn = l0_w_out[:FOX_WIDTH].astype(BF16)
    w0_out_conv = l0_w_out[FOX_WIDTH:].astype(BF16)
    w0_up, w0_down = l0_w_up.astype(BF16), l0_w_down.astype(BF16)
    w1_in, w1_out = l1_w_in.astype(BF16), l1_w_out.astype(BF16)
    w1_up, w1_down = l1_w_up.astype(BF16), l1_w_down.astype(BF16)
    g0_mix, g0_mlp = l0_norm_mix.reshape(1, d), l0_norm_mlp.reshape(1, d)
    g1_mix, g1_mlp = l1_norm_mix.reshape(1, d), l1_norm_mlp.reshape(1, d)
    gfin = final_norm.reshape(1, d)

    c_all = jnp.concatenate([c_prompt, c_sample], axis=0)
    mod0 = _ada(c_all, l0_w_ada, l0_b_ada)
    mod1 = _ada(c_all, l1_w_ada, l1_b_ada)
    mod0_p, mod0_s = mod0[:1], mod0[1:]
    mod1_p, mod1_s = mod1[:1], mod1[1:]

    xp = x_prompt.reshape(t, d)
    qa_p, ka_p, vt_p, k_p, v_p, lf_p, u_p = _proj0(xp, mod0_p, g0_mix, w0, True)
    yc_p = _conv0_prompt(u_p, l0_w_dw, l0_b_dw, l0_conv_ln_g, l0_conv_ln_b)
    at_p = _attn_prompt(qa_p, ka_p, vt_p)
    x1_p = _post(xp, [(at_p, w0_out_attn), (yc_p, w0_out_conv)], mod0_p, g0_mlp, w0_up, w0_down)
    by_p, tail_p = _mix1(x1_p, mod1_p, g1_mix, w1_in, l1_w_dw)
    y_p = _post(x1_p, [(by_p, w1_out)], mod1_p, g1_mlp, w1_up, w1_down, gfin)

    xs = x_sample.reshape(bs, d)
    qp_s, k_s, v_s, lf_s, u_s = _proj0(xs, mod0_s, g0_mix, w0, False)
    q_s = qp_s.reshape(bs, FOX_HEADS, AUG)[:, :, :HEAD_DIM].reshape(bs, FOX_WIDTH)
    yc_s = _conv0_sample(state_conformer_conv.transpose(1, 0, 2), u_s,
                         l0_w_dw, l0_b_dw, l0_conv_ln_g, l0_conv_ln_b)
    at_s = _attn_sample(q_s, k_s, v_s, lf_s, cache_k, cache_v, cache_logf, page_table)
    x1_s = _post(xs, [(at_s, w0_out_attn), (yc_s, w0_out_conv)], mod0_s, g0_mlp, w0_up, w0_down)
    by_s, cx_s = _mix1(x1_s, mod1_s, g1_mix, w1_in, l1_w_dw,
                       state=(state_short_conv[:, 0], state_short_conv[:, 1]))
    y_s = _post(x1_s, [(by_s, w1_out)], mod1_s, g1_mlp, w1_up, w1_down, gfin)

    hs = (FOX_HEADS, HEAD_DIM)
    return (y_p.reshape(1, t, d), y_s.reshape(bs, 1, d),
            k_p.reshape(1, t, *hs), v_p.reshape(1, t, *hs), lf_p.reshape(1, t, FOX_HEADS),
            u_p[t - (CONF_CONV_WIDTH - 1):].reshape(1, CONF_CONV_WIDTH - 1, CONF_CH),
            tail_p[-(SC_CONV_WIDTH - 1):].reshape(1, SC_CONV_WIDTH - 1, d),
            k_s.reshape(bs, 1, *hs), v_s.reshape(bs, 1, *hs), lf_s.reshape(bs, 1, FOX_HEADS),
            jnp.concatenate([state_conformer_conv[:, 1:], u_s[:, None, :]], axis=1),
            jnp.stack([state_short_conv[:, 1], cx_s], axis=1))
```

```python
import functools

import jax
import jax.numpy as jnp
from jax import lax
from jax.experimental import pallas as pl
from jax.experimental.pallas import tpu as pltpu

F32 = jnp.float32
BF16 = jnp.bfloat16

D_MODEL = 1024
FOX_HEADS = 8
HEAD_DIM = 64
FOX_WIDTH = FOX_HEADS * HEAD_DIM
CONF_CH = D_MODEL - FOX_WIDTH
CONF_CONV_WIDTH = 31
SC_CONV_WIDTH = 3
D_FF = 4 * D_MODEL
PAGE_SIZE = 128
EPS = 1e-6
ATTN_SCALE = HEAD_DIM ** -0.5
LOG2E = 1.4426950408889634

V7X_LANES = 128
V7X_SUBLANES = 8
V7X_VMEM_BYTES = 64 * 1024 * 1024
VMEM_LIMIT = 52 * 1024 * 1024

ROW_TILE = 512
CONV_TILE = 256
CONV_CHUNK = 64
CONV_HALO = 32
ATTN_TILE = 256
AUG = V7X_LANES
ATTN_HEADS = 4
ATTN_UNROLL = 2
PAGES_PER_STEP = 16


def _const_spec(shape):
    nd = len(shape)
    return pl.BlockSpec(shape, lambda *_: (0,) * nd, pipeline_mode=pl.Buffered(1))


def _params(*sem):
    return pltpu.CompilerParams(dimension_semantics=sem, vmem_limit_bytes=VMEM_LIMIT)


def _modnorm(x, g, shift, scale):
    ms = jnp.mean(x * x, axis=-1, keepdims=True)
    y = x * lax.rsqrt(ms + EPS)
    return (y * g) * (1.0 + scale) + shift


def _split3(x):
    hi = x.astype(BF16).astype(F32)
    r = x - hi
    mid = r.astype(BF16).astype(F32)
    lo = (r - mid).astype(BF16).astype(F32)
    return hi, mid, lo


def _ada_kernel(c_ref, w_ref, b_ref, o_ref):
    c = c_ref[...]
    s = (c * jax.nn.sigmoid(c)).astype(BF16)
    o_ref[...] = jnp.dot(s, w_ref[...].astype(BF16), preferred_element_type=F32) + b_ref[...]


def _ada(c_all, w_ada, b_ada):
    m = c_all.shape[0]
    n = w_ada.shape[1]
    bn = 1536
    return pl.pallas_call(
        _ada_kernel,
        out_shape=jax.ShapeDtypeStruct((m, n), F32),
        grid=(n // bn,),
        in_specs=[pl.BlockSpec((m, D_MODEL), lambda j: (0, 0)),
                  pl.BlockSpec((D_MODEL, bn), lambda j: (0, j)),
                  pl.BlockSpec((1, bn), lambda j: (0, j))],
        out_specs=pl.BlockSpec((m, bn), lambda j: (0, j)),
        compiler_params=_params("arbitrary"),
        name="ada",
    )(c_all, w_ada, b_ada.reshape(1, n))


def _bias_placement():
    import numpy as np
    pq = np.zeros((V7X_LANES, FOX_HEADS * AUG), np.float32)
    pk = np.zeros((V7X_LANES, FOX_HEADS * AUG), np.float32)
    for h in range(FOX_HEADS):
        for p in range(3):
            pq[p * FOX_HEADS + h, h * AUG + HEAD_DIM + p] = 1.0
            pk[p * FOX_HEADS + h, h * AUG + HEAD_DIM + 3 + p] = -1.0
            pq[3 * FOX_HEADS, h * AUG + HEAD_DIM + 3 + p] = 1.0
            pk[3 * FOX_HEADS, h * AUG + HEAD_DIM + p] = 1.0
    return jnp.asarray(pq, BF16), jnp.asarray(pk, BF16)


def _proj0_kernel(*refs, prompt):
    x_ref, sh_ref, sc_ref, g_ref, wq_ref, wkv_ref, wf_ref, bf_ref, wglu_ref = refs[:9]
    x = x_ref[...]
    h = _modnorm(x, g_ref[...], sh_ref[...], sc_ref[...]).astype(BF16)
    kv = jnp.dot(h, wkv_ref[...], preferred_element_type=F32)
    k = kv[:, :FOX_WIDTH]
    v = kv[:, FOX_WIDTH:]
    fg = jnp.dot(h, wf_ref[...], preferred_element_type=F32) + bf_ref[...]
    lf = jnp.minimum(fg, 0.0) - jnp.log1p(jnp.exp(-jnp.abs(fg)))
    glu = jnp.dot(h, wglu_ref[...], preferred_element_type=F32)
    u = glu[:, :CONF_CH] * jax.nn.sigmoid(glu[:, CONF_CH:])
    qp = jnp.dot(h, wq_ref[...], preferred_element_type=F32)

    if not prompt:
        q_ref, k_ref, v_ref, lf_ref, u_ref = refs[9:]
        q_ref[...] = qp
    else:
        wk_ref, pq_ref, pk_ref, qa_ref, ka_ref, vt_ref, k_ref, v_ref, lf_ref, u_ref, tri_ref, carry_ref = refs[9:]
        tm = x.shape[0]
        tk = vt_ref.shape[2]
        i = pl.program_id(0)

        @pl.when(i == 0)
        def _():
            carry_ref[...] = jnp.zeros_like(carry_ref)
            r = lax.broadcasted_iota(jnp.int32, (tm, tm), 0)
            c = lax.broadcasted_iota(jnp.int32, (tm, tm), 1)
            tri_ref[...] = (r <= c).astype(BF16)

        lft = lf.T[:FOX_HEADS]
        parts = jnp.concatenate(_split3(lft), axis=0).astype(BF16)
        cs = jnp.dot(parts, tri_ref[...], preferred_element_type=F32)
        ft = cs[0:8] + cs[8:16] + cs[16:24] + carry_ref[...]
        carry_ref[...] = ft[:, tm - 1:tm]

        pieces_t = jnp.concatenate(
            list(_split3(ft * LOG2E)) + [jnp.ones((FOX_HEADS, tm), F32),
                                 jnp.zeros((V7X_LANES - 4 * FOX_HEADS, tm), F32)], axis=0)
        pieces = pieces_t.T.astype(BF16)
        qa_ref[...] = (qp + jnp.dot(pieces, pq_ref[...], preferred_element_type=F32)).T.astype(BF16)
        kp = jnp.dot(h, wk_ref[...], preferred_element_type=F32)
        ka_ref[...] = (kp + jnp.dot(pieces, pk_ref[...], preferred_element_type=F32)).astype(BF16)
        vtt = v.T.astype(BF16)
        for c0 in range(tm // tk):
            vt_ref[c0] = vtt[:, c0 * tk:(c0 + 1) * tk]

    k_ref[...] = k
    v_ref[...] = v
    lf_ref[...] = lf[:, :FOX_HEADS]
    u_ref[...] = u


def _proj0(x, mod, g, w, prompt):
    t = x.shape[0]
    tm = min(ROW_TILE, t)
    mm = mod.shape[0]
    mrow = (lambda i: (0, 0)) if mm == 1 else (lambda i: (i, 0))
    mrow1 = (lambda i: (0, 1)) if mm == 1 else (lambda i: (i, 1))
    bm = 1 if mm == 1 else tm
    row = lambda i: (i, 0)
    wide = FOX_HEADS * AUG
    in_specs = [pl.BlockSpec((tm, D_MODEL), row),
                pl.BlockSpec((bm, D_MODEL), mrow),
                pl.BlockSpec((bm, D_MODEL), mrow1),
                _const_spec((1, D_MODEL)),
                _const_spec(w["q"].shape),
                _const_spec(w["kv"].shape),
                _const_spec(w["f"].shape),
                _const_spec((1, V7X_LANES)),
                _const_spec(w["glu"].shape)]
    args = [x, mod, mod, g, w["q"], w["kv"], w["f"], w["bf"], w["glu"]]
    tail_shape = [jax.ShapeDtypeStruct((t, FOX_WIDTH), F32), jax.ShapeDtypeStruct((t, FOX_WIDTH), F32),
                  jax.ShapeDtypeStruct((t, FOX_HEADS), F32), jax.ShapeDtypeStruct((t, CONF_CH), F32)]
    tail_specs = [pl.BlockSpec((tm, FOX_WIDTH), row), pl.BlockSpec((tm, FOX_WIDTH), row),
                  pl.BlockSpec((tm, FOX_HEADS), row), pl.BlockSpec((tm, CONF_CH), row)]
    if prompt:
        tk = ATTN_TILE
        pq, pk = _bias_placement()
        in_specs += [_const_spec(w["k"].shape), _const_spec(pq.shape), _const_spec(pk.shape)]
        args += [w["k"], pq, pk]
        out_shape = [jax.ShapeDtypeStruct((wide, t), BF16), jax.ShapeDtypeStruct((t, wide), BF16),
                     jax.ShapeDtypeStruct((t // tk, FOX_WIDTH, tk), BF16)] + tail_shape
        out_specs = [pl.BlockSpec((wide, tm), lambda i: (0, i)), pl.BlockSpec((tm, wide), row),
                     pl.BlockSpec((tm // tk, FOX_WIDTH, tk), lambda i: (i, 0, 0))] + tail_specs
        scratch = [pltpu.VMEM((tm, tm), BF16), pltpu.VMEM((FOX_HEADS, 1), F32)]
    else:
        out_shape = [jax.ShapeDtypeStruct((t, wide), F32)] + tail_shape
        out_specs = [pl.BlockSpec((tm, wide), row)] + tail_specs
        scratch = []
    return pl.pallas_call(
        functools.partial(_proj0_kernel, prompt=prompt),
        out_shape=out_shape,
        grid=(t // tm,),
        in_specs=in_specs,
        out_specs=out_specs,
        scratch_shapes=scratch,
        compiler_params=_params("arbitrary"),
        name="proj0",
    )(*args)


def _ln_silu(y, ln_g, ln_b):
    mu = jnp.mean(y, axis=-1, keepdims=True)
    d = y - mu
    var = jnp.mean(d * d, axis=-1, keepdims=True)
    z = d * lax.rsqrt(var + EPS) * ln_g + ln_b
    return z * jax.nn.sigmoid(z)


def _conv0_kernel(u_ref, up_ref, wdw_ref, bdw_ref, lng_ref, lnb_ref, y_ref, buf_ref):
    i = pl.program_id(0)
    tm = u_ref.shape[0]
    buf_ref[0, 0:CONV_HALO, :] = jnp.where(i == 0, 0.0, up_ref[...])
    buf_ref[0, CONV_HALO:, :] = u_ref[...]
    span = tm + CONV_HALO - V7X_SUBLANES
    for r in range(1, V7X_SUBLANES):
        buf_ref[r, 0:span, :] = buf_ref[0, pl.ds(r, span), :]
    first = CONV_HALO - (CONF_CONV_WIDTH - 1)
    for c0 in range(0, tm, CONV_CHUNK):
        acc = jnp.zeros((CONV_CHUNK, CONF_CH), F32)
        for kk in range(CONF_CONV_WIDTH):
            r = (first + kk) % V7X_SUBLANES
            acc = acc + wdw_ref[kk:kk + 1, :] * buf_ref[r, pl.ds(first + kk - r + c0, CONV_CHUNK), :]
        z = _ln_silu(acc + bdw_ref[...], lng_ref[...], lnb_ref[...])
        y_ref[c0:c0 + CONV_CHUNK, :] = z.astype(BF16)


def _conv0_prompt(u, w_dw, b_dw, ln_g, ln_b):
    t = u.shape[0]
    tm = CONV_TILE
    per = tm // CONV_HALO
    return pl.pallas_call(
        _conv0_kernel,
        out_shape=jax.ShapeDtypeStruct((t, CONF_CH), BF16),
        grid=(t // tm,),
        in_specs=[pl.BlockSpec((tm, CONF_CH), lambda i: (i, 0)),
                  pl.BlockSpec((CONV_HALO, CONF_CH), lambda i: (jnp.maximum(i * per - 1, 0), 0)),
                  _const_spec((CONF_CONV_WIDTH, CONF_CH)),
                  _const_spec((1, CONF_CH)), _const_spec((1, CONF_CH)), _const_spec((1, CONF_CH))],
        out_specs=pl.BlockSpec((tm, CONF_CH), lambda i: (i, 0)),
        scratch_shapes=[pltpu.VMEM((V7X_SUBLANES, tm + CONV_HALO, CONF_CH), F32)],
        compiler_params=_params("arbitrary"),
        name="conv0",
    )(u, u, w_dw, b_dw.reshape(1, -1), ln_g.reshape(1, -1), ln_b.reshape(1, -1))


def _conv0s_kernel(st_ref, u_ref, wdw_ref, bdw_ref, lng_ref, lnb_ref, y_ref):
    acc = wdw_ref[CONF_CONV_WIDTH - 1:CONF_CONV_WIDTH, :] * u_ref[...]
    for kk in range(CONF_CONV_WIDTH - 1):
        acc = acc + wdw_ref[kk:kk + 1, :] * st_ref[kk]
    z = _ln_silu(acc + bdw_ref[...], lng_ref[...], lnb_ref[...])
    y_ref[...] = z.astype(BF16)


def _conv0_sample(state_t, u, w_dw, b_dw, ln_g, ln_b):
    b = u.shape[0]
    return pl.pallas_call(
        _conv0s_kernel,
        out_shape=jax.ShapeDtypeStruct((b, CONF_CH), BF16),
        name="conv0s",
    )(state_t, u, w_dw, b_dw.reshape(1, -1), ln_g.reshape(1, -1), ln_b.reshape(1, -1))


def _attn_kernel(qa_ref, ka_ref, vt_ref, o_ref, s_ref, acc_ref):
    i = pl.program_id(1)
    tq = qa_ref.shape[1]
    heads = range(ATTN_HEADS)
    qa = [qa_ref[hh * AUG:(hh + 1) * AUG, :] for hh in heads]

    def scores(j):
        start = pl.multiple_of(j * tq, tq)
        return [jnp.dot(ka_ref[pl.ds(start, tq), hh * AUG:(hh + 1) * AUG], qa[hh],
                        preferred_element_type=F32)
                for hh in heads]

    def fold(j, slot, carry, masked):
        out = []
        for hh in heads:
            m_prev, l_prev = carry[hh]
            st = s_ref[slot, hh]
            if masked:
                key = j * tq + lax.broadcasted_iota(jnp.int32, st.shape, 0)
                qry = i * tq + lax.broadcasted_iota(jnp.int32, st.shape, 1)
                st = jnp.where(key <= qry, st, -jnp.inf)
            m_new = jnp.maximum(m_prev, jnp.max(st, axis=0, keepdims=True))
            alpha = jnp.exp2(m_prev - m_new)
            pt = jnp.exp2(st - m_new)
            l_new = alpha * l_prev + jnp.sum(pt, axis=0, keepdims=True)
            vt = vt_ref[jnp.minimum(j, last), hh * HEAD_DIM:(hh + 1) * HEAD_DIM, :]
            acc_ref[hh] = acc_ref[hh] * alpha + jnp.dot(vt, pt.astype(BF16), preferred_element_type=F32)
            out.append((m_new, l_new))
        return tuple(out)

    def step(j, slot, carry, masked):
        nxt = scores(jnp.minimum(j + 1, last))
        carry = fold(j, slot, carry, masked)
        for hh in heads:
            s_ref[1 - slot, hh] = nxt[hh]
        return carry

    def group(gi, carry):
        for u in range(ATTN_UNROLL):
            carry = step(ATTN_UNROLL * gi + u, u % 2, carry, False)
        return carry

    last = vt_ref.shape[0] - 1
    first = scores(0)
    for hh in heads:
        s_ref[0, hh] = first[hh]
    acc_ref[...] = jnp.zeros_like(acc_ref)
    init = tuple((jnp.full((1, tq), -jnp.inf, F32), jnp.zeros((1, tq), F32)) for _ in heads)
    ngroup = i // ATTN_UNROLL
    carry = lax.fori_loop(0, ngroup, group, init)
    for u in range(ATTN_UNROLL - 1):
        carry = step(ATTN_UNROLL * ngroup + u, u % 2, carry, True)
    carry = fold(ATTN_UNROLL * ngroup + ATTN_UNROLL - 1, (ATTN_UNROLL - 1) % 2, carry, True)
    ot = jnp.concatenate([acc_ref[hh] / carry[hh][1] for hh in heads], axis=0)
    o_ref[...] = ot.T.astype(BF16)


def _attn_prompt(qa, ka, vt):
    t = ka.shape[0]
    tq = ATTN_TILE
    nh = ATTN_HEADS
    nblk = t // tq
    once = pl.Buffered(1)
    return pl.pallas_call(
        _attn_kernel,
        out_shape=jax.ShapeDtypeStruct((t, FOX_WIDTH), BF16),
        grid=(FOX_HEADS // nh, nblk),
        in_specs=[pl.BlockSpec((nh * AUG, tq), lambda p, i: (p, i)),
                  pl.BlockSpec((t, nh * AUG), lambda p, i: (0, p), pipeline_mode=once),
                  pl.BlockSpec((nblk, nh * HEAD_DIM, tq), lambda p, i: (0, p, 0), pipeline_mode=once)],
        out_specs=pl.BlockSpec((tq, nh * HEAD_DIM), lambda p, i: (i, p)),
        scratch_shapes=[pltpu.VMEM((2, nh, tq, tq), F32),
                        pltpu.VMEM((nh, HEAD_DIM, tq), F32)],
        compiler_params=_params("arbitrary", "arbitrary"),
        name="attn_prompt",
    )(qa, ka, vt)


def _attn_s_kernel(pt_ref, q_ref, kn_ref, vn_ref, lfn_ref, *rest):
    del pt_ref
    npg = PAGES_PER_STEP
    k_refs = rest[:npg]
    v_refs = rest[npg:2 * npg]
    lf_refs = rest[2 * npg:3 * npg]
    o_ref, qb_ref, tri_ref, m_ref, l_ref, run_ref, acc_ref = rest[3 * npg:]
    g = pl.program_id(1)
    lanes = V7X_LANES
    heads = range(FOX_HEADS)
    row = lax.broadcasted_iota(jnp.int32, (FOX_HEADS, lanes), 0)
    lane = lax.broadcasted_iota(jnp.int32, (FOX_HEADS, lanes), 1)

    def rows_to_tile(vals):
        out = jnp.zeros((FOX_HEADS, lanes), F32)
        for h in heads:
            out = jnp.where(row == h, jnp.broadcast_to(vals[h], (FOX_HEADS, lanes)), out)
        return out

    @pl.when(g == 0)
    def _():
        tr = lax.broadcasted_iota(jnp.int32, (PAGE_SIZE, 2 * lanes), 0)
        tc = lax.broadcasted_iota(jnp.int32, (PAGE_SIZE, 2 * lanes), 1)
        tri_ref[...] = ((tr > tc) | (tc >= lanes)).astype(BF16)
        run_ref[...] = jnp.broadcast_to(lfn_ref[0], (FOX_HEADS, lanes))
        s_new = []
        for h in heads:
            qb_ref[h] = jnp.broadcast_to(q_ref[0, h], (HEAD_DIM, lanes))
            s_new.append(jnp.sum(q_ref[0, h] * kn_ref[0, h], axis=0, keepdims=True))
            acc_ref[h] = jnp.where(lane[:1] == 0, jnp.broadcast_to(vn_ref[0, h], (HEAD_DIM, lanes)), 0.0)
        m_ref[...] = rows_to_tile([jnp.broadcast_to(s, (1, lanes)) for s in s_new])
        l_ref[...] = jnp.ones_like(l_ref)

    parts = jnp.concatenate([p for i in range(npg) for p in _split3(lf_refs[i][0])], axis=0)
    cs = jnp.dot(parts.astype(BF16), tri_ref[...], preferred_element_type=F32)
    run = run_ref[...]
    scores = []
    for i in range(npg):
        c3 = cs[24 * i:24 * i + 8] + cs[24 * i + 8:24 * i + 16] + cs[24 * i + 16:24 * i + 24]
        qk = rows_to_tile([jnp.sum(k_refs[i][0, h] * qb_ref[h], axis=0, keepdims=True) for h in heads])
        scores.append(qk + (c3[:, :lanes] + run) * LOG2E)
        run = run + c3[:, lanes:]
    run_ref[...] = run

    m_prev = m_ref[...]
    m_blk = scores[0]
    for s in scores[1:]:
        m_blk = jnp.maximum(m_blk, s)
    m_new = jnp.maximum(m_prev, jnp.max(m_blk, axis=1, keepdims=True))
    alpha = jnp.exp2(m_prev - m_new)
    ps = [jnp.exp2(s - m_new) for s in scores]
    p_tot = ps[0]
    for p in ps[1:]:
        p_tot = p_tot + p
    l_ref[...] = alpha * l_ref[...] + jnp.sum(p_tot, axis=1, keepdims=True)
    m_ref[...] = m_new
    for h in heads:
        acc = acc_ref[h] * alpha[h:h + 1, :]
        for i in range(npg):
            acc = acc + v_refs[i][0, h] * ps[i][h:h + 1, :]
        acc_ref[h] = acc

    @pl.when(g == pl.num_programs(1) - 1)
    def _():
        for h in heads:
            o_ref[0, h] = jnp.sum(acc_ref[h], axis=1, keepdims=True) / l_ref[h:h + 1, 0:1]


def _attn_sample(q, k_new, v_new, lf_new, cache_k, cache_v, cache_logf, page_table):
    b = q.shape[0]
    n_pages = page_table.shape[1]
    npg = PAGES_PER_STEP
    nsteps = n_pages // npg
    col = (FOX_HEADS, HEAD_DIM, 1)
    ck = cache_k.transpose(0, 2, 3, 1)
    cv = cache_v.transpose(0, 2, 3, 1)
    clf = cache_logf.transpose(0, 2, 1)

    def page_map(nd):
        def for_slot(slot):
            return lambda bi, g, pt: (pt[bi, n_pages - 1 - (g * npg + slot)],) + (0,) * (nd - 1)
        return for_slot

    seq4 = lambda bi, g, pt: (bi, 0, 0, 0)
    page_blk = (1, FOX_HEADS, HEAD_DIM, PAGE_SIZE)
    in_specs = [pl.BlockSpec((1,) + col, seq4)] * 3 + [pl.BlockSpec((1, FOX_HEADS, 1), lambda bi, g, pt: (bi, 0, 0))]
    in_specs += [pl.BlockSpec(page_blk, page_map(4)(s)) for s in range(npg)]
    in_specs += [pl.BlockSpec(page_blk, page_map(4)(s)) for s in range(npg)]
    in_specs += [pl.BlockSpec((1, FOX_HEADS, PAGE_SIZE), page_map(3)(s)) for s in range(npg)]
    grid_spec = pltpu.PrefetchScalarGridSpec(
        num_scalar_prefetch=1,
        grid=(b, nsteps),
        in_specs=in_specs,
        out_specs=pl.BlockSpec((1,) + col, seq4),
        scratch_shapes=[pltpu.VMEM((FOX_HEADS, HEAD_DIM, V7X_LANES), F32),
                        pltpu.VMEM((PAGE_SIZE, 2 * V7X_LANES), BF16),
                        pltpu.VMEM((FOX_HEADS, V7X_LANES), F32),
                        pltpu.VMEM((FOX_HEADS, V7X_LANES), F32),
                        pltpu.VMEM((FOX_HEADS, V7X_LANES), F32),
                        pltpu.VMEM((FOX_HEADS, HEAD_DIM, V7X_LANES), F32)])
    out = pl.pallas_call(
        _attn_s_kernel,
        out_shape=jax.ShapeDtypeStruct((b,) + col, F32),
        grid_spec=grid_spec,
        compiler_params=_params("arbitrary", "arbitrary"),
        name="attn_sample",
    )(page_table, q.reshape((b,) + col), k_new.reshape((b,) + col), v_new.reshape((b,) + col),
      lf_new.reshape(b, FOX_HEADS, 1), *([ck] * npg), *([cv] * npg), *([clf] * npg))
    return out.reshape(b, FOX_WIDTH).astype(BF16)


def _post_kernel(*refs, n_mix, final):
    x_ref = refs[0]
    mix_refs = refs[1:1 + 2 * n_mix]
    gm_ref, shf_ref, scf_ref, gf_ref, g_ref, wup_ref, wdn_ref = refs[1 + 2 * n_mix:8 + 2 * n_mix]
    rest = refs[8 + 2 * n_mix:]
    if final:
        gfin_ref, o_ref = rest
    else:
        (o_ref,) = rest
    mix = None
    for a in range(n_mix):
        t = jnp.dot(mix_refs[2 * a][...], mix_refs[2 * a + 1][...], preferred_element_type=F32)
        mix = t if mix is None else mix + t
    x1 = x_ref[...] + gm_ref[...] * mix
    h = _modnorm(x1, g_ref[...], shf_ref[...], scf_ref[...]).astype(BF16)
    fchunk = 1024
    acc = None
    for f0 in range(0, D_FF, fchunk):
        a = jnp.maximum(jnp.dot(h, wup_ref[:, f0:f0 + fchunk], preferred_element_type=F32), 0.0)
        t = jnp.dot((a * a).astype(BF16), wdn_ref[f0:f0 + fchunk, :], preferred_element_type=F32)
        acc = t if acc is None else acc + t
    x2 = x1 + gf_ref[...] * acc
    if final:
        ms = jnp.mean(x2 * x2, axis=-1, keepdims=True)
        x2 = x2 * lax.rsqrt(ms + EPS) * gfin_ref[...]
    o_ref[...] = x2


def _post(x, mixes, mod, g_mlp, w_up, w_down, final_g=None):
    t = x.shape[0]
    tm = min(ROW_TILE, t)
    mm = mod.shape[0]
    bm = 1 if mm == 1 else tm
    row = lambda i: (i, 0)

    def mcol(c):
        return (lambda i: (0, c)) if mm == 1 else (lambda i: (i, c))

    in_specs = [pl.BlockSpec((tm, D_MODEL), row)]
    args = [x]
    for a, w in mixes:
        in_specs += [pl.BlockSpec((tm, a.shape[1]), row), _const_spec(w.shape)]
        args += [a, w]
    in_specs += [pl.BlockSpec((bm, D_MODEL), mcol(2)), pl.BlockSpec((bm, D_MODEL), mcol(3)),
                 pl.BlockSpec((bm, D_MODEL), mcol(4)), pl.BlockSpec((bm, D_MODEL), mcol(5)),
                 _const_spec((1, D_MODEL)), _const_spec(w_up.shape), _const_spec(w_down.shape)]
    args += [mod, mod, mod, mod, g_mlp, w_up, w_down]
    if final_g is not None:
        in_specs.append(_const_spec((1, D_MODEL)))
        args.append(final_g)
    return pl.pallas_call(
        functools.partial(_post_kernel, n_mix=len(mixes), final=final_g is not None),
        out_shape=jax.ShapeDtypeStruct((t, D_MODEL), F32),
        grid=(t // tm,),
        in_specs=in_specs,
        out_specs=pl.BlockSpec((tm, D_MODEL), row),
        compiler_params=_params("arbitrary"),
        name="post",
    )(*args)


def _mix1_kernel(x_ref, sh_ref, sc_ref, g_ref, win_ref, wdw_ref, *rest, per_row_state):
    x = x_ref[...]
    tm = x.shape[0]
    h = _modnorm(x, g_ref[...], sh_ref[...], sc_ref[...]).astype(BF16)
    proj = jnp.dot(h, win_ref[...], preferred_element_type=F32)
    b_gate = proj[:, :D_MODEL]
    cx = proj[:, D_MODEL:2 * D_MODEL] * proj[:, 2 * D_MODEL:]
    w0, w1, w2 = wdw_ref[0:1, :], wdw_ref[1:2, :], wdw_ref[2:3, :]
    if per_row_state:
        s0_ref, s1_ref, by_ref, cx_ref = rest
        y = w0 * s0_ref[...] + w1 * s1_ref[...] + w2 * cx
        cx_ref[...] = cx
    else:
        by_ref, tail_ref, buf_ref = rest
        i = pl.program_id(0)
        pad = V7X_SUBLANES

        @pl.when(i == 0)
        def _():
            buf_ref[0:pad, :] = jnp.zeros((pad, D_MODEL), F32)

        buf_ref[pad:, :] = cx
        y = (w0 * buf_ref[pl.ds(pad - 2, tm), :] + w1 * buf_ref[pl.ds(pad - 1, tm), :] + w2 * cx)
        tail = cx[tm - pad:, :]
        buf_ref[0:pad, :] = tail
        tail_ref[...] = tail
    by_ref[...] = (b_gate * y).astype(BF16)


def _mix1(x, mod, g, w_in, w_dw, state=None):
    t = x.shape[0]
    tm = min(ROW_TILE, t)
    mm = mod.shape[0]
    bm = 1 if mm == 1 else tm
    row = lambda i: (i, 0)
    mcol = lambda c: (lambda i: (0, c)) if mm == 1 else (lambda i: (i, c))
    in_specs = [pl.BlockSpec((tm, D_MODEL), row),
                pl.BlockSpec((bm, D_MODEL), mcol(0)), pl.BlockSpec((bm, D_MODEL), mcol(1)),
                _const_spec((1, D_MODEL)), _const_spec(w_in.shape), _const_spec(w_dw.shape)]
    args = [x, mod, mod, g, w_in, w_dw]
    if state is not None:
        in_specs += [pl.BlockSpec((tm, D_MODEL), row)] * 2
        args += [state[0], state[1]]
        out_shape = [jax.ShapeDtypeStruct((t, D_MODEL), BF16), jax.ShapeDtypeStruct((t, D_MODEL), F32)]
        out_specs = [pl.BlockSpec((tm, D_MODEL), row)] * 2
        scratch = []
    else:
        nt = t // tm
        out_shape = [jax.ShapeDtypeStruct((t, D_MODEL), BF16),
                     jax.ShapeDtypeStruct((nt * V7X_SUBLANES, D_MODEL), F32)]
        out_specs = [pl.BlockSpec((tm, D_MODEL), row), pl.BlockSpec((V7X_SUBLANES, D_MODEL), row)]
        scratch = [pltpu.VMEM((tm + V7X_SUBLANES, D_MODEL), F32)]
    return pl.pallas_call(
        functools.partial(_mix1_kernel, per_row_state=state is not None),
        out_shape=out_shape,
        grid=(t // tm,),
        in_specs=in_specs,
        out_specs=out_specs,
        scratch_shapes=scratch,
        compiler_params=_params("arbitrary"),
        name="mix1",
    )(*args)


def kernel(x_prompt, x_sample, cache_k, cache_v, cache_logf, state_conformer_conv, state_short_conv, page_table, c_prompt, c_sample, l0_w_ada, l0_b_ada, l0_norm_mix, l0_norm_mlp, l0_w_in, l0_b_forget, l0_w_dw, l0_b_dw, l0_conv_ln_g, l0_conv_ln_b, l0_w_out, l0_w_up, l0_w_down, l1_w_ada, l1_b_ada, l1_norm_mix, l1_norm_mlp, l1_w_in, l1_w_dw, l1_w_out, l1_w_up, l1_w_down, final_norm):
    bp, t, d = x_prompt.shape
    bs = x_sample.shape[0]
    assert bp == 1 and d == D_MODEL and x_sample.shape[1] == 1

    def per_head_padded(wcols, scale):
        w3 = (wcols * scale).reshape(d, FOX_HEADS, HEAD_DIM)
        return jnp.pad(w3, ((0, 0), (0, 0), (0, AUG - HEAD_DIM))).reshape(d, FOX_HEADS * AUG).astype(BF16)

    w0 = {
        "q": per_head_padded(l0_w_in[:, :FOX_WIDTH], ATTN_SCALE * LOG2E),
        "k": per_head_padded(l0_w_in[:, FOX_WIDTH:2 * FOX_WIDTH], 1.0),
        "kv": l0_w_in[:, FOX_WIDTH:3 * FOX_WIDTH].astype(BF16),
        "f":jnp.pad(l0_w_in[:, 3 * FOX_WIDTH:3 * FOX_WIDTH + FOX_HEADS],
                     ((0, 0), (0, V7X_LANES - FOX_HEADS))).astype(BF16),
        "bf": jnp.pad(l0_b_forget, (0, V7X_LANES - FOX_HEADS)).reshape(1, V7X_LANES),
        "glu": l0_w_in[:, 3 * FOX_WIDTH + FOX_HEADS:].astype(BF16),
    }
    w0_out_attn = l0_w_out[:FOX_WIDTH].astype(BF16)
    w0_out_conv = l0_w_out[FOX_WIDTH:].astype(BF16)
    w0_up, w0_down = l0_w_up.astype(BF16), l0_w_down.astype(BF16)
    w1_in, w1_out = l1_w_in.astype(BF16), l1_w_out.astype(BF16)
    w1_up, w1_down = l1_w_up.astype(BF16), l1_w_down.astype(BF16)
    g0_mix, g0_mlp = l0_norm_mix.reshape(1, d), l0_norm_mlp.reshape(1, d)
    g1_mix, g1_mlp = l1_norm_mix.reshape(1, d), l1_norm_mlp.reshape(1, d)
    gfin = final_norm.reshape(1, d)

    c_all = jnp.concatenate([c_prompt, c_sample], axis=0)
    mod0 = _ada(c_all, l0_w_ada, l0_b_ada)
    mod1 = _ada(c_all, l1_w_ada, l1_b_ada)
    mod0_p, mod0_s = mod0[:1], mod0[1:]
    mod1_p, mod1_s = mod1[:1], mod1[1:]

    xp = x_prompt.reshape(t, d)
    qa_p, ka_p, vt_p, k_p, v_p, lf_p, u_p = _proj0(xp, mod0_p, g0_mix, w0, True)
    yc_p = _conv0_prompt(u_p, l0_w_dw, l0_b_dw, l0_conv_ln_g, l0_conv_ln_b)
    at_p = _attn_prompt(qa_p, ka_p, vt_p)
    x1_p = _post(xp, [(at_p, w0_out_attn), (yc_p, w0_out_conv)], mod0_p, g0_mlp, w0_up, w0_down)
    by_p, tail_p = _mix1(x1_p, mod1_p, g1_mix, w1_in, l1_w_dw)
    y_p = _post(x1_p, [(by_p, w1_out)], mod1_p, g1_mlp, w1_up, w1_down, gfin)

    xs = x_sample.reshape(bs, d)
    qp_s, k_s, v_s, lf_s, u_s = _proj0(xs, mod0_s, g0_mix, w0, False)
    q_s = qp_s.reshape(bs, FOX_HEADS, AUG)[:, :, :HEAD_DIM].reshape(bs, FOX_WIDTH)
    yc_s = _conv0_sample(state_conformer_conv.transpose(1, 0, 2), u_s,
                         l0_w_dw, l0_b_dw, l0_conv_ln_g, l0_conv_ln_b)
    at_s = _attn_sample(q_s, k_s, v_s, lf_s, cache_k, cache_v, cache_logf, page_table)
    x1_s = _post(xs, [(at_s, w0_out_attn), (yc_s, w0_out_conv)], mod0_s, g0_mlp, w0_up, w0_down)
    by_s, cx_s = _mix1(x1_s, mod1_s, g1_mix, w1_in, l1_w_dw,
                       state=(state_short_conv[:, 0], state_short_conv[:, 1]))
    y_s = _post(x1_s, [(by_s, w1_out)], mod1_s, g1_mlp, w1_up, w1_down, gfin)

    hs = (FOX_HEADS, HEAD_DIM)
    return (y_p.reshape(1, t, d), y_s.reshape(bs, 1, d),
            k_p.reshape(1, t, *hs), v_p.reshape(1, t, *hs), lf_p.reshape(1, t, FOX_HEADS),
            u_p[t - (CONF_CONV_WIDTH - 1):].reshape(1, CONF_CONV_WIDTH - 1, CONF_CH),
            tail_p[-(SC_CONV_WIDTH - 1):].reshape(1, SC_CONV_WIDTH - 1, d),
            k_s.reshape(bs, 1, *hs), v_s.reshape(bs, 1, *hs), lf_s.reshape(bs, 1, FOX_HEADS),
            jnp.concatenate([state_conformer_conv[:, 1:], u_s[:, None, :]], axis=1),
            jnp.stack([state_short_conv[:, 1], cx_s], axis=1))
```

```python
import functools

import jax
import jax.numpy as jnp
from jax import lax
from jax.experimental import pallas as pl
from jax.experimental.pallas import tpu as pltpu

F32 = jnp.float32
BF16 = jnp.bfloat16

D_MODEL = 1024
FOX_HEADS = 8
HEAD_DIM = 64
FOX_WIDTH = FOX_HEADS * HEAD_DIM
CONF_CH = D_MODEL - FOX_WIDTH
CONF_CONV_WIDTH = 31
SC_CONV_WIDTH = 3
D_FF = 4 * D_MODEL
PAGE_SIZE = 128
EPS = 1e-6
ATTN_SCALE = HEAD_DIM ** -0.5
LOG2E = 1.4426950408889634

V7X_LANES = 128
V7X_SUBLANES = 8
V7X_VMEM_BYTES = 64 * 1024 * 1024
VMEM_LIMIT = 52 * 1024 * 1024

ROW_TILE = 512
CONV_TILE = 256
CONV_CHUNK = 64
CONV_HALO = 32
ATTN_TILE = 256
AUG = V7X_LANES
ATTN_HEADS = 4
ATTN_UNROLL = 2
PRUNE_LOG2 = 160.0
NORM_SLACK = 1.02
PAGES_PER_STEP = 16


def _const_spec(shape):
    nd = len(shape)
    return pl.BlockSpec(shape, lambda *_: (0,) * nd, pipeline_mode=pl.Buffered(1))


def _params(*sem):
    return pltpu.CompilerParams(dimension_semantics=sem, vmem_limit_bytes=VMEM_LIMIT)


def _modnorm(x, g, shift, scale):
    ms = jnp.mean(x * x, axis=-1, keepdims=True)
    y = x * lax.rsqrt(ms + EPS)
    return (y * g) * (1.0 + scale) + shift


def _split3(x):
    hi = x.astype(BF16).astype(F32)
    r = x - hi
    mid = r.astype(BF16).astype(F32)
    lo = (r - mid).astype(BF16).astype(F32)
    return hi, mid, lo


def _ada_kernel(c_ref, w_ref, b_ref, o_ref):
    c = c_ref[...]
    s = (c * jax.nn.sigmoid(c)).astype(BF16)
    o_ref[...] = jnp.dot(s, w_ref[...].astype(BF16), preferred_element_type=F32) + b_ref[...]


def _ada(c_all, w_ada, b_ada):
    m = c_all.shape[0]
    n = w_ada.shape[1]
    bn = 1536
    return pl.pallas_call(
        _ada_kernel,
        out_shape=jax.ShapeDtypeStruct((m, n), F32),
        grid=(n // bn,),
        in_specs=[pl.BlockSpec((m, D_MODEL), lambda j: (0, 0)),
                  pl.BlockSpec((D_MODEL, bn), lambda j: (0, j)),
                  pl.BlockSpec((1, bn), lambda j: (0, j))],
        out_specs=pl.BlockSpec((m, bn), lambda j: (0, j)),
        compiler_params=_params("arbitrary"),
        name="ada",
    )(c_all, w_ada, b_ada.reshape(1, n))


def _bias_placement():
    import numpy as np
    pq = np.zeros((V7X_LANES, FOX_HEADS * AUG), np.float32)
    pk = np.zeros((V7X_LANES, FOX_HEADS * AUG), np.float32)
    for h in range(FOX_HEADS):
        for p in range(3):
            pq[p * FOX_HEADS + h, h * AUG + HEAD_DIM + p] = 1.0
            pk[p * FOX_HEADS + h, h * AUG + HEAD_DIM + 3 + p] = -1.0
            pq[3 * FOX_HEADS, h * AUG + HEAD_DIM + 3 + p] = 1.0
            pk[3 * FOX_HEADS, h * AUG + HEAD_DIM + p] = 1.0
    return jnp.asarray(pq, BF16), jnp.asarray(pk, BF16)


def _proj0_kernel(*refs, prompt):
    x_ref, sh_ref, sc_ref, g_ref, wq_ref, wkv_ref, wf_ref, bf_ref, wglu_ref = refs[:9]
    x = x_ref[...]
    h = _modnorm(x, g_ref[...], sh_ref[...], sc_ref[...]).astype(BF16)
    kv = jnp.dot(h, wkv_ref[...], preferred_element_type=F32)
    k = kv[:, :FOX_WIDTH]
    v = kv[:, FOX_WIDTH:]
    fg = jnp.dot(h, wf_ref[...], preferred_element_type=F32) + bf_ref[...]
    lf = jnp.minimum(fg, 0.0) - jnp.log1p(jnp.exp(-jnp.abs(fg)))
    glu = jnp.dot(h, wglu_ref[...], preferred_element_type=F32)
    u = glu[:, :CONF_CH] * jax.nn.sigmoid(glu[:, CONF_CH:])
    qp = jnp.dot(h, wq_ref[...], preferred_element_type=F32)

    if not prompt:
        q_ref, k_ref, v_ref, lf_ref, u_ref = refs[9:]
        q_ref[...] = qp
    else:
        (wk_ref, pq_ref, pk_ref, qa_ref, ka_ref, vt_ref, st_ref, k_ref, v_ref, lf_ref, u_ref,
         tri_ref, carry_ref) = refs[9:]
        tm = x.shape[0]
        tk = vt_ref.shape[2]
        i = pl.program_id(0)

        @pl.when(i == 0)
        def _():
            carry_ref[...] = jnp.zeros_like(carry_ref)
            r = lax.broadcasted_iota(jnp.int32, (tm, tm), 0)
            c = lax.broadcasted_iota(jnp.int32, (tm, tm), 1)
            tri_ref[...] = (r <= c).astype(BF16)

        lft = lf.T[:FOX_HEADS]
        parts = jnp.concatenate(_split3(lft), axis=0).astype(BF16)
        cs = jnp.dot(parts, tri_ref[...], preferred_element_type=F32)
        ft = cs[0:8] + cs[8:16] + cs[16:24] + carry_ref[...]
        carry_ref[...] = ft[:, tm - 1:tm]

        pieces_t = jnp.concatenate(
            list(_split3(ft * LOG2E)) + [jnp.ones((FOX_HEADS, tm), F32),
                                 jnp.zeros((V7X_LANES - 4 * FOX_HEADS, tm), F32)], axis=0)
        pieces = pieces_t.T.astype(BF16)
        qpt = (qp + jnp.dot(pieces, pq_ref[...], preferred_element_type=F32)).T
        qa_ref[...] = qpt.astype(BF16)
        kp = jnp.dot(h, wk_ref[...], preferred_element_type=F32)
        ka_ref[...] = (kp + jnp.dot(pieces, pk_ref[...], preferred_element_type=F32)).astype(BF16)
        vtt = v.T.astype(BF16)
        for c0 in range(tm // tk):
            vt_ref[c0] = vtt[:, c0 * tk:(c0 + 1) * tk]

        nb = tm // tk
        row = lax.broadcasted_iota(jnp.int32, (FOX_HEADS, V7X_LANES), 0)
        lane = lax.broadcasted_iota(jnp.int32, (FOX_HEADS, V7X_LANES), 1)
        stat = jnp.zeros((FOX_HEADS, V7X_LANES), F32)
        for c0 in range(nb):
            blk = ft[:, c0 * tk:(c0 + 1) * tk]
            stat = jnp.where(lane == c0, jnp.max(blk, axis=1, keepdims=True), stat)
            stat = jnp.where(lane == nb + c0, jnp.min(blk, axis=1, keepdims=True), stat)
        for hh in range(FOX_HEADS):
            qh = qpt[hh * AUG:hh * AUG + HEAD_DIM, :]
            qn2 = jnp.max(jnp.sum(qh * qh, axis=0, keepdims=True), axis=1, keepdims=True)
            kh = kp[:, hh * AUG:(hh + 1) * AUG]
            kn2 = jnp.max(jnp.sum(kh * kh, axis=1, keepdims=True), axis=0, keepdims=True)
            stat = jnp.where((row == hh) & (lane == 2 * nb), qn2, stat)
            stat = jnp.where((row == hh) & (lane == 2 * nb + 1), kn2, stat)
        st_ref[0] = stat

    k_ref[...] = k
    v_ref[...] = v
    lf_ref[...] = lf[:, :FOX_HEADS]
    u_ref[...] = u


def _proj0(x, mod, g, w, prompt):
    t = x.shape[0]
    tm = min(ROW_TILE, t)
    mm = mod.shape[0]
    mrow = (lambda i: (0, 0)) if mm == 1 else (lambda i: (i, 0))
    mrow1 = (lambda i: (0, 1)) if mm == 1 else (lambda i: (i, 1))
    bm = 1 if mm == 1 else tm
    row = lambda i: (i, 0)
    wide = FOX_HEADS * AUG
    in_specs = [pl.BlockSpec((tm, D_MODEL), row),
                pl.BlockSpec((bm, D_MODEL), mrow),
                pl.BlockSpec((bm, D_MODEL), mrow1),
                _const_spec((1, D_MODEL)),
                _const_spec(w["q"].shape),
                _const_spec(w["kv"].shape),
                _const_spec(w["f"].shape),
                _const_spec((1, V7X_LANES)),
                _const_spec(w["glu"].shape)]
    args = [x, mod, mod, g, w["q"], w["kv"], w["f"], w["bf"], w["glu"]]
    tail_shape = [jax.ShapeDtypeStruct((t, FOX_WIDTH), F32), jax.ShapeDtypeStruct((t, FOX_WIDTH), F32),
                  jax.ShapeDtypeStruct((t, FOX_HEADS), F32), jax.ShapeDtypeStruct((t, CONF_CH), F32)]
    tail_specs = [pl.BlockSpec((tm, FOX_WIDTH), row), pl.BlockSpec((tm, FOX_WIDTH), row),
                  pl.BlockSpec((tm, FOX_HEADS), row), pl.BlockSpec((tm, CONF_CH), row)]
    if prompt:
        tk = ATTN_TILE
        pq, pk = _bias_placement()
        in_specs += [_const_spec(w["k"].shape), _const_spec(pq.shape), _const_spec(pk.shape)]
        args += [w["k"], pq, pk]
        out_shape = [jax.ShapeDtypeStruct((wide, t), BF16), jax.ShapeDtypeStruct((t, wide), BF16),
                     jax.ShapeDtypeStruct((t // tk, FOX_WIDTH, tk), BF16),
                     jax.ShapeDtypeStruct((t // tm, FOX_HEADS, V7X_LANES), F32)] + tail_shape
        out_specs = [pl.BlockSpec((wide, tm), lambda i: (0, i)), pl.BlockSpec((tm, wide), row),
                     pl.BlockSpec((tm // tk, FOX_WIDTH, tk), lambda i: (i, 0, 0)),
                     pl.BlockSpec((1, FOX_HEADS, V7X_LANES), lambda i: (i, 0, 0))] + tail_specs
        scratch = [pltpu.VMEM((tm, tm), BF16), pltpu.VMEM((FOX_HEADS, 1), F32)]
    else:
        out_shape = [jax.ShapeDtypeStruct((t, wide), F32)] + tail_shape
        out_specs = [pl.BlockSpec((tm, wide), row)] + tail_specs
        scratch = []
    return pl.pallas_call(
        functools.partial(_proj0_kernel, prompt=prompt),
        out_shape=out_shape,
        grid=(t // tm,),
        in_specs=in_specs,
        out_specs=out_specs,
        scratch_shapes=scratch,
        compiler_params=_params("arbitrary"),
        name="proj0",
    )(*args)


def _ln_silu(y, ln_g, ln_b):
    mu = jnp.mean(y, axis=-1, keepdims=True)
    d = y - mu
    var = jnp.mean(d * d, axis=-1, keepdims=True)
    z = d * lax.rsqrt(var + EPS) * ln_g + ln_b
    return z * jax.nn.sigmoid(z)


def _conv0_kernel(u_ref, up_ref, wdw_ref, bdw_ref, lng_ref, lnb_ref, y_ref, buf_ref):
    i = pl.program_id(0)
    tm = u_ref.shape[0]
    buf_ref[0, 0:CONV_HALO, :] = jnp.where(i == 0, 0.0, up_ref[...])
    buf_ref[0, CONV_HALO:, :] = u_ref[...]
    span = tm + CONV_HALO - V7X_SUBLANES
    for r in range(1, V7X_SUBLANES):
        buf_ref[r, 0:span, :] = buf_ref[0, pl.ds(r, span), :]
    first = CONV_HALO - (CONF_CONV_WIDTH - 1)
    for c0 in range(0, tm, CONV_CHUNK):
        acc = jnp.zeros((CONV_CHUNK, CONF_CH), F32)
        for kk in range(CONF_CONV_WIDTH):
            r = (first + kk) % V7X_SUBLANES
            acc = acc + wdw_ref[kk:kk + 1, :] * buf_ref[r, pl.ds(first + kk - r + c0, CONV_CHUNK), :]
        z = _ln_silu(acc + bdw_ref[...], lng_ref[...], lnb_ref[...])
        y_ref[c0:c0 + CONV_CHUNK, :] = z.astype(BF16)


def _conv0_prompt(u, w_dw, b_dw, ln_g, ln_b):
    t = u.shape[0]
    tm = CONV_TILE
    per = tm // CONV_HALO
    return pl.pallas_call(
        _conv0_kernel,
        out_shape=jax.ShapeDtypeStruct((t, CONF_CH), BF16),
        grid=(t // tm,),
        in_specs=[pl.BlockSpec((tm, CONF_CH), lambda i: (i, 0)),
                  pl.BlockSpec((CONV_HALO, CONF_CH), lambda i: (jnp.maximum(i * per - 1, 0), 0)),
                  _const_spec((CONF_CONV_WIDTH, CONF_CH)),
                  _const_spec((1, CONF_CH)), _const_spec((1, CONF_CH)), _const_spec((1, CONF_CH))],
        out_specs=pl.BlockSpec((tm, CONF_CH), lambda i: (i, 0)),
        scratch_shapes=[pltpu.VMEM((V7X_SUBLANES, tm + CONV_HALO, CONF_CH), F32)],
        compiler_params=_params("arbitrary"),
        name="conv0",
    )(u, u, w_dw, b_dw.reshape(1, -1), ln_g.reshape(1, -1), ln_b.reshape(1, -1))


def _conv0s_kernel(st_ref, u_ref, wdw_ref, bdw_ref, lng_ref, lnb_ref, y_ref):
    acc = wdw_ref[CONF_CONV_WIDTH - 1:CONF_CONV_WIDTH, :] * u_ref[...]
    for kk in range(CONF_CONV_WIDTH - 1):
        acc = acc + wdw_ref[kk:kk + 1, :] * st_ref[kk]
    z = _ln_silu(acc + bdw_ref[...], lng_ref[...], lnb_ref[...])
    y_ref[...] = z.astype(BF16)


def _conv0_sample(state_t, u, w_dw, b_dw, ln_g, ln_b):
    b = u.shape[0]
    return pl.pallas_call(
        _conv0s_kernel,
        out_shape=jax.ShapeDtypeStruct((b, CONF_CH), BF16),
        name="conv0s",
    )(state_t, u, w_dw, b_dw.reshape(1, -1), ln_g.reshape(1, -1), ln_b.reshape(1, -1))


def _attn_kernel(g0_ref, qa_ref, ka_ref, vt_ref, o_ref, s_ref, acc_ref):
    i = pl.program_id(1)
    g0 = g0_ref[pl.program_id(0), i]
    tq = qa_ref.shape[1]
    heads = range(ATTN_HEADS)
    qa = [qa_ref[hh * AUG:(hh + 1) * AUG, :] for hh in heads]

    def scores(j):
        start = pl.multiple_of(j * tq, tq)
        return [jnp.dot(ka_ref[pl.ds(start, tq), hh * AUG:(hh + 1) * AUG], qa[hh],
                        preferred_element_type=F32)
                for hh in heads]

    def fold(j, slot, carry, masked):
        out = []
        for hh in heads:
            m_prev, l_prev = carry[hh]
            st = s_ref[slot, hh]
            if masked:
                key = j * tq + lax.broadcasted_iota(jnp.int32, st.shape, 0)
                qry = i * tq + lax.broadcasted_iota(jnp.int32, st.shape, 1)
                st = jnp.where(key <= qry, st, -jnp.inf)
            m_new = jnp.maximum(m_prev, jnp.max(st, axis=0, keepdims=True))
            alpha = jnp.exp2(m_prev - m_new)
            pt = jnp.exp2(st - m_new)
            l_new = alpha * l_prev + jnp.sum(pt, axis=0, keepdims=True)
            vt = vt_ref[jnp.minimum(j, last), hh * HEAD_DIM:(hh + 1) * HEAD_DIM, :]
            acc_ref[hh] = acc_ref[hh] * alpha + jnp.dot(vt, pt.astype(BF16), preferred_element_type=F32)
            out.append((m_new, l_new))
        return tuple(out)

    def step(j, slot, carry, masked):
        nxt = scores(jnp.minimum(j + 1, last))
        carry = fold(j, slot, carry, masked)
        for hh in heads:
            s_ref[1 - slot, hh] = nxt[hh]
        return carry

    def group(gi, carry):
        for u in range(ATTN_UNROLL):
            carry = step(ATTN_UNROLL * gi + u, u % 2, carry, False)
        return carry

    last = vt_ref.shape[0] - 1
    first = scores(ATTN_UNROLL * g0)
    for hh in heads:
        s_ref[0, hh] = first[hh]
    acc_ref[...] = jnp.zeros_like(acc_ref)
    init = tuple((jnp.full((1, tq), -jnp.inf, F32), jnp.zeros((1, tq), F32)) for _ in heads)
    ngroup = i // ATTN_UNROLL
    carry = lax.fori_loop(g0, ngroup, group, init)
    for u in range(ATTN_UNROLL - 1):
        carry = step(ATTN_UNROLL * ngroup + u, u % 2, carry, True)
    carry = fold(ATTN_UNROLL * ngroup + ATTN_UNROLL - 1, (ATTN_UNROLL - 1) % 2, carry, True)
    ot = jnp.concatenate([acc_ref[hh] / carry[hh][1] for hh in heads], axis=0)
    o_ref[...] = ot.T.astype(BF16)


def _first_live_group(stat, nblk):
    per = stat.shape[0]
    nb = nblk // per
    fmax = stat[:, :, 0:nb].transpose(1, 0, 2).reshape(FOX_HEADS, nblk)
    fmin = stat[:, :, nb:2 * nb].transpose(1, 0, 2).reshape(FOX_HEADS, nblk)
    qk = jnp.sqrt(jnp.max(stat[:, :, 2 * nb], axis=0) * jnp.max(stat[:, :, 2 * nb + 1], axis=0))
    bound = 2.0 * NORM_SLACK * qk[:, None, None] + LOG2E * (fmax[:, :, None] - fmin[:, None, :])
    dead = (bound <= -PRUNE_LOG2).reshape(FOX_HEADS // ATTN_HEADS, ATTN_HEADS, nblk, nblk).all(axis=1)
    n_dead = jnp.sum(jnp.cumprod(dead.astype(jnp.int32), axis=2), axis=2)
    return (n_dead // ATTN_UNROLL).astype(jnp.int32)


def _attn_prompt(qa, ka, vt, stat):
    t = ka.shape[0]
    tq = ATTN_TILE
    nh = ATTN_HEADS
    nblk = t // tq
    once = pl.Buffered(1)
    grid_spec = pltpu.PrefetchScalarGridSpec(
        num_scalar_prefetch=1,
        grid=(FOX_HEADS // nh, nblk),
        in_specs=[pl.BlockSpec((nh * AUG, tq), lambda p, i, g0: (p, i)),
                  pl.BlockSpec((t, nh * AUG), lambda p, i, g0: (0, p), pipeline_mode=once),
                  pl.BlockSpec((nblk, nh * HEAD_DIM, tq), lambda p, i, g0: (0, p, 0), pipeline_mode=once)],
        out_specs=pl.BlockSpec((tq, nh * HEAD_DIM), lambda p, i, g0: (i, p)),
        scratch_shapes=[pltpu.VMEM((2, nh, tq, tq), F32),
                        pltpu.VMEM((nh, HEAD_DIM, tq), F32)])
    return pl.pallas_call(
        _attn_kernel,
        out_shape=jax.ShapeDtypeStruct((t, FOX_WIDTH), BF16),
        grid_spec=grid_spec,
        compiler_params=_params("arbitrary", "arbitrary"),
        name="attn_prompt",
    )(_first_live_group(stat, nblk), qa, ka, vt)


def _attn_s_kernel(pt_ref, q_ref, kn_ref, vn_ref, lfn_ref, *rest):
    del pt_ref
    npg = PAGES_PER_STEP
    k_refs = rest[:npg]
    v_refs = rest[npg:2 * npg]
    lf_refs = rest[2 * npg:3 * npg]
    o_ref, qb_ref, tri_ref, m_ref, l_ref, run_ref, acc_ref = rest[3 * npg:]
    g = pl.program_id(1)
    lanes = V7X_LANES
    heads = range(FOX_HEADS)
    row = lax.broadcasted_iota(jnp.int32, (FOX_HEADS, lanes), 0)
    lane = lax.broadcasted_iota(jnp.int32, (FOX_HEADS, lanes), 1)

    def rows_to_tile(vals):
        out = jnp.zeros((FOX_HEADS, lanes), F32)
        for h in heads:
            out = jnp.where(row == h, jnp.broadcast_to(vals[h], (FOX_HEADS, lanes)), out)
        return out

    @pl.when(g == 0)
    def _():
        tr = lax.broadcasted_iota(jnp.int32, (PAGE_SIZE, 2 * lanes), 0)
        tc = lax.broadcasted_iota(jnp.int32, (PAGE_SIZE, 2 * lanes), 1)
        tri_ref[...] = ((tr > tc) | (tc >= lanes)).astype(BF16)
        run_ref[...] = jnp.broadcast_to(lfn_ref[0], (FOX_HEADS, lanes))
        s_new = []
        for h in heads:
            qb_ref[h] = jnp.broadcast_to(q_ref[0, h], (HEAD_DIM, lanes))
            s_new.append(jnp.sum(q_ref[0, h] * kn_ref[0, h], axis=0, keepdims=True))
            acc_ref[h] = jnp.where(lane[:1] == 0, jnp.broadcast_to(vn_ref[0, h], (HEAD_DIM, lanes)), 0.0)
        m_ref[...] = rows_to_tile([jnp.broadcast_to(s, (1, lanes)) for s in s_new])
        l_ref[...] = jnp.ones_like(l_ref)

    parts = jnp.concatenate([p for i in range(npg) for p in _split3(lf_refs[i][0])], axis=0)
    cs = jnp.dot(parts.astype(BF16), tri_ref[...], preferred_element_type=F32)
    run = run_ref[...]
    scores = []
    for i in range(npg):
        c3 = cs[24 * i:24 * i + 8] + cs[24 * i + 8:24 * i + 16] + cs[24 * i + 16:24 * i + 24]
        qk = rows_to_tile([jnp.sum(k_refs[i][0, h] * qb_ref[h], axis=0, keepdims=True) for h in heads])
        scores.append(qk + (c3[:, :lanes] + run) * LOG2E)
        run = run + c3[:, lanes:]
    run_ref[...] = run

    m_prev = m_ref[...]
    m_blk = scores[0]
    for s in scores[1:]:
        m_blk = jnp.maximum(m_blk, s)
    m_new = jnp.maximum(m_prev, jnp.max(m_blk, axis=1, keepdims=True))
    alpha = jnp.exp2(m_prev - m_new)
    ps = [jnp.exp2(s - m_new) for s in scores]
    p_tot = ps[0]
    for p in ps[1:]:
        p_tot = p_tot + p
    l_ref[...] = alpha * l_ref[...] + jnp.sum(p_tot, axis=1, keepdims=True)
    m_ref[...] = m_new
    for h in heads:
        acc = acc_ref[h] * alpha[h:h + 1, :]
        for i in range(npg):
            acc = acc + v_refs[i][0, h] * ps[i][h:h + 1, :]
        acc_ref[h] = acc

    @pl.when(g == pl.num_programs(1) - 1)
    def _():
        for h in heads:
            o_ref[0, h] = jnp.sum(acc_ref[h], axis=1, keepdims=True) / l_ref[h:h + 1, 0:1]


def _attn_sample(q, k_new, v_new, lf_new, cache_k, cache_v, cache_logf, page_table):
    b = q.shape[0]
    n_pages = page_table.shape[1]
    npg = PAGES_PER_STEP
    nsteps = n_pages // npg
    col = (FOX_HEADS, HEAD_DIM, 1)
    ck = cache_k.transpose(0, 2, 3, 1)
    cv = cache_v.transpose(0, 2, 3, 1)
    clf = cache_logf.transpose(0, 2, 1)

    def page_map(nd):
        def for_slot(slot):
            return lambda bi, g, pt: (pt[bi, n_pages - 1 - (g * npg + slot)],) + (0,) * (nd - 1)
        return for_slot

    seq4 = lambda bi, g, pt: (bi, 0, 0, 0)
    page_blk = (1, FOX_HEADS, HEAD_DIM, PAGE_SIZE)
    in_specs = [pl.BlockSpec((1,) + col, seq4)] * 3 + [pl.BlockSpec((1, FOX_HEADS, 1), lambda bi, g, pt: (bi, 0, 0))]
    in_specs += [pl.BlockSpec(page_blk, page_map(4)(s)) for s in range(npg)]
    in_specs += [pl.BlockSpec(page_blk, page_map(4)(s)) for s in range(npg)]
    in_specs += [pl.BlockSpec((1, FOX_HEADS, PAGE_SIZE), page_map(3)(s)) for s in range(npg)]
    grid_spec = pltpu.PrefetchScalarGridSpec(
        num_scalar_prefetch=1,
        grid=(b, nsteps),
        in_specs=in_specs,
        out_specs=pl.BlockSpec((1,) + col, seq4),
        scratch_shapes=[pltpu.VMEM((FOX_HEADS, HEAD_DIM, V7X_LANES), F32),
                        pltpu.VMEM((PAGE_SIZE, 2 * V7X_LANES), BF16),
                        pltpu.VMEM((FOX_HEADS, V7X_LANES), F32),
                        pltpu.VMEM((FOX_HEADS, V7X_LANES), F32),
                        pltpu.VMEM((FOX_HEADS, V7X_LANES), F32),
                        pltpu.VMEM((FOX_HEADS, HEAD_DIM, V7X_LANES), F32)])
    out = pl.pallas_call(
        _attn_s_kernel,
        out_shape=jax.ShapeDtypeStruct((b,) + col, F32),
        grid_spec=grid_spec,
        compiler_params=_params("arbitrary", "arbitrary"),
        name="attn_sample",
    )(page_table, q.reshape((b,) + col), k_new.reshape((b,) + col), v_new.reshape((b,) + col),
      lf_new.reshape(b, FOX_HEADS, 1), *([ck] * npg), *([cv] * npg), *([clf] * npg))
    return out.reshape(b, FOX_WIDTH).astype(BF16)


def _post_kernel(*refs, n_mix, final):
    x_ref = refs[0]
    mix_refs = refs[1:1 + 2 * n_mix]
    gm_ref, shf_ref, scf_ref, gf_ref, g_ref, wup_ref, wdn_ref = refs[1 + 2 * n_mix:8 + 2 * n_mix]
    rest = refs[8 + 2 * n_mix:]
    if final:
        gfin_ref, o_ref = rest
    else:
        (o_ref,) = rest
    mix = None
    for a in range(n_mix):
        t = jnp.dot(mix_refs[2 * a][...], mix_refs[2 * a + 1][...], preferred_element_type=F32)
        mix = t if mix is None else mix + t
    x1 = x_ref[...] + gm_ref[...] * mix
    h = _modnorm(x1, g_ref[...], shf_ref[...], scf_ref[...]).astype(BF16)
    fchunk = 1024
    acc = None
    for f0 in range(0, D_FF, fchunk):
        a = jnp.maximum(jnp.dot(h, wup_ref[:, f0:f0 + fchunk], preferred_element_type=F32), 0.0)
        t = jnp.dot((a * a).astype(BF16), wdn_ref[f0:f0 + fchunk, :], preferred_element_type=F32)
        acc = t if acc is None else acc + t
    x2 = x1 + gf_ref[...] * acc
    if final:
        ms = jnp.mean(x2 * x2, axis=-1, keepdims=True)
        x2 = x2 * lax.rsqrt(ms + EPS) * gfin_ref[...]
    o_ref[...] = x2


def _post(x, mixes, mod, g_mlp, w_up, w_down, final_g=None):
    t = x.shape[0]
    tm = min(ROW_TILE, t)
    mm = mod.shape[0]
    bm = 1 if mm == 1 else tm
    row = lambda i: (i, 0)

    def mcol(c):
        return (lambda i: (0, c)) if mm == 1 else (lambda i: (i, c))

    in_specs = [pl.BlockSpec((tm, D_MODEL), row)]
    args = [x]
    for a, w in mixes:
        in_specs += [pl.BlockSpec((tm, a.shape[1]), row), _const_spec(w.shape)]
        args += [a, w]
    in_specs += [pl.BlockSpec((bm, D_MODEL), mcol(2)), pl.BlockSpec((bm, D_MODEL), mcol(3)),
                 pl.BlockSpec((bm, D_MODEL), mcol(4)), pl.BlockSpec((bm, D_MODEL), mcol(5)),
                 _const_spec((1, D_MODEL)), _const_spec(w_up.shape), _const_spec(w_down.shape)]
    args += [mod, mod, mod, mod, g_mlp, w_up, w_down]
    if final_g is not None:
        in_specs.append(_const_spec((1, D_MODEL)))
        args.append(final_g)
    return pl.pallas_call(
        functools.partial(_post_kernel, n_mix=len(mixes), final=final_g is not None),
        out_shape=jax.ShapeDtypeStruct((t, D_MODEL), F32),
        grid=(t // tm,),
        in_specs=in_specs,
        out_specs=pl.BlockSpec((tm, D_MODEL), row),
        compiler_params=_params("arbitrary"),
        name="post",
    )(*args)


def _mix1_kernel(x_ref, sh_ref, sc_ref, g_ref, win_ref, wdw_ref, *rest, per_row_state):
    x = x_ref[...]
    tm = x.shape[0]
    h = _modnorm(x, g_ref[...], sh_ref[...], sc_ref[...]).astype(BF16)
    proj = jnp.dot(h, win_ref[...], preferred_element_type=F32)
    b_gate = proj[:, :D_MODEL]
    cx = proj[:, D_MODEL:2 * D_MODEL] * proj[:, 2 * D_MODEL:]
    w0, w1, w2 = wdw_ref[0:1, :], wdw_ref[1:2, :], wdw_ref[2:3, :]
    if per_row_state:
        s0_ref, s1_ref, by_ref, cx_ref = rest
        y = w0 * s0_ref[...] + w1 * s1_ref[...] + w2 * cx
        cx_ref[...] = cx
    else:
        by_ref, tail_ref, buf_ref = rest
        i = pl.program_id(0)
        pad = V7X_SUBLANES

        @pl.when(i == 0)
        def _():
            buf_ref[0:pad, :] = jnp.zeros((pad, D_MODEL), F32)

        buf_ref[pad:, :] = cx
        y = (w0 * buf_ref[pl.ds(pad - 2, tm), :] + w1 * buf_ref[pl.ds(pad - 1, tm), :] + w2 * cx)
        tail = cx[tm - pad:, :]
        buf_ref[0:pad, :] = tail
        tail_ref[...] = tail
    by_ref[...] = (b_gate * y).astype(BF16)


def _mix1(x, mod, g, w_in, w_dw, state=None):
    t = x.shape[0]
    tm = min(ROW_TILE, t)
    mm = mod.shape[0]
    bm = 1 if mm == 1 else tm
    row = lambda i: (i, 0)
    mcol = lambda c: (lambda i: (0, c)) if mm == 1 else (lambda i: (i, c))
    in_specs = [pl.BlockSpec((tm, D_MODEL), row),
                pl.BlockSpec((bm, D_MODEL), mcol(0)), pl.BlockSpec((bm, D_MODEL), mcol(1)),
                _const_spec((1, D_MODEL)), _const_spec(w_in.shape), _const_spec(w_dw.shape)]
    args = [x, mod, mod, g, w_in, w_dw]
    if state is not None:
        in_specs += [pl.BlockSpec((tm, D_MODEL), row)] * 2
        args += [state[0], state[1]]
        out_shape = [jax.ShapeDtypeStruct((t, D_MODEL), BF16), jax.ShapeDtypeStruct((t, D_MODEL), F32)]
        out_specs = [pl.BlockSpec((tm, D_MODEL), row)] * 2
        scratch = []
    else:
        nt = t // tm
        out_shape = [jax.ShapeDtypeStruct((t, D_MODEL), BF16),
                     jax.ShapeDtypeStruct((nt * V7X_SUBLANES, D_MODEL), F32)]
        out_specs = [pl.BlockSpec((tm, D_MODEL), row), pl.BlockSpec((V7X_SUBLANES, D_MODEL), row)]
        scratch = [pltpu.VMEM((tm + V7X_SUBLANES, D_MODEL), F32)]
    return pl.pallas_call(
        functools.partial(_mix1_kernel, per_row_state=state is not None),
        out_shape=out_shape,
        grid=(t // tm,),
        in_specs=in_specs,
        out_specs=out_specs,
        scratch_shapes=scratch,
        compiler_params=_params("arbitrary"),
        name="mix1",
    )(*args)


def kernel(x_prompt, x_sample, cache_k, cache_v, cache_logf, state_conformer_conv, state_short_conv, page_table, c_prompt, c_sample, l0_w_ada, l0_b_ada, l0_norm_mix, l0_norm_mlp, l0_w_in, l0_b_forget, l0_w_dw, l0_b_dw, l0_conv_ln_g, l0_conv_ln_b, l0_w_out, l0_w_up, l0_w_down, l1_w_ada, l1_b_ada, l1_norm_mix, l1_norm_mlp, l1_w_in, l1_w_dw, l1_w_out, l1_w_up, l1_w_down, final_norm):
    bp, t, d = x_prompt.shape
    bs = x_sample.shape[0]
    assert bp == 1 and d == D_MODEL and x_sample.shape[1] == 1

    def per_head_padded(wcols, scale):
        w3 = (wcols * scale).reshape(d, FOX_HEADS, HEAD_DIM)
        return jnp.pad(w3, ((0, 0), (0, 0), (0, AUG - HEAD_DIM))).reshape(d, FOX_HEADS * AUG).astype(BF16)

    w0 = {
        "q": per_head_padded(l0_w_in[:, :FOX_WIDTH], ATTN_SCALE * LOG2E),
        "k": per_head_padded(l0_w_in[:, FOX_WIDTH:2 * FOX_WIDTH], 1.0),
        "kv": l0_w_in[:, FOX_WIDTH:3 * FOX_WIDTH].astype(BF16),
        "f":jnp.pad(l0_w_in[:, 3 * FOX_WIDTH:3 * FOX_WIDTH + FOX_HEADS],
                     ((0, 0), (0, V7X_LANES - FOX_HEADS))).astype(BF16),
        "bf": jnp.pad(l0_b_forget, (0, V7X_LANES - FOX_HEADS)).reshape(1, V7X_LANES),
        "glu": l0_w_in[:, 3 * FOX_WIDTH + FOX_HEADS:].astype(BF16),
    }
    w0_out_attn = l0_w_out[:FOX_WIDTH].astype(BF16)
    w0_out_conv = l0_w_out[FOX_WIDTH:].astype(BF16)
    w0_up, w0_down = l0_w_up.astype(BF16), l0_w_down.astype(BF16)
    w1_in, w1_out = l1_w_in.astype(BF16), l1_w_out.astype(BF16)
    w1_up, w1_down = l1_w_up.astype(BF16), l1_w_down.astype(BF16)
    g0_mix, g0_mlp = l0_norm_mix.reshape(1, d), l0_norm_mlp.reshape(1, d)
    g1_mix, g1_mlp = l1_norm_mix.reshape(1, d), l1_norm_mlp.reshape(1, d)
    gfin = final_norm.reshape(1, d)

    c_all = jnp.concatenate([c_prompt, c_sample], axis=0)
    mod0 = _ada(c_all, l0_w_ada, l0_b_ada)
    mod1 = _ada(c_all, l1_w_ada, l1_b_ada)
    mod0_p, mod0_s = mod0[:1], mod0[1:]
    mod1_p, mod1_s = mod1[:1], mod1[1:]

    xp = x_prompt.reshape(t, d)
    qa_p, ka_p, vt_p, st_p, k_p, v_p, lf_p, u_p = _proj0(xp, mod0_p, g0_mix, w0, True)
    yc_p = _conv0_prompt(u_p, l0_w_dw, l0_b_dw, l0_conv_ln_g, l0_conv_ln_b)
    at_p = _attn_prompt(qa_p, ka_p, vt_p, st_p)
    x1_p = _post(xp, [(at_p, w0_out_attn), (yc_p, w0_out_conv)], mod0_p, g0_mlp, w0_up, w0_down)
    by_p, tail_p = _mix1(x1_p, mod1_p, g1_mix, w1_in, l1_w_dw)
    y_p = _post(x1_p, [(by_p, w1_out)], mod1_p, g1_mlp, w1_up, w1_down, gfin)

    xs = x_sample.reshape(bs, d)
    qp_s, k_s, v_s, lf_s, u_s = _proj0(xs, mod0_s, g0_mix, w0, False)
    q_s = qp_s.reshape(bs, FOX_HEADS, AUG)[:, :, :HEAD_DIM].reshape(bs, FOX_WIDTH)
    yc_s = _conv0_sample(state_conformer_conv.transpose(1, 0, 2), u_s,
                         l0_w_dw, l0_b_dw, l0_conv_ln_g, l0_conv_ln_b)
    at_s = _attn_sample(q_s, k_s, v_s, lf_s, cache_k, cache_v, cache_logf, page_table)
    x1_s = _post(xs, [(at_s, w0_out_attn), (yc_s, w0_out_conv)], mod0_s, g0_mlp, w0_up, w0_down)
    by_s, cx_s = _mix1(x1_s, mod1_s, g1_mix, w1_in, l1_w_dw,
                       state=(state_short_conv[:, 0], state_short_conv[:, 1]))
    y_s = _post(x1_s, [(by_s, w1_out)], mod1_s, g1_mlp, w1_up, w1_down, gfin)

    hs = (FOX_HEADS, HEAD_DIM)
    return (y_p.reshape(1, t, d), y_s.reshape(bs, 1, d),
            k_p.reshape(1, t, *hs), v_p.reshape(1, t, *hs), lf_p.reshape(1, t, FOX_HEADS),
            u_p[t - (CONF_CONV_WIDTH - 1):].reshape(1, CONF_CONV_WIDTH - 1, CONF_CH),
            tail_p[-(SC_CONV_WIDTH - 1):].reshape(1, SC_CONV_WIDTH - 1, d),
            k_s.reshape(bs, 1, *hs), v_s.reshape(bs, 1, *hs), lf_s.reshape(bs, 1, FOX_HEADS),
            jnp.concatenate([state_conformer_conv[:, 1:], u_s[:, None, :]], axis=1),
            jnp.stack([state_short_conv[:, 1], cx_s], axis=1))
```

```python
import functools

import jax
import jax.numpy as jnp
from jax import lax
from jax.experimental import pallas as pl
from jax.experimental.pallas import tpu as pltpu

F32 = jnp.float32
BF16 = jnp.bfloat16

D_MODEL = 1024
FOX_HEADS = 8
HEAD_DIM = 64
FOX_WIDTH = FOX_HEADS * HEAD_DIM
CONF_CH = D_MODEL - FOX_WIDTH
CONF_CONV_WIDTH = 31
SC_CONV_WIDTH = 3
D_FF = 4 * D_MODEL
PAGE_SIZE = 128
EPS = 1e-6
ATTN_SCALE = HEAD_DIM ** -0.5
LOG2E = 1.4426950408889634

V7X_LANES = 128
V7X_SUBLANES = 8
V7X_VMEM_BYTES = 64 * 1024 * 1024
VMEM_LIMIT = 52 * 1024 * 1024
FUSED_VMEM_LIMIT = 60 * 1024 * 1024

ROW_TILE = 512
CONV_TILE = 256
CONV_CHUNK = 64
CONV_HALO = 32
ATTN_TILE = 256
AUG = V7X_LANES
ATTN_HEADS = 4
ATTN_UNROLL = 2
PRUNE_LOG2 = 160.0
NORM_SLACK = 1.02
PAGES_PER_STEP = 16


def _const_spec(shape):
    nd = len(shape)
    return pl.BlockSpec(shape, lambda *_: (0,) * nd, pipeline_mode=pl.Buffered(1))


def _params(*sem):
    return pltpu.CompilerParams(dimension_semantics=sem, vmem_limit_bytes=VMEM_LIMIT)


def _modnorm(x, g, shift, scale):
    ms = jnp.mean(x * x, axis=-1, keepdims=True)
    y = x * lax.rsqrt(ms + EPS)
    return (y * g) * (1.0 + scale) + shift


def _split3(x):
    hi = x.astype(BF16).astype(F32)
    r = x - hi
    mid = r.astype(BF16).astype(F32)
    lo = (r - mid).astype(BF16).astype(F32)
    return hi, mid, lo


def _ada_kernel(c_ref, w_ref, b_ref, o_ref):
    c = c_ref[...]
    s = (c * jax.nn.sigmoid(c)).astype(BF16)
    o_ref[...] = jnp.dot(s, w_ref[...].astype(BF16), preferred_element_type=F32) + b_ref[...]


def _ada(c_all, w_ada, b_ada):
    m = c_all.shape[0]
    n = w_ada.shape[1]
    bn = 1536
    return pl.pallas_call(
        _ada_kernel,
        out_shape=jax.ShapeDtypeStruct((m, n), F32),
        grid=(n // bn,),
        in_specs=[pl.BlockSpec((m, D_MODEL), lambda j: (0, 0)),
                  pl.BlockSpec((D_MODEL, bn), lambda j: (0, j)),
                  pl.BlockSpec((1, bn), lambda j: (0, j))],
        out_specs=pl.BlockSpec((m, bn), lambda j: (0, j)),
        compiler_params=_params("arbitrary"),
        name="ada",
    )(c_all, w_ada, b_ada.reshape(1, n))


def _bias_placement():
    import numpy as np
    pq = np.zeros((V7X_LANES, FOX_HEADS * AUG), np.float32)
    pk = np.zeros((V7X_LANES, FOX_HEADS * AUG), np.float32)
    for h in range(FOX_HEADS):
        for p in range(3):
            pq[p * FOX_HEADS + h, h * AUG + HEAD_DIM + p] = 1.0
            pk[p * FOX_HEADS + h, h * AUG + HEAD_DIM + 3 + p] = -1.0
            pq[3 * FOX_HEADS, h * AUG + HEAD_DIM + 3 + p] = 1.0
            pk[3 * FOX_HEADS, h * AUG + HEAD_DIM + p] = 1.0
    return jnp.asarray(pq, BF16), jnp.asarray(pk, BF16)


def _proj0_kernel(*refs, prompt):
    x_ref, sh_ref, sc_ref, g_ref, wq_ref, wkv_ref, wf_ref, bf_ref, wglu_ref = refs[:9]
    x = x_ref[...]
    h = _modnorm(x, g_ref[...], sh_ref[...], sc_ref[...]).astype(BF16)
    kv = jnp.dot(h, wkv_ref[...], preferred_element_type=F32)
    k = kv[:, :FOX_WIDTH]
    v = kv[:, FOX_WIDTH:]
    fg = jnp.dot(h, wf_ref[...], preferred_element_type=F32) + bf_ref[...]
    lf = jnp.minimum(fg, 0.0) - jnp.log1p(jnp.exp(-jnp.abs(fg)))
    glu = jnp.dot(h, wglu_ref[...], preferred_element_type=F32)
    u = glu[:, :CONF_CH] * jax.nn.sigmoid(glu[:, CONF_CH:])
    qp = jnp.dot(h, wq_ref[...], preferred_element_type=F32)

    if not prompt:
        q_ref, k_ref, v_ref, lf_ref, u_ref = refs[9:]
        q_ref[...] = qp
    else:
        (wk_ref, pq_ref, pk_ref, qa_ref, ka_ref, vt_ref, st_ref, k_ref, v_ref, lf_ref, u_ref,
         tri_ref, carry_ref) = refs[9:]
        tm = x.shape[0]
        tk = vt_ref.shape[2]
        i = pl.program_id(0)

        @pl.when(i == 0)
        def _():
            carry_ref[...] = jnp.zeros_like(carry_ref)
            r = lax.broadcasted_iota(jnp.int32, (tm, tm), 0)
            c = lax.broadcasted_iota(jnp.int32, (tm, tm), 1)
            tri_ref[...] = (r <= c).astype(BF16)

        lft = lf.T[:FOX_HEADS]
        parts = jnp.concatenate(_split3(lft), axis=0).astype(BF16)
        cs = jnp.dot(parts, tri_ref[...], preferred_element_type=F32)
        ft = cs[0:8] + cs[8:16] + cs[16:24] + carry_ref[...]
        carry_ref[...] = ft[:, tm - 1:tm]

        pieces_t = jnp.concatenate(
            list(_split3(ft * LOG2E)) + [jnp.ones((FOX_HEADS, tm), F32),
                                 jnp.zeros((V7X_LANES - 4 * FOX_HEADS, tm), F32)], axis=0)
        pieces = pieces_t.T.astype(BF16)
        qpt = (qp + jnp.dot(pieces, pq_ref[...], preferred_element_type=F32)).T
        qa_ref[...] = qpt.astype(BF16)
        kp = jnp.dot(h, wk_ref[...], preferred_element_type=F32)
        ka_ref[...] = (kp + jnp.dot(pieces, pk_ref[...], preferred_element_type=F32)).astype(BF16)
        vtt = v.T.astype(BF16)
        for c0 in range(tm // tk):
            vt_ref[c0] = vtt[:, c0 * tk:(c0 + 1) * tk]

        nb = tm // tk
        row = lax.broadcasted_iota(jnp.int32, (FOX_HEADS, V7X_LANES), 0)
        lane = lax.broadcasted_iota(jnp.int32, (FOX_HEADS, V7X_LANES), 1)
        stat = jnp.zeros((FOX_HEADS, V7X_LANES), F32)
        for c0 in range(nb):
            blk = ft[:, c0 * tk:(c0 + 1) * tk]
            stat = jnp.where(lane == c0, jnp.max(blk, axis=1, keepdims=True), stat)
            stat = jnp.where(lane == nb + c0, jnp.min(blk, axis=1, keepdims=True), stat)
        for hh in range(FOX_HEADS):
            qh = qpt[hh * AUG:hh * AUG + HEAD_DIM, :]
            qn2 = jnp.max(jnp.sum(qh * qh, axis=0, keepdims=True), axis=1, keepdims=True)
            kh = kp[:, hh * AUG:(hh + 1) * AUG]
            kn2 = jnp.max(jnp.sum(kh * kh, axis=1, keepdims=True), axis=0, keepdims=True)
            stat = jnp.where((row == hh) & (lane == 2 * nb), qn2, stat)
            stat = jnp.where((row == hh) & (lane == 2 * nb + 1), kn2, stat)
        st_ref[0] = stat

    k_ref[...] = k
    v_ref[...] = v
    lf_ref[...] = lf[:, :FOX_HEADS]
    u_ref[...] = u


def _proj0(x, mod, g, w, prompt):
    t = x.shape[0]
    tm = min(ROW_TILE, t)
    mm = mod.shape[0]
    mrow = (lambda i: (0, 0)) if mm == 1 else (lambda i: (i, 0))
    mrow1 = (lambda i: (0, 1)) if mm == 1 else (lambda i: (i, 1))
    bm = 1 if mm == 1 else tm
    row = lambda i: (i, 0)
    wide = FOX_HEADS * AUG
    in_specs = [pl.BlockSpec((tm, D_MODEL), row),
                pl.BlockSpec((bm, D_MODEL), mrow),
                pl.BlockSpec((bm, D_MODEL), mrow1),
                _const_spec((1, D_MODEL)),
                _const_spec(w["q"].shape),
                _const_spec(w["kv"].shape),
                _const_spec(w["f"].shape),
                _const_spec((1, V7X_LANES)),
                _const_spec(w["glu"].shape)]
    args = [x, mod, mod, g, w["q"], w["kv"], w["f"], w["bf"], w["glu"]]
    tail_shape = [jax.ShapeDtypeStruct((t, FOX_WIDTH), F32), jax.ShapeDtypeStruct((t, FOX_WIDTH), F32),
                  jax.ShapeDtypeStruct((t, FOX_HEADS), F32), jax.ShapeDtypeStruct((t, CONF_CH), F32)]
    tail_specs = [pl.BlockSpec((tm, FOX_WIDTH), row), pl.BlockSpec((tm, FOX_WIDTH), row),
                  pl.BlockSpec((tm, FOX_HEADS), row), pl.BlockSpec((tm, CONF_CH), row)]
    if prompt:
        tk = ATTN_TILE
        pq, pk = _bias_placement()
        in_specs += [_const_spec(w["k"].shape), _const_spec(pq.shape), _const_spec(pk.shape)]
        args += [w["k"], pq, pk]
        out_shape = [jax.ShapeDtypeStruct((wide, t), BF16), jax.ShapeDtypeStruct((t, wide), BF16),
                     jax.ShapeDtypeStruct((t // tk, FOX_WIDTH, tk), BF16),
                     jax.ShapeDtypeStruct((t // tm, FOX_HEADS, V7X_LANES), F32)] + tail_shape
        out_specs = [pl.BlockSpec((wide, tm), lambda i: (0, i)), pl.BlockSpec((tm, wide), row),
                     pl.BlockSpec((tm // tk, FOX_WIDTH, tk), lambda i: (i, 0, 0)),
                     pl.BlockSpec((1, FOX_HEADS, V7X_LANES), lambda i: (i, 0, 0))] + tail_specs
        scratch = [pltpu.VMEM((tm, tm), BF16), pltpu.VMEM((FOX_HEADS, 1), F32)]
    else:
        out_shape = [jax.ShapeDtypeStruct((t, wide), F32)] + tail_shape
        out_specs = [pl.BlockSpec((tm, wide), row)] + tail_specs
        scratch = []
    return pl.pallas_call(
        functools.partial(_proj0_kernel, prompt=prompt),
        out_shape=out_shape,
        grid=(t // tm,),
        in_specs=in_specs,
        out_specs=out_specs,
        scratch_shapes=scratch,
        compiler_params=_params("arbitrary"),
        name="proj0",
    )(*args)


def _ln_silu(y, ln_g, ln_b):
    mu = jnp.mean(y, axis=-1, keepdims=True)
    d = y - mu
    var = jnp.mean(d * d, axis=-1, keepdims=True)
    z = d * lax.rsqrt(var + EPS) * ln_g + ln_b
    return z * jax.nn.sigmoid(z)


def _conv0_kernel(u_ref, up_ref, wdw_ref, bdw_ref, lng_ref, lnb_ref, y_ref, buf_ref):
    i = pl.program_id(0)
    tm = u_ref.shape[0]
    buf_ref[0, 0:CONV_HALO, :] = jnp.where(i == 0, 0.0, up_ref[...])
    buf_ref[0, CONV_HALO:, :] = u_ref[...]
    span = tm + CONV_HALO - V7X_SUBLANES
    for r in range(1, V7X_SUBLANES):
        buf_ref[r, 0:span, :] = buf_ref[0, pl.ds(r, span), :]
    first = CONV_HALO - (CONF_CONV_WIDTH - 1)
    for c0 in range(0, tm, CONV_CHUNK):
        acc = jnp.zeros((CONV_CHUNK, CONF_CH), F32)
        for kk in range(CONF_CONV_WIDTH):
            r = (first + kk) % V7X_SUBLANES
            acc = acc + wdw_ref[kk:kk + 1, :] * buf_ref[r, pl.ds(first + kk - r + c0, CONV_CHUNK), :]
        z = _ln_silu(acc + bdw_ref[...], lng_ref[...], lnb_ref[...])
        y_ref[c0:c0 + CONV_CHUNK, :] = z.astype(BF16)


def _conv0_prompt(u, w_dw, b_dw, ln_g, ln_b):
    t = u.shape[0]
    tm = CONV_TILE
    per = tm // CONV_HALO
    return pl.pallas_call(
        _conv0_kernel,
        out_shape=jax.ShapeDtypeStruct((t, CONF_CH), BF16),
        grid=(t // tm,),
        in_specs=[pl.BlockSpec((tm, CONF_CH), lambda i: (i, 0)),
                  pl.BlockSpec((CONV_HALO, CONF_CH), lambda i: (jnp.maximum(i * per - 1, 0), 0)),
                  _const_spec((CONF_CONV_WIDTH, CONF_CH)),
                  _const_spec((1, CONF_CH)), _const_spec((1, CONF_CH)), _const_spec((1, CONF_CH))],
        out_specs=pl.BlockSpec((tm, CONF_CH), lambda i: (i, 0)),
        scratch_shapes=[pltpu.VMEM((V7X_SUBLANES, tm + CONV_HALO, CONF_CH), F32)],
        compiler_params=_params("arbitrary"),
        name="conv0",
    )(u, u, w_dw, b_dw.reshape(1, -1), ln_g.reshape(1, -1), ln_b.reshape(1, -1))


def _conv0s_kernel(st_ref, u_ref, wdw_ref, bdw_ref, lng_ref, lnb_ref, y_ref):
    acc = wdw_ref[CONF_CONV_WIDTH - 1:CONF_CONV_WIDTH, :] * u_ref[...]
    for kk in range(CONF_CONV_WIDTH - 1):
        acc = acc + wdw_ref[kk:kk + 1, :] * st_ref[kk]
    z = _ln_silu(acc + bdw_ref[...], lng_ref[...], lnb_ref[...])
    y_ref[...] = z.astype(BF16)


def _conv0_sample(state_t, u, w_dw, b_dw, ln_g, ln_b):
    b = u.shape[0]
    return pl.pallas_call(
        _conv0s_kernel,
        out_shape=jax.ShapeDtypeStruct((b, CONF_CH), BF16),
        name="conv0s",
    )(state_t, u, w_dw, b_dw.reshape(1, -1), ln_g.reshape(1, -1), ln_b.reshape(1, -1))


def _attn_kernel(g0_ref, qa_ref, ka_ref, vt_ref, o_ref, s_ref, acc_ref):
    i = pl.program_id(1)
    g0 = g0_ref[pl.program_id(0), i]
    tq = qa_ref.shape[1]
    heads = range(ATTN_HEADS)
    qa = [qa_ref[hh * AUG:(hh + 1) * AUG, :] for hh in heads]

    def scores(j):
        start = pl.multiple_of(j * tq, tq)
        return [jnp.dot(ka_ref[pl.ds(start, tq), hh * AUG:(hh + 1) * AUG], qa[hh],
                        preferred_element_type=F32)
                for hh in heads]

    def fold(j, slot, carry, masked):
        out = []
        for hh in heads:
            m_prev, l_prev = carry[hh]
            st = s_ref[slot, hh]
            if masked:
                key = j * tq + lax.broadcasted_iota(jnp.int32, st.shape, 0)
                qry = i * tq + lax.broadcasted_iota(jnp.int32, st.shape, 1)
                st = jnp.where(key <= qry, st, -jnp.inf)
            m_new = jnp.maximum(m_prev, jnp.max(st, axis=0, keepdims=True))
            alpha = jnp.exp2(m_prev - m_new)
            pt = jnp.exp2(st - m_new)
            l_new = alpha * l_prev + jnp.sum(pt, axis=0, keepdims=True)
            vt = vt_ref[jnp.minimum(j, last), hh * HEAD_DIM:(hh + 1) * HEAD_DIM, :]
            acc_ref[hh] = acc_ref[hh] * alpha + jnp.dot(vt, pt.astype(BF16), preferred_element_type=F32)
            out.append((m_new, l_new))
        return tuple(out)

    def step(j, slot, carry, masked):
        nxt = scores(jnp.minimum(j + 1, last))
        carry = fold(j, slot, carry, masked)
        for hh in heads:
            s_ref[1 - slot, hh] = nxt[hh]
        return carry

    def group(gi, carry):
        for u in range(ATTN_UNROLL):
            carry = step(ATTN_UNROLL * gi + u, u % 2, carry, False)
        return carry

    last = vt_ref.shape[0] - 1
    first = scores(ATTN_UNROLL * g0)
    for hh in heads:
        s_ref[0, hh] = first[hh]
    acc_ref[...] = jnp.zeros_like(acc_ref)
    init = tuple((jnp.full((1, tq), -jnp.inf, F32), jnp.zeros((1, tq), F32)) for _ in heads)
    ngroup = i // ATTN_UNROLL
    carry = lax.fori_loop(g0, ngroup, group, init)
    for u in range(ATTN_UNROLL - 1):
        carry = step(ATTN_UNROLL * ngroup + u, u % 2, carry, True)
    carry = fold(ATTN_UNROLL * ngroup + ATTN_UNROLL - 1, (ATTN_UNROLL - 1) % 2, carry, True)
    ot = jnp.concatenate([acc_ref[hh] / carry[hh][1] for hh in heads], axis=0)
    o_ref[...] = ot.T.astype(BF16)


def _first_live_group(stat, nblk):
    per = stat.shape[0]
    nb = nblk // per
    fmax = stat[:, :, 0:nb].transpose(1, 0, 2).reshape(FOX_HEADS, nblk)
    fmin = stat[:, :, nb:2 * nb].transpose(1, 0, 2).reshape(FOX_HEADS, nblk)
    qk = jnp.sqrt(jnp.max(stat[:, :, 2 * nb], axis=0) * jnp.max(stat[:, :, 2 * nb + 1], axis=0))
    bound = 2.0 * NORM_SLACK * qk[:, None, None] + LOG2E * (fmax[:, :, None] - fmin[:, None, :])
    dead = (bound <= -PRUNE_LOG2).reshape(FOX_HEADS // ATTN_HEADS, ATTN_HEADS, nblk, nblk).all(axis=1)
    key_blk = lax.broadcasted_iota(jnp.int32, dead.shape, 2)
    first_live = jnp.min(jnp.where(dead, nblk, key_blk), axis=2)
    return (first_live // ATTN_UNROLL).astype(jnp.int32)


def _attn_prompt(qa, ka, vt, stat):
    t = ka.shape[0]
    tq = ATTN_TILE
    nh = ATTN_HEADS
    nblk = t // tq
    once = pl.Buffered(1)
    grid_spec = pltpu.PrefetchScalarGridSpec(
        num_scalar_prefetch=1,
        grid=(FOX_HEADS // nh, nblk),
        in_specs=[pl.BlockSpec((nh * AUG, tq), lambda p, i, g0: (p, i)),
                  pl.BlockSpec((t, nh * AUG), lambda p, i, g0: (0, p), pipeline_mode=once),
                  pl.BlockSpec((nblk, nh * HEAD_DIM, tq), lambda p, i, g0: (0, p, 0), pipeline_mode=once)],
        out_specs=pl.BlockSpec((tq, nh * HEAD_DIM), lambda p, i, g0: (i, p)),
        scratch_shapes=[pltpu.VMEM((2, nh, tq, tq), F32),
                        pltpu.VMEM((nh, HEAD_DIM, tq), F32)])
    return pl.pallas_call(
        _attn_kernel,
        out_shape=jax.ShapeDtypeStruct((t, FOX_WIDTH), BF16),
        grid_spec=grid_spec,
        compiler_params=_params("arbitrary", "arbitrary"),
        name="attn_prompt",
    )(_first_live_group(stat, nblk), qa, ka, vt)


def _post_decode_kernel(pt_ref, x_ref, at_ref, wa_ref, yc_ref, wc_ref, gm_ref, shf_ref, scf_ref, gf_ref,
                        g_ref, wup_ref, wdn_ref, q_ref, kn_ref, vn_ref, lfn_ref, *rest):
    del pt_ref
    npg = PAGES_PER_STEP
    k_refs, v_refs, lf_refs = rest[:npg], rest[npg:2 * npg], rest[2 * npg:3 * npg]
    o_ref, od_ref, h_ref, x1_ref, mlp_ref = rest[3 * npg:3 * npg + 5]
    dec = rest[3 * npg + 5:]
    s = pl.program_id(1)
    nsteps = pl.num_programs(1)

    @pl.when(s == 0)
    def _():
        mix = (jnp.dot(at_ref[...], wa_ref[...], preferred_element_type=F32)
               + jnp.dot(yc_ref[...], wc_ref[...], preferred_element_type=F32))
        x1 = x_ref[...] + gm_ref[...] * mix
        x1_ref[...] = x1
        h_ref[...] = _modnorm(x1, g_ref[...], shf_ref[...], scf_ref[...]).astype(BF16)
        mlp_ref[...] = jnp.zeros_like(mlp_ref)
        _decode_init(q_ref, kn_ref, vn_ref, lfn_ref, *dec)

    a = jnp.maximum(jnp.dot(h_ref[...], wup_ref[s], preferred_element_type=F32), 0.0)
    mlp_ref[...] += jnp.dot((a * a).astype(BF16), wdn_ref[s], preferred_element_type=F32)
    _decode_pages(k_refs, v_refs, lf_refs, *dec)

    @pl.when(s == nsteps - 1)
    def _():
        o_ref[...] = x1_ref[...] + gf_ref[...] * mlp_ref[...]
        _decode_final(od_ref, dec[3], dec[5])


def _rows_to_tile(vals):
    row = lax.broadcasted_iota(jnp.int32, (FOX_HEADS, V7X_LANES), 0)
    out = jnp.zeros((FOX_HEADS, V7X_LANES), F32)
    for h in range(FOX_HEADS):
        out = jnp.where(row == h, jnp.broadcast_to(vals[h], (FOX_HEADS, V7X_LANES)), out)
    return out


def _decode_init(q_ref, kn_ref, vn_ref, lfn_ref, qb_ref, tri_ref, m_ref, l_ref, run_ref, acc_ref):
    lanes = V7X_LANES
    lane = lax.broadcasted_iota(jnp.int32, (1, lanes), 1)
    tr = lax.broadcasted_iota(jnp.int32, (PAGE_SIZE, 2 * lanes), 0)
    tc = lax.broadcasted_iota(jnp.int32, (PAGE_SIZE, 2 * lanes), 1)
    tri_ref[...] = ((tr > tc) | (tc >= lanes)).astype(BF16)
    run_ref[...] = jnp.broadcast_to(lfn_ref[0], (FOX_HEADS, lanes))
    s_new = []
    for h in range(FOX_HEADS):
        qb_ref[h] = jnp.broadcast_to(q_ref[0, h], (HEAD_DIM, lanes))
        s_new.append(jnp.sum(q_ref[0, h] * kn_ref[0, h], axis=0, keepdims=True))
        acc_ref[h] = jnp.where(lane == 0, jnp.broadcast_to(vn_ref[0, h], (HEAD_DIM, lanes)), 0.0)
    m_ref[...] = _rows_to_tile([jnp.broadcast_to(s, (1, lanes)) for s in s_new])
    l_ref[...] = jnp.ones_like(l_ref)


def _decode_final(o_ref, l_ref, acc_ref):
    for h in range(FOX_HEADS):
        o_ref[0, h] = jnp.sum(acc_ref[h], axis=1, keepdims=True) / l_ref[h:h + 1, 0:1]


def _decode_pages(k_refs, v_refs, lf_refs, qb_ref, tri_ref, m_ref, l_ref, run_ref, acc_ref):
    npg = len(k_refs)
    lanes = V7X_LANES
    heads = range(FOX_HEADS)
    rows_to_tile = _rows_to_tile

    parts = jnp.concatenate([p for i in range(npg) for p in _split3(lf_refs[i][0])], axis=0)
    cs = jnp.dot(parts.astype(BF16), tri_ref[...], preferred_element_type=F32)
    run = run_ref[...]
    scores = []
    for i in range(npg):
        c3 = cs[24 * i:24 * i + 8] + cs[24 * i + 8:24 * i + 16] + cs[24 * i + 16:24 * i + 24]
        qk = rows_to_tile([jnp.sum(k_refs[i][0, h] * qb_ref[h], axis=0, keepdims=True) for h in heads])
        scores.append(qk + (c3[:, :lanes] + run) * LOG2E)
        run = run + c3[:, lanes:]
    run_ref[...] = run

    m_prev = m_ref[...]
    m_blk = scores[0]
    for s in scores[1:]:
        m_blk = jnp.maximum(m_blk, s)
    m_new = jnp.maximum(m_prev, jnp.max(m_blk, axis=1, keepdims=True))
    alpha = jnp.exp2(m_prev - m_new)
    ps = [jnp.exp2(s - m_new) for s in scores]
    p_tot = ps[0]
    for p in ps[1:]:
        p_tot = p_tot + p
    l_ref[...] = alpha * l_ref[...] + jnp.sum(p_tot, axis=1, keepdims=True)
    m_ref[...] = m_new
    for h in heads:
        acc = acc_ref[h] * alpha[h:h + 1, :]
        for i in range(npg):
            acc = acc + v_refs[i][0, h] * ps[i][h:h + 1, :]
        acc_ref[h] = acc


def _post_decode(x, at, w_at, yc, w_yc, mod, g_mlp, w_up, w_down,
                 q, k_new, v_new, lf_new, cache_k, cache_v, cache_logf, page_table):
    t = x.shape[0]
    tm = ROW_TILE
    b = q.shape[0]
    n_pages = page_table.shape[1]
    npg = PAGES_PER_STEP
    nsteps = n_pages // npg
    assert t // tm == b, "one prompt row tile per sample sequence"
    fch = D_FF // nsteps
    col = (FOX_HEADS, HEAD_DIM, 1)
    w_up_c = w_up.reshape(D_MODEL, nsteps, fch).transpose(1, 0, 2)
    w_dn_c = w_down.reshape(nsteps, fch, D_MODEL)
    ck = cache_k.transpose(0, 2, 3, 1)
    cv = cache_v.transpose(0, 2, 3, 1)
    clf = cache_logf.transpose(0, 2, 1)

    def page_map(nd):
        def for_slot(slot):
            return lambda bi, g, pt: (pt[bi, n_pages - 1 - (g * npg + slot)],) + (0,) * (nd - 1)
        return for_slot

    def const(shape):
        nd = len(shape)
        return pl.BlockSpec(shape, lambda bi, g, pt: (0,) * nd, pipeline_mode=pl.Buffered(1))

    rows = lambda bi, g, pt: (bi, 0)
    mcol = lambda c: (lambda bi, g, pt: (0, c))
    seq4 = lambda bi, g, pt: (bi, 0, 0, 0)
    page_blk = (1, FOX_HEADS, HEAD_DIM, PAGE_SIZE)
    in_specs = [pl.BlockSpec((tm, D_MODEL), rows),
                pl.BlockSpec((tm, FOX_WIDTH), rows), const(w_at.shape),
                pl.BlockSpec((tm, CONF_CH), rows), const(w_yc.shape),
                pl.BlockSpec((1, D_MODEL), mcol(2)), pl.BlockSpec((1, D_MODEL), mcol(3)),
                pl.BlockSpec((1, D_MODEL), mcol(4)), pl.BlockSpec((1, D_MODEL), mcol(5)),
                const((1, D_MODEL)), const(w_up_c.shape), const(w_dn_c.shape)]
    in_specs += [pl.BlockSpec((1,) + col, seq4)] * 3 + [pl.BlockSpec((1, FOX_HEADS, 1), lambda bi, g, pt: (bi, 0, 0))]
    in_specs += [pl.BlockSpec(page_blk, page_map(4)(s)) for s in range(npg)]
    in_specs += [pl.BlockSpec(page_blk, page_map(4)(s)) for s in range(npg)]
    in_specs += [pl.BlockSpec((1, FOX_HEADS, PAGE_SIZE), page_map(3)(s)) for s in range(npg)]
    grid_spec = pltpu.PrefetchScalarGridSpec(
        num_scalar_prefetch=1,
        grid=(b, nsteps),
        in_specs=in_specs,
        out_specs=[pl.BlockSpec((tm, D_MODEL), rows), pl.BlockSpec((1,) + col, seq4)],
        scratch_shapes=[pltpu.VMEM((tm, D_MODEL), BF16),
                        pltpu.VMEM((tm, D_MODEL), F32),
                        pltpu.VMEM((tm, D_MODEL), F32),
                        pltpu.VMEM((FOX_HEADS, HEAD_DIM, V7X_LANES), F32),
                        pltpu.VMEM((PAGE_SIZE, 2 * V7X_LANES), BF16),
                        pltpu.VMEM((FOX_HEADS, V7X_LANES), F32),
                        pltpu.VMEM((FOX_HEADS, V7X_LANES), F32),
                        pltpu.VMEM((FOX_HEADS, V7X_LANES), F32),
                        pltpu.VMEM((FOX_HEADS, HEAD_DIM, V7X_LANES), F32)])
    y, dec = pl.pallas_call(
        _post_decode_kernel,
        out_shape=[jax.ShapeDtypeStruct((t, D_MODEL), F32), jax.ShapeDtypeStruct((b,) + col, F32)],
        grid_spec=grid_spec,
        compiler_params=pltpu.CompilerParams(dimension_semantics=("arbitrary", "arbitrary"),
                                             vmem_limit_bytes=FUSED_VMEM_LIMIT),
        name="post_decode",
    )(page_table, x, at, w_at, yc, w_yc, mod, mod, mod, mod, g_mlp, w_up_c, w_dn_c,
      q.reshape((b,) + col), k_new.reshape((b,) + col), v_new.reshape((b,) + col),
      lf_new.reshape(b, FOX_HEADS, 1), *([ck] * npg), *([cv] * npg), *([clf] * npg))
    return y, dec.reshape(b, FOX_WIDTH).astype(BF16)


def _post_kernel(*refs, n_mix, final):
    x_ref = refs[0]
    mix_refs = refs[1:1 + 2 * n_mix]
    gm_ref, shf_ref, scf_ref, gf_ref, g_ref, wup_ref, wdn_ref = refs[1 + 2 * n_mix:8 + 2 * n_mix]
    rest = refs[8 + 2 * n_mix:]
    if final:
        gfin_ref, o_ref = rest
    else:
        (o_ref,) = rest
    mix = None
    for a in range(n_mix):
        t = jnp.dot(mix_refs[2 * a][...], mix_refs[2 * a + 1][...], preferred_element_type=F32)
        mix = t if mix is None else mix + t
    x1 = x_ref[...] + gm_ref[...] * mix
    h = _modnorm(x1, g_ref[...], shf_ref[...], scf_ref[...]).astype(BF16)
    fchunk = 1024
    acc = None
    for f0 in range(0, D_FF, fchunk):
        a = jnp.maximum(jnp.dot(h, wup_ref[:, f0:f0 + fchunk], preferred_element_type=F32), 0.0)
        t = jnp.dot((a * a).astype(BF16), wdn_ref[f0:f0 + fchunk, :], preferred_element_type=F32)
        acc = t if acc is None else acc + t
    x2 = x1 + gf_ref[...] * acc
    if final:
        ms = jnp.mean(x2 * x2, axis=-1, keepdims=True)
        x2 = x2 * lax.rsqrt(ms + EPS) * gfin_ref[...]
    o_ref[...] = x2


def _post(x, mixes, mod, g_mlp, w_up, w_down, final_g=None):
    t = x.shape[0]
    tm = min(ROW_TILE, t)
    mm = mod.shape[0]
    bm = 1 if mm == 1 else tm
    row = lambda i: (i, 0)

    def mcol(c):
        return (lambda i: (0, c)) if mm == 1 else (lambda i: (i, c))

    in_specs = [pl.BlockSpec((tm, D_MODEL), row)]
    args = [x]
    for a, w in mixes:
        in_specs += [pl.BlockSpec((tm, a.shape[1]), row), _const_spec(w.shape)]
        args += [a, w]
    in_specs += [pl.BlockSpec((bm, D_MODEL), mcol(2)), pl.BlockSpec((bm, D_MODEL), mcol(3)),
                 pl.BlockSpec((bm, D_MODEL), mcol(4)), pl.BlockSpec((bm, D_MODEL), mcol(5)),
                 _const_spec((1, D_MODEL)), _const_spec(w_up.shape), _const_spec(w_down.shape)]
    args += [mod, mod, mod, mod, g_mlp, w_up, w_down]
    if final_g is not None:
        in_specs.append(_const_spec((1, D_MODEL)))
        args.append(final_g)
    return pl.pallas_call(
        functools.partial(_post_kernel, n_mix=len(mixes), final=final_g is not None),
        out_shape=jax.ShapeDtypeStruct((t, D_MODEL), F32),
        grid=(t // tm,),
        in_specs=in_specs,
        out_specs=pl.BlockSpec((tm, D_MODEL), row),
        compiler_params=_params("arbitrary"),
        name="post",
    )(*args)


def _mix1_kernel(x_ref, sh_ref, sc_ref, g_ref, win_ref, wdw_ref, *rest, per_row_state):
    x = x_ref[...]
    tm = x.shape[0]
    h = _modnorm(x, g_ref[...], sh_ref[...], sc_ref[...]).astype(BF16)
    proj = jnp.dot(h, win_ref[...], preferred_element_type=F32)
    b_gate = proj[:, :D_MODEL]
    cx = proj[:, D_MODEL:2 * D_MODEL] * proj[:, 2 * D_MODEL:]
    w0, w1, w2 = wdw_ref[0:1, :], wdw_ref[1:2, :], wdw_ref[2:3, :]
    if per_row_state:
        s0_ref, s1_ref, by_ref, cx_ref = rest
        y = w0 * s0_ref[...] + w1 * s1_ref[...] + w2 * cx
        cx_ref[...] = cx
    else:
        by_ref, tail_ref, buf_ref = rest
        i = pl.program_id(0)
        pad = V7X_SUBLANES

        @pl.when(i == 0)
        def _():
            buf_ref[0:pad, :] = jnp.zeros((pad, D_MODEL), F32)

        buf_ref[pad:, :] = cx
        y = (w0 * buf_ref[pl.ds(pad - 2, tm), :] + w1 * buf_ref[pl.ds(pad - 1, tm), :] + w2 * cx)
        tail = cx[tm - pad:, :]
        buf_ref[0:pad, :] = tail
        tail_ref[...] = tail
    by_ref[...] = (b_gate * y).astype(BF16)


def _mix1(x, mod, g, w_in, w_dw, state=None):
    t = x.shape[0]
    tm = min(ROW_TILE, t)
    mm = mod.shape[0]
    bm = 1 if mm == 1 else tm
    row = lambda i: (i, 0)
    mcol = lambda c: (lambda i: (0, c)) if mm == 1 else (lambda i: (i, c))
    in_specs = [pl.BlockSpec((tm, D_MODEL), row),
                pl.BlockSpec((bm, D_MODEL), mcol(0)), pl.BlockSpec((bm, D_MODEL), mcol(1)),
                _const_spec((1, D_MODEL)), _const_spec(w_in.shape), _const_spec(w_dw.shape)]
    args = [x, mod, mod, g, w_in, w_dw]
    if state is not None:
        in_specs += [pl.BlockSpec((tm, D_MODEL), row)] * 2
        args += [state[0], state[1]]
        out_shape = [jax.ShapeDtypeStruct((t, D_MODEL), BF16), jax.ShapeDtypeStruct((t, D_MODEL), F32)]
        out_specs = [pl.BlockSpec((tm, D_MODEL), row)] * 2
        scratch = []
    else:
        nt = t // tm
        out_shape = [jax.ShapeDtypeStruct((t, D_MODEL), BF16),
                     jax.ShapeDtypeStruct((nt * V7X_SUBLANES, D_MODEL), F32)]
        out_specs = [pl.BlockSpec((tm, D_MODEL), row), pl.BlockSpec((V7X_SUBLANES, D_MODEL), row)]
        scratch = [pltpu.VMEM((tm + V7X_SUBLANES, D_MODEL), F32)]
    return pl.pallas_call(
        functools.partial(_mix1_kernel, per_row_state=state is not None),
        out_shape=out_shape,
        grid=(t // tm,),
        in_specs=in_specs,
        out_specs=out_specs,
        scratch_shapes=scratch,
        compiler_params=_params("arbitrary"),
        name="mix1",
    )(*args)


def kernel(x_prompt, x_sample, cache_k, cache_v, cache_logf, state_conformer_conv, state_short_conv, page_table, c_prompt, c_sample, l0_w_ada, l0_b_ada, l0_norm_mix, l0_norm_mlp, l0_w_in, l0_b_forget, l0_w_dw, l0_b_dw, l0_conv_ln_g, l0_conv_ln_b, l0_w_out, l0_w_up, l0_w_down, l1_w_ada, l1_b_ada, l1_norm_mix, l1_norm_mlp, l1_w_in, l1_w_dw, l1_w_out, l1_w_up, l1_w_down, final_norm):
    bp, t, d = x_prompt.shape
    bs = x_sample.shape[0]
    assert bp == 1 and d == D_MODEL and x_sample.shape[1] == 1

    def per_head_padded(wcols, scale):
        w3 = (wcols * scale).reshape(d, FOX_HEADS, HEAD_DIM)
        return jnp.pad(w3, ((0, 0), (0, 0), (0, AUG - HEAD_DIM))).reshape(d, FOX_HEADS * AUG).astype(BF16)

    w0 = {
        "q": per_head_padded(l0_w_in[:, :FOX_WIDTH], ATTN_SCALE * LOG2E),
        "k": per_head_padded(l0_w_in[:, FOX_WIDTH:2 * FOX_WIDTH], 1.0),
        "kv": l0_w_in[:, FOX_WIDTH:3 * FOX_WIDTH].astype(BF16),
        "f":jnp.pad(l0_w_in[:, 3 * FOX_WIDTH:3 * FOX_WIDTH + FOX_HEADS],
                     ((0, 0), (0, V7X_LANES - FOX_HEADS))).astype(BF16),
        "bf": jnp.pad(l0_b_forget, (0, V7X_LANES - FOX_HEADS)).reshape(1, V7X_LANES),
        "glu": l0_w_in[:, 3 * FOX_WIDTH + FOX_HEADS:].astype(BF16),
    }
    w0_out_attn = l0_w_out[:FOX_WIDTH].astype(BF16)
    w0_out_conv = l0_w_out[FOX_WIDTH:].astype(BF16)
    w0_up, w0_down = l0_w_up.astype(BF16), l0_w_down.astype(BF16)
    w1_in, w1_out = l1_w_in.astype(BF16), l1_w_out.astype(BF16)
    w1_up, w1_down = l1_w_up.astype(BF16), l1_w_down.astype(BF16)
    g0_mix, g0_mlp = l0_norm_mix.reshape(1, d), l0_norm_mlp.reshape(1, d)
    g1_mix, g1_mlp = l1_norm_mix.reshape(1, d), l1_norm_mlp.reshape(1, d)
    gfin = final_norm.reshape(1, d)

    c_all = jnp.concatenate([c_prompt, c_sample], axis=0)
    mod0 = _ada(c_all, l0_w_ada, l0_b_ada)
    mod1 = _ada(c_all, l1_w_ada, l1_b_ada)
    mod0_p, mod0_s = mod0[:1], mod0[1:]
    mod1_p, mod1_s = mod1[:1], mod1[1:]

    xp = x_prompt.reshape(t, d)
    qa_p, ka_p, vt_p, st_p, k_p, v_p, lf_p, u_p = _proj0(xp, mod0_p, g0_mix, w0, True)
    yc_p = _conv0_prompt(u_p, l0_w_dw, l0_b_dw, l0_conv_ln_g, l0_conv_ln_b)
    at_p = _attn_prompt(qa_p, ka_p, vt_p, st_p)

    xs = x_sample.reshape(bs, d)
    qp_s, k_s, v_s, lf_s, u_s = _proj0(xs, mod0_s, g0_mix, w0, False)
    q_s = qp_s.reshape(bs, FOX_HEADS, AUG)[:, :, :HEAD_DIM].reshape(bs, FOX_WIDTH)
    yc_s = _conv0_sample(state_conformer_conv.transpose(1, 0, 2), u_s,
                         l0_w_dw, l0_b_dw, l0_conv_ln_g, l0_conv_ln_b)

    x1_p, at_s = _post_decode(xp, at_p, w0_out_attn, yc_p, w0_out_conv, mod0_p, g0_mlp, w0_up, w0_down,
                              q_s, k_s, v_s, lf_s, cache_k, cache_v, cache_logf, page_table)

    by_p, tail_p = _mix1(x1_p, mod1_p, g1_mix, w1_in, l1_w_dw)
    y_p = _post(x1_p, [(by_p, w1_out)], mod1_p, g1_mlp, w1_up, w1_down, gfin)

    x1_s = _post(xs, [(at_s, w0_out_attn), (yc_s, w0_out_conv)], mod0_s, g0_mlp, w0_up, w0_down)
    by_s, cx_s = _mix1(x1_s, mod1_s, g1_mix, w1_in, l1_w_dw,
                       state=(state_short_conv[:, 0], state_short_conv[:, 1]))
    y_s = _post(x1_s, [(by_s, w1_out)], mod1_s, g1_mlp, w1_up, w1_down, gfin)

    hs = (FOX_HEADS, HEAD_DIM)
    return (y_p.reshape(1, t, d), y_s.reshape(bs, 1, d),
            k_p.reshape(1, t, *hs), v_p.reshape(1, t, *hs), lf_p.reshape(1, t, FOX_HEADS),
            u_p[t - (CONF_CONV_WIDTH - 1):].reshape(1, CONF_CONV_WIDTH - 1, CONF_CH),
            tail_p[-(SC_CONV_WIDTH - 1):].reshape(1, SC_CONV_WIDTH - 1, d),
            k_s.reshape(bs, 1, *hs), v_s.reshape(bs, 1, *hs), lf_s.reshape(bs, 1, FOX_HEADS),
            jnp.concatenate([state_conformer_conv[:, 1:], u_s[:, None, :]], axis=1),
            jnp.stack([state_short_conv[:, 1], cx_s], axis=1))
```

```python
import functools

import jax
import jax.numpy as jnp
import numpy as np
from jax import lax
from jax.experimental import pallas as pl
from jax.experimental.pallas import tpu as pltpu

F32 = jnp.float32
BF16 = jnp.bfloat16

D_MODEL = 1024
FOX_HEADS = 8
HEAD_DIM = 64
FOX_WIDTH = FOX_HEADS * HEAD_DIM
CONF_CH = D_MODEL - FOX_WIDTH
CONF_CONV_WIDTH = 31
SC_CONV_WIDTH = 3
D_FF = 4 * D_MODEL
PAGE_SIZE = 128
EPS = 1e-6
ATTN_SCALE = HEAD_DIM ** -0.5
LOG2E = 1.4426950408889634

V7X_LANES = 128
V7X_SUBLANES = 8
V7X_VMEM_BYTES = 64 * 1024 * 1024
VMEM_LIMIT = V7X_VMEM_BYTES - 12 * 1024 * 1024
FUSED_VMEM_LIMIT = V7X_VMEM_BYTES - 4 * 1024 * 1024

ROW_TILE = 512
CONV_TILE = 256
CONV_CHUNK = 64
CONV_HALO = 32
ATTN_TILE = 256
AUG = V7X_LANES
ATTN_HEADS = 4
ATTN_UNROLL = 2
PRUNE_LOG2 = 136.0
MIX_CHUNK = 256
NORM_SLACK = 1.02
PAGES_PER_STEP = 16


def _const_spec(shape):
    nd = len(shape)
    return pl.BlockSpec(shape, lambda *_: (0,) * nd, pipeline_mode=pl.Buffered(1))


def _params(*sem):
    return pltpu.CompilerParams(dimension_semantics=sem, vmem_limit_bytes=VMEM_LIMIT)


def _modnorm(x, g, shift, scale):
    ms = jnp.mean(x * x, axis=-1, keepdims=True)
    y = x * lax.rsqrt(ms + EPS)
    return (y * g) * (1.0 + scale) + shift


def _split3(x):
    hi = x.astype(BF16).astype(F32)
    r = x - hi
    mid = r.astype(BF16).astype(F32)
    lo = (r - mid).astype(BF16).astype(F32)
    return hi, mid, lo


def _ada_kernel(c_ref, w_ref, b_ref, o_ref):
    c = c_ref[...]
    s = (c * jax.nn.sigmoid(c)).astype(BF16)
    o_ref[...] = jnp.dot(s, w_ref[...].astype(BF16), preferred_element_type=F32) + b_ref[...]


def _ada(c_all, w_ada, b_ada):
    m = c_all.shape[0]
    n = w_ada.shape[1]
    bn = 1536
    return pl.pallas_call(
        _ada_kernel,
        out_shape=jax.ShapeDtypeStruct((m, n), F32),
        grid=(n // bn,),
        in_specs=[pl.BlockSpec((m, D_MODEL), lambda j: (0, 0)),
                  pl.BlockSpec((D_MODEL, bn), lambda j: (0, j)),
                  pl.BlockSpec((1, bn), lambda j: (0, j))],
        out_specs=pl.BlockSpec((m, bn), lambda j: (0, j)),
        compiler_params=_params("arbitrary"),
        name="ada",
    )(c_all, w_ada, b_ada.reshape(1, n))


def _bias_placement():
    pq = np.zeros((V7X_LANES, FOX_HEADS * AUG), np.float32)
    pk = np.zeros((V7X_LANES, FOX_HEADS * AUG), np.float32)
    for h in range(FOX_HEADS):
        for p in range(3):
            pq[p * FOX_HEADS + h, h * AUG + HEAD_DIM + p] = 1.0
            pk[p * FOX_HEADS + h, h * AUG + HEAD_DIM + 3 + p] = -1.0
            pq[3 * FOX_HEADS, h * AUG + HEAD_DIM + 3 + p] = 1.0
            pk[3 * FOX_HEADS, h * AUG + HEAD_DIM + p] = 1.0
    return jnp.asarray(pq, BF16), jnp.asarray(pk, BF16)


def _proj0_kernel(*refs, prompt):
    x_ref, sh_ref, sc_ref, g_ref, wq_ref, wkv_ref, wf_ref, bf_ref, wglu_ref = refs[:9]
    x = x_ref[...]
    h = _modnorm(x, g_ref[...], sh_ref[...], sc_ref[...]).astype(BF16)
    kv = jnp.dot(h, wkv_ref[...], preferred_element_type=F32)
    k = kv[:, :FOX_WIDTH]
    v = kv[:, FOX_WIDTH:]
    fg = jnp.dot(h, wf_ref[...], preferred_element_type=F32) + bf_ref[...]
    lf = jnp.minimum(fg, 0.0) - jnp.log1p(jnp.exp(-jnp.abs(fg)))
    glu = jnp.dot(h, wglu_ref[...], preferred_element_type=F32)
    u = glu[:, :CONF_CH] * jax.nn.sigmoid(glu[:, CONF_CH:])
    qp = jnp.dot(h, wq_ref[...], preferred_element_type=F32)

    if not prompt:
        q_ref, k_ref, v_ref, lf_ref, u_ref = refs[9:]
        q_ref[...] = qp
    else:
        (wk_ref, pq_ref, pk_ref, qa_ref, ka_ref, vt_ref, st_ref, k_ref, v_ref, lf_ref, u_ref,
         tri_ref, carry_ref) = refs[9:]
        tm = x.shape[0]
        tk = vt_ref.shape[2]
        i = pl.program_id(0)

        @pl.when(i == 0)
        def _():
            carry_ref[...] = jnp.zeros_like(carry_ref)
            r = lax.broadcasted_iota(jnp.int32, (tm, tm), 0)
            c = lax.broadcasted_iota(jnp.int32, (tm, tm), 1)
            tri_ref[...] = (r <= c).astype(BF16)

        lft = lf.T[:FOX_HEADS]
        parts = jnp.concatenate(_split3(lft), axis=0).astype(BF16)
        cs = jnp.dot(parts, tri_ref[...], preferred_element_type=F32)
        ft = cs[0:8] + cs[8:16] + cs[16:24] + carry_ref[...]
        carry_ref[...] = ft[:, tm - 1:tm]

        pieces_t = jnp.concatenate(
            list(_split3(ft * LOG2E)) + [jnp.ones((FOX_HEADS, tm), F32),
                                 jnp.zeros((V7X_LANES - 4 * FOX_HEADS, tm), F32)], axis=0)
        pieces = pieces_t.T.astype(BF16)
        qpt = (qp + jnp.dot(pieces, pq_ref[...], preferred_element_type=F32)).T
        qa_ref[...] = qpt.astype(BF16)
        kp = jnp.dot(h, wk_ref[...], preferred_element_type=F32)
        ka_ref[...] = (kp + jnp.dot(pieces, pk_ref[...], preferred_element_type=F32)).astype(BF16)
        vtt = v.T.astype(BF16)
        for c0 in range(tm // tk):
            vt_ref[c0] = vtt[:, c0 * tk:(c0 + 1) * tk]

        nb = tm // tk
        row = lax.broadcasted_iota(jnp.int32, (FOX_HEADS, V7X_LANES), 0)
        lane = lax.broadcasted_iota(jnp.int32, (FOX_HEADS, V7X_LANES), 1)
        stat = jnp.zeros((FOX_HEADS, V7X_LANES), F32)
        for c0 in range(nb):
            blk = ft[:, c0 * tk:(c0 + 1) * tk]
            stat = jnp.where(lane == c0, jnp.max(blk, axis=1, keepdims=True), stat)
            stat = jnp.where(lane == nb + c0, jnp.min(blk, axis=1, keepdims=True), stat)
        for hh in range(FOX_HEADS):
            qh = qpt[hh * AUG:hh * AUG + HEAD_DIM, :]
            qn2 = jnp.max(jnp.sum(qh * qh, axis=0, keepdims=True), axis=1, keepdims=True)
            kh = kp[:, hh * AUG:(hh + 1) * AUG]
            kn2 = jnp.max(jnp.sum(kh * kh, axis=1, keepdims=True), axis=0, keepdims=True)
            stat = jnp.where((row == hh) & (lane == 2 * nb), qn2, stat)
            stat = jnp.where((row == hh) & (lane == 2 * nb + 1), kn2, stat)
        st_ref[0] = stat

    k_ref[...] = k
    v_ref[...] = v
    lf_ref[...] = lf[:, :FOX_HEADS]
    u_ref[...] = u


def _proj0(x, mod, g, w, prompt):
    t = x.shape[0]
    tm = min(ROW_TILE, t)
    mm = mod.shape[0]
    mrow = (lambda i: (0, 0)) if mm == 1 else (lambda i: (i, 0))
    mrow1 = (lambda i: (0, 1)) if mm == 1 else (lambda i: (i, 1))
    bm = 1 if mm == 1 else tm
    row = lambda i: (i, 0)
    wide = FOX_HEADS * AUG
    in_specs = [pl.BlockSpec((tm, D_MODEL), row),
                pl.BlockSpec((bm, D_MODEL), mrow),
                pl.BlockSpec((bm, D_MODEL), mrow1),
                _const_spec((1, D_MODEL)),
                _const_spec(w["q"].shape),
                _const_spec(w["kv"].shape),
                _const_spec(w["f"].shape),
                _const_spec((1, V7X_LANES)),
                _const_spec(w["glu"].shape)]
    args = [x, mod, mod, g, w["q"], w["kv"], w["f"], w["bf"], w["glu"]]
    tail_shape = [jax.ShapeDtypeStruct((t, FOX_WIDTH), F32), jax.ShapeDtypeStruct((t, FOX_WIDTH), F32),
                  jax.ShapeDtypeStruct((t, FOX_HEADS), F32), jax.ShapeDtypeStruct((t, CONF_CH), F32)]
    tail_specs = [pl.BlockSpec((tm, FOX_WIDTH), row), pl.BlockSpec((tm, FOX_WIDTH), row),
                  pl.BlockSpec((tm, FOX_HEADS), row), pl.BlockSpec((tm, CONF_CH), row)]
    if prompt:
        tk = ATTN_TILE
        pq, pk = _bias_placement()
        in_specs += [_const_spec(w["k"].shape), _const_spec(pq.shape), _const_spec(pk.shape)]
        args += [w["k"], pq, pk]
        out_shape = [jax.ShapeDtypeStruct((wide, t), BF16), jax.ShapeDtypeStruct((t, wide), BF16),
                     jax.ShapeDtypeStruct((t // tk, FOX_WIDTH, tk), BF16),
                     jax.ShapeDtypeStruct((t // tm, FOX_HEADS, V7X_LANES), F32)] + tail_shape
        out_specs = [pl.BlockSpec((wide, tm), lambda i: (0, i)), pl.BlockSpec((tm, wide), row),
                     pl.BlockSpec((tm // tk, FOX_WIDTH, tk), lambda i: (i, 0, 0)),
                     pl.BlockSpec((1, FOX_HEADS, V7X_LANES), lambda i: (i, 0, 0))] + tail_specs
        scratch = [pltpu.VMEM((tm, tm), BF16), pltpu.VMEM((FOX_HEADS, 1), F32)]
    else:
        out_shape = [jax.ShapeDtypeStruct((t, wide), F32)] + tail_shape
        out_specs = [pl.BlockSpec((tm, wide), row)] + tail_specs
        scratch = []
    return pl.pallas_call(
        functools.partial(_proj0_kernel, prompt=prompt),
        out_shape=out_shape,
        grid=(t // tm,),
        in_specs=in_specs,
        out_specs=out_specs,
        scratch_shapes=scratch,
        compiler_params=_params("arbitrary"),
        name="proj0",
    )(*args)


def _ln_silu(y, ln_g, ln_b):
    mu = jnp.mean(y, axis=-1, keepdims=True)
    d = y - mu
    var = jnp.mean(d * d, axis=-1, keepdims=True)
    z = d * lax.rsqrt(var + EPS) * ln_g + ln_b
    return z * jax.nn.sigmoid(z)


def _conv0_kernel(u_ref, up_ref, wdw_ref, bdw_ref, lng_ref, lnb_ref, y_ref, buf_ref):
    i = pl.program_id(0)
    tm = u_ref.shape[0]
    buf_ref[0, 0:CONV_HALO, :] = jnp.where(i == 0, 0.0, up_ref[...])
    buf_ref[0, CONV_HALO:, :] = u_ref[...]
    span = tm + CONV_HALO - V7X_SUBLANES
    for r in range(1, V7X_SUBLANES):
        buf_ref[r, 0:span, :] = buf_ref[0, pl.ds(r, span), :]
    first = CONV_HALO - (CONF_CONV_WIDTH - 1)
    for c0 in range(0, tm, CONV_CHUNK):
        acc = jnp.zeros((CONV_CHUNK, CONF_CH), F32)
        for kk in range(CONF_CONV_WIDTH):
            r = (first + kk) % V7X_SUBLANES
            acc = acc + wdw_ref[kk:kk + 1, :] * buf_ref[r, pl.ds(first + kk - r + c0, CONV_CHUNK), :]
        z = _ln_silu(acc + bdw_ref[...], lng_ref[...], lnb_ref[...])
        y_ref[c0:c0 + CONV_CHUNK, :] = z.astype(BF16)


def _conv0_prompt(u, w_dw, b_dw, ln_g, ln_b):
    t = u.shape[0]
    tm = CONV_TILE
    per = tm // CONV_HALO
    return pl.pallas_call(
        _conv0_kernel,
        out_shape=jax.ShapeDtypeStruct((t, CONF_CH), BF16),
        grid=(t // tm,),
        in_specs=[pl.BlockSpec((tm, CONF_CH), lambda i: (i, 0)),
                  pl.BlockSpec((CONV_HALO, CONF_CH), lambda i: (jnp.maximum(i * per - 1, 0), 0)),
                  _const_spec((CONF_CONV_WIDTH, CONF_CH)),
                  _const_spec((1, CONF_CH)), _const_spec((1, CONF_CH)), _const_spec((1, CONF_CH))],
        out_specs=pl.BlockSpec((tm, CONF_CH), lambda i: (i, 0)),
        scratch_shapes=[pltpu.VMEM((V7X_SUBLANES, tm + CONV_HALO, CONF_CH), F32)],
        compiler_params=_params("arbitrary"),
        name="conv0",
    )(u, u, w_dw, b_dw.reshape(1, -1), ln_g.reshape(1, -1), ln_b.reshape(1, -1))


def _conv0s_kernel(st_ref, u_ref, wdw_ref, bdw_ref, lng_ref, lnb_ref, y_ref):
    acc = wdw_ref[CONF_CONV_WIDTH - 1:CONF_CONV_WIDTH, :] * u_ref[...]
    for kk in range(CONF_CONV_WIDTH - 1):
        acc = acc + wdw_ref[kk:kk + 1, :] * st_ref[kk]
    z = _ln_silu(acc + bdw_ref[...], lng_ref[...], lnb_ref[...])
    y_ref[...] = z.astype(BF16)


def _conv0_sample(state_t, u, w_dw, b_dw, ln_g, ln_b):
    b = u.shape[0]
    return pl.pallas_call(
        _conv0s_kernel,
        out_shape=jax.ShapeDtypeStruct((b, CONF_CH), BF16),
        name="conv0s",
    )(state_t, u, w_dw, b_dw.reshape(1, -1), ln_g.reshape(1, -1), ln_b.reshape(1, -1))


def _attn_kernel(g0_ref, qa_ref, ka_ref, vt_ref, o_ref, s_ref, acc_ref):
    i = pl.program_id(1)
    g0 = g0_ref[pl.program_id(0), i]
    tq = qa_ref.shape[1]
    heads = range(ATTN_HEADS)
    qa = [qa_ref[hh * AUG:(hh + 1) * AUG, :] for hh in heads]

    def scores(j):
        start = pl.multiple_of(j * tq, tq)
        return [jnp.dot(ka_ref[pl.ds(start, tq), hh * AUG:(hh + 1) * AUG], qa[hh],
                        preferred_element_type=F32)
                for hh in heads]

    def fold(j, slot, carry, masked):
        out = []
        for hh in heads:
            m_prev, l_prev = carry[hh]
            st = s_ref[slot, hh]
            if masked:
                key = j * tq + lax.broadcasted_iota(jnp.int32, st.shape, 0)
                qry = i * tq + lax.broadcasted_iota(jnp.int32, st.shape, 1)
                st = jnp.where(key <= qry, st, -jnp.inf)
            m_new = jnp.maximum(m_prev, jnp.max(st, axis=0, keepdims=True))
            alpha = jnp.exp2(m_prev - m_new)
            pt = jnp.exp2(st - m_new)
            l_new = alpha * l_prev + jnp.sum(pt, axis=0, keepdims=True)
            vt = vt_ref[jnp.minimum(j, last), hh * HEAD_DIM:(hh + 1) * HEAD_DIM, :]
            acc_ref[hh] = acc_ref[hh] * alpha + jnp.dot(vt, pt.astype(BF16), preferred_element_type=F32)
            out.append((m_new, l_new))
        return tuple(out)

    def step(j, slot, carry, masked):
        nxt = scores(jnp.minimum(j + 1, last))
        carry = fold(j, slot, carry, masked)
        for hh in heads:
            s_ref[1 - slot, hh] = nxt[hh]
        return carry

    def group(gi, carry):
        for u in range(ATTN_UNROLL):
            carry = step(ATTN_UNROLL * gi + u, u % 2, carry, False)
        return carry

    last = vt_ref.shape[0] - 1
    first = scores(ATTN_UNROLL * g0)
    for hh in heads:
        s_ref[0, hh] = first[hh]
    acc_ref[...] = jnp.zeros_like(acc_ref)
    init = tuple((jnp.full((1, tq), -jnp.inf, F32), jnp.zeros((1, tq), F32)) for _ in heads)
    ngroup = i // ATTN_UNROLL
    carry = lax.fori_loop(g0, ngroup, group, init)
    for u in range(ATTN_UNROLL - 1):
        carry = step(ATTN_UNROLL * ngroup + u, u % 2, carry, True)
    carry = fold(ATTN_UNROLL * ngroup + ATTN_UNROLL - 1, (ATTN_UNROLL - 1) % 2, carry, True)
    ot = jnp.concatenate([acc_ref[hh] / carry[hh][1] for hh in heads], axis=0)
    o_ref[...] = ot.T.astype(BF16)


def _first_live_group(stat, nblk):
    per = stat.shape[0]
    nb = nblk // per
    fmax = stat[:, :, 0:nb].transpose(1, 0, 2).reshape(FOX_HEADS, nblk)
    fmin = stat[:, :, nb:2 * nb].transpose(1, 0, 2).reshape(FOX_HEADS, nblk)
    qk = jnp.sqrt(jnp.max(stat[:, :, 2 * nb], axis=0) * jnp.max(stat[:, :, 2 * nb + 1], axis=0))
    bound = 2.0 * NORM_SLACK * qk[:, None, None] + LOG2E * (fmax[:, :, None] - fmin[:, None, :])
    dead = (bound <= -PRUNE_LOG2).reshape(FOX_HEADS // ATTN_HEADS, ATTN_HEADS, nblk, nblk).all(axis=1)
    key_blk = lax.broadcasted_iota(jnp.int32, dead.shape, 2)
    first_live = jnp.min(jnp.where(dead, nblk, key_blk), axis=2)
    return (first_live // ATTN_UNROLL).astype(jnp.int32)


def _attn_prompt(qa, ka, vt, stat):
    t = ka.shape[0]
    tq = ATTN_TILE
    nh = ATTN_HEADS
    nblk = t // tq
    once = pl.Buffered(1)
    grid_spec = pltpu.PrefetchScalarGridSpec(
        num_scalar_prefetch=1,
        grid=(FOX_HEADS // nh, nblk),
        in_specs=[pl.BlockSpec((nh * AUG, tq), lambda p, i, g0: (p, i)),
                  pl.BlockSpec((t, nh * AUG), lambda p, i, g0: (0, p), pipeline_mode=once),
                  pl.BlockSpec((nblk, nh * HEAD_DIM, tq), lambda p, i, g0: (0, p, 0), pipeline_mode=once)],
        out_specs=pl.BlockSpec((tq, nh * HEAD_DIM), lambda p, i, g0: (i, p)),
        scratch_shapes=[pltpu.VMEM((2, nh, tq, tq), F32),
                        pltpu.VMEM((nh, HEAD_DIM, tq), F32)])
    return pl.pallas_call(
        _attn_kernel,
        out_shape=jax.ShapeDtypeStruct((t, FOX_WIDTH), BF16),
        grid_spec=grid_spec,
        compiler_params=_params("arbitrary", "arbitrary"),
        name="attn_prompt",
    )(_first_live_group(stat, nblk), qa, ka, vt)


def _post_decode_kernel(pt_ref, x_ref, at_ref, wa_ref, yc_ref, wc_ref, gm_ref, shf_ref, scf_ref, gf_ref,
                        g_ref, wup_ref, wdn_ref, q_ref, kn_ref, vn_ref, lfn_ref, *rest):
    del pt_ref
    npg = PAGES_PER_STEP
    k_refs, v_refs, lf_refs = rest[:npg], rest[npg:2 * npg], rest[2 * npg:3 * npg]
    o_ref, od_ref, h_ref, x1_ref, mlp_ref = rest[3 * npg:3 * npg + 5]
    dec = rest[3 * npg + 5:]
    s = pl.program_id(1)
    nsteps = pl.num_programs(1)

    @pl.when(s == 0)
    def _():
        mix = (jnp.dot(at_ref[...], wa_ref[...], preferred_element_type=F32)
               + jnp.dot(yc_ref[...], wc_ref[...], preferred_element_type=F32))
        x1 = x_ref[...] + gm_ref[...] * mix
        x1_ref[...] = x1
        h_ref[...] = _modnorm(x1, g_ref[...], shf_ref[...], scf_ref[...]).astype(BF16)
        mlp_ref[...] = jnp.zeros_like(mlp_ref)
        _decode_init(q_ref, kn_ref, vn_ref, lfn_ref, *dec)

    a = jnp.maximum(jnp.dot(h_ref[...], wup_ref[s], preferred_element_type=F32), 0.0)
    mlp_ref[...] += jnp.dot((a * a).astype(BF16), wdn_ref[s], preferred_element_type=F32)
    _decode_pages(k_refs, v_refs, lf_refs, *dec)

    @pl.when(s == nsteps - 1)
    def _():
        o_ref[...] = x1_ref[...] + gf_ref[...] * mlp_ref[...]
        _decode_final(od_ref, dec[3], dec[5])


def _rows_to_tile(vals):
    row = lax.broadcasted_iota(jnp.int32, (FOX_HEADS, V7X_LANES), 0)
    out = jnp.zeros((FOX_HEADS, V7X_LANES), F32)
    for h in range(FOX_HEADS):
        out = jnp.where(row == h, jnp.broadcast_to(vals[h], (FOX_HEADS, V7X_LANES)), out)
    return out


def _decode_init(q_ref, kn_ref, vn_ref, lfn_ref, qb_ref, tri_ref, m_ref, l_ref, run_ref, acc_ref):
    lanes = V7X_LANES
    lane = lax.broadcasted_iota(jnp.int32, (1, lanes), 1)
    tr = lax.broadcasted_iota(jnp.int32, (PAGE_SIZE, 2 * lanes), 0)
    tc = lax.broadcasted_iota(jnp.int32, (PAGE_SIZE, 2 * lanes), 1)
    tri_ref[...] = ((tr > tc) | (tc >= lanes)).astype(BF16)
    run_ref[...] = jnp.broadcast_to(lfn_ref[0], (FOX_HEADS, lanes))
    s_new = []
    for h in range(FOX_HEADS):
        qb_ref[h] = jnp.broadcast_to(q_ref[0, h], (HEAD_DIM, lanes))
        s_new.append(jnp.sum(q_ref[0, h] * kn_ref[0, h], axis=0, keepdims=True))
        acc_ref[h] = jnp.where(lane == 0, jnp.broadcast_to(vn_ref[0, h], (HEAD_DIM, lanes)), 0.0)
    m_ref[...] = _rows_to_tile([jnp.broadcast_to(s, (1, lanes)) for s in s_new])
    l_ref[...] = jnp.ones_like(l_ref)


def _decode_final(o_ref, l_ref, acc_ref):
    for h in range(FOX_HEADS):
        o_ref[0, h] = jnp.sum(acc_ref[h], axis=1, keepdims=True) / l_ref[h:h + 1, 0:1]


def _decode_pages(k_refs, v_refs, lf_refs, qb_ref, tri_ref, m_ref, l_ref, run_ref, acc_ref):
    npg = len(k_refs)
    lanes = V7X_LANES
    heads = range(FOX_HEADS)
    rows_to_tile = _rows_to_tile

    parts = jnp.concatenate([p for i in range(npg) for p in _split3(lf_refs[i][0])], axis=0)
    cs = jnp.dot(parts.astype(BF16), tri_ref[...], preferred_element_type=F32)
    run = run_ref[...]
    scores = []
    for i in range(npg):
        c3 = cs[24 * i:24 * i + 8] + cs[24 * i + 8:24 * i + 16] + cs[24 * i + 16:24 * i + 24]
        qk = rows_to_tile([jnp.sum(k_refs[i][0, h] * qb_ref[h], axis=0, keepdims=True) for h in heads])
        scores.append(qk + (c3[:, :lanes] + run) * LOG2E)
        run = run + c3[:, lanes:]
    run_ref[...] = run

    m_prev = m_ref[...]
    m_blk = scores[0]
    for s in scores[1:]:
        m_blk = jnp.maximum(m_blk, s)
    m_new = jnp.maximum(m_prev, jnp.max(m_blk, axis=1, keepdims=True))
    alpha = jnp.exp2(m_prev - m_new)
    ps = [jnp.exp2(s - m_new) for s in scores]
    p_tot = ps[0]
    for p in ps[1:]:
        p_tot = p_tot + p
    l_ref[...] = alpha * l_ref[...] + jnp.sum(p_tot, axis=1, keepdims=True)
    m_ref[...] = m_new
    for h in heads:
        acc = acc_ref[h] * alpha[h:h + 1, :]
        for i in range(npg):
            acc = acc + v_refs[i][0, h] * ps[i][h:h + 1, :]
        acc_ref[h] = acc


def _post_decode(x, at, w_at, yc, w_yc, mod, g_mlp, w_up, w_down,
                 q, k_new, v_new, lf_new, cache_k, cache_v, cache_logf, page_table):
    t = x.shape[0]
    tm = ROW_TILE
    b = q.shape[0]
    n_pages = page_table.shape[1]
    npg = PAGES_PER_STEP
    nsteps = n_pages // npg
    assert t // tm == b, "one prompt row tile per sample sequence"
    fch = D_FF // nsteps
    col = (FOX_HEADS, HEAD_DIM, 1)
    w_up_c = w_up.reshape(D_MODEL, nsteps, fch).transpose(1, 0, 2)
    w_dn_c = w_down.reshape(nsteps, fch, D_MODEL)
    ck = cache_k.transpose(0, 2, 3, 1)
    cv = cache_v.transpose(0, 2, 3, 1)
    clf = cache_logf.transpose(0, 2, 1)

    def page_map(nd):
        def for_slot(slot):
            return lambda bi, g, pt: (pt[bi, n_pages - 1 - (g * npg + slot)],) + (0,) * (nd - 1)
        return for_slot

    def const(shape):
        nd = len(shape)
        return pl.BlockSpec(shape, lambda bi, g, pt: (0,) * nd, pipeline_mode=pl.Buffered(1))

    rows = lambda bi, g, pt: (bi, 0)
    mcol = lambda c: (lambda bi, g, pt: (0, c))
    seq4 = lambda bi, g, pt: (bi, 0, 0, 0)
    page_blk = (1, FOX_HEADS, HEAD_DIM, PAGE_SIZE)
    in_specs = [pl.BlockSpec((tm, D_MODEL), rows),
                pl.BlockSpec((tm, FOX_WIDTH), rows), const(w_at.shape),
                pl.BlockSpec((tm, CONF_CH), rows), const(w_yc.shape),
                pl.BlockSpec((1, D_MODEL), mcol(2)), pl.BlockSpec((1, D_MODEL), mcol(3)),
                pl.BlockSpec((1, D_MODEL), mcol(4)), pl.BlockSpec((1, D_MODEL), mcol(5)),
                const((1, D_MODEL)), const(w_up_c.shape), const(w_dn_c.shape)]
    in_specs += [pl.BlockSpec((1,) + col, seq4)] * 3 + [pl.BlockSpec((1, FOX_HEADS, 1), lambda bi, g, pt: (bi, 0, 0))]
    in_specs += [pl.BlockSpec(page_blk, page_map(4)(s)) for s in range(npg)]
    in_specs += [pl.BlockSpec(page_blk, page_map(4)(s)) for s in range(npg)]
    in_specs += [pl.BlockSpec((1, FOX_HEADS, PAGE_SIZE), page_map(3)(s)) for s in range(npg)]
    grid_spec = pltpu.PrefetchScalarGridSpec(
        num_scalar_prefetch=1,
        grid=(b, nsteps),
        in_specs=in_specs,
        out_specs=[pl.BlockSpec((tm, D_MODEL), rows), pl.BlockSpec((1,) + col, seq4)],
        scratch_shapes=[pltpu.VMEM((tm, D_MODEL), BF16),
                        pltpu.VMEM((tm, D_MODEL), F32),
                        pltpu.VMEM((tm, D_MODEL), F32),
                        pltpu.VMEM((FOX_HEADS, HEAD_DIM, V7X_LANES), F32),
                        pltpu.VMEM((PAGE_SIZE, 2 * V7X_LANES), BF16),
                        pltpu.VMEM((FOX_HEADS, V7X_LANES), F32),
                        pltpu.VMEM((FOX_HEADS, V7X_LANES), F32),
                        pltpu.VMEM((FOX_HEADS, V7X_LANES), F32),
                        pltpu.VMEM((FOX_HEADS, HEAD_DIM, V7X_LANES), F32)])
    y, dec = pl.pallas_call(
        _post_decode_kernel,
        out_shape=[jax.ShapeDtypeStruct((t, D_MODEL), F32), jax.ShapeDtypeStruct((b,) + col, F32)],
        grid_spec=grid_spec,
        compiler_params=pltpu.CompilerParams(dimension_semantics=("arbitrary", "arbitrary"),
                                             vmem_limit_bytes=FUSED_VMEM_LIMIT),
        name="post_decode",
    )(page_table, x, at, w_at, yc, w_yc, mod, mod, mod, mod, g_mlp, w_up_c, w_dn_c,
      q.reshape((b,) + col), k_new.reshape((b,) + col), v_new.reshape((b,) + col),
      lf_new.reshape(b, FOX_HEADS, 1), *([ck] * npg), *([cv] * npg), *([clf] * npg))
    return y, dec.reshape(b, FOX_WIDTH).astype(BF16)


def _post_kernel(*refs, n_mix, final):
    x_ref = refs[0]
    mix_refs = refs[1:1 + 2 * n_mix]
    gm_ref, shf_ref, scf_ref, gf_ref, g_ref, wup_ref, wdn_ref = refs[1 + 2 * n_mix:8 + 2 * n_mix]
    rest = refs[8 + 2 * n_mix:]
    if final:
        gfin_ref, o_ref = rest
    else:
        (o_ref,) = rest
    mix = None
    for a in range(n_mix):
        t = jnp.dot(mix_refs[2 * a][...], mix_refs[2 * a + 1][...], preferred_element_type=F32)
        mix = t if mix is None else mix + t
    x1 = x_ref[...] + gm_ref[...] * mix
    h = _modnorm(x1, g_ref[...], shf_ref[...], scf_ref[...]).astype(BF16)
    fchunk = 1024
    acc = None
    for f0 in range(0, D_FF, fchunk):
        a = jnp.maximum(jnp.dot(h, wup_ref[:, f0:f0 + fchunk], preferred_element_type=F32), 0.0)
        t = jnp.dot((a * a).astype(BF16), wdn_ref[f0:f0 + fchunk, :], preferred_element_type=F32)
        acc = t if acc is None else acc + t
    x2 = x1 + gf_ref[...] * acc
    if final:
        ms = jnp.mean(x2 * x2, axis=-1, keepdims=True)
        x2 = x2 * lax.rsqrt(ms + EPS) * gfin_ref[...]
    o_ref[...] = x2


def _post(x, mixes, mod, g_mlp, w_up, w_down, final_g=None):
    t = x.shape[0]
    tm = min(ROW_TILE, t)
    mm = mod.shape[0]
    bm = 1 if mm == 1 else tm
    row = lambda i: (i, 0)

    def mcol(c):
        return (lambda i: (0, c)) if mm == 1 else (lambda i: (i, c))

    in_specs = [pl.BlockSpec((tm, D_MODEL), row)]
    args = [x]
    for a, w in mixes:
        in_specs += [pl.BlockSpec((tm, a.shape[1]), row), _const_spec(w.shape)]
        args += [a, w]
    in_specs += [pl.BlockSpec((bm, D_MODEL), mcol(2)), pl.BlockSpec((bm, D_MODEL), mcol(3)),
                 pl.BlockSpec((bm, D_MODEL), mcol(4)), pl.BlockSpec((bm, D_MODEL), mcol(5)),
                 _const_spec((1, D_MODEL)), _const_spec(w_up.shape), _const_spec(w_down.shape)]
    args += [mod, mod, mod, mod, g_mlp, w_up, w_down]
    if final_g is not None:
        in_specs.append(_const_spec((1, D_MODEL)))
        args.append(final_g)
    return pl.pallas_call(
        functools.partial(_post_kernel, n_mix=len(mixes), final=final_g is not None),
        out_shape=jax.ShapeDtypeStruct((t, D_MODEL), F32),
        grid=(t // tm,),
        in_specs=in_specs,
        out_specs=pl.BlockSpec((tm, D_MODEL), row),
        compiler_params=_params("arbitrary"),
        name="post",
    )(*args)


def _mix1_kernel(x_ref, sh_ref, sc_ref, g_ref, win_ref, wdw_ref, *rest, per_row_state):
    x = x_ref[...]
    tm = x.shape[0]
    h = _modnorm(x, g_ref[...], sh_ref[...], sc_ref[...]).astype(BF16)
    pad = V7X_SUBLANES
    if not per_row_state:
        by_ref, tail_ref, buf_ref = rest

        @pl.when(pl.program_id(0) == 0)
        def _():
            buf_ref[0:pad, :] = jnp.zeros((pad, D_MODEL), F32)
    else:
        s0_ref, s1_ref, by_ref, cx_ref = rest

    for c0 in range(0, D_MODEL, MIX_CHUNK):
        cols = slice(c0, c0 + MIX_CHUNK)

        def proj(part):
            w = win_ref[:, part * D_MODEL + c0:part * D_MODEL + c0 + MIX_CHUNK]
            return jnp.dot(h, w, preferred_element_type=F32)

        b_gate = proj(0)
        cx = proj(1) * proj(2)
        w0, w1, w2 = wdw_ref[0:1, cols], wdw_ref[1:2, cols], wdw_ref[2:3, cols]
        if per_row_state:
            y = w0 * s0_ref[:, cols] + w1 * s1_ref[:, cols] + w2 * cx
            cx_ref[:, cols] = cx
        else:
            buf_ref[pad:, cols] = cx
            y = w0 * buf_ref[pl.ds(pad - 2, tm), cols] + w1 * buf_ref[pl.ds(pad - 1, tm), cols] + w2 * cx
            tail = cx[tm - pad:, :]
            buf_ref[0:pad, cols] = tail
            tail_ref[:, cols] = tail
        by_ref[:, cols] = (b_gate * y).astype(BF16)


def _mix1(x, mod, g, w_in, w_dw, state=None):
    t = x.shape[0]
    tm = min(ROW_TILE, t)
    mm = mod.shape[0]
    bm = 1 if mm == 1 else tm
    row = lambda i: (i, 0)
    mcol = lambda c: (lambda i: (0, c)) if mm == 1 else (lambda i: (i, c))
    in_specs = [pl.BlockSpec((tm, D_MODEL), row),
                pl.BlockSpec((bm, D_MODEL), mcol(0)), pl.BlockSpec((bm, D_MODEL), mcol(1)),
                _const_spec((1, D_MODEL)), _const_spec(w_in.shape), _const_spec(w_dw.shape)]
    args = [x, mod, mod, g, w_in, w_dw]
    if state is not None:
        in_specs += [pl.BlockSpec((tm, D_MODEL), row)] * 2
        args += [state[0], state[1]]
        out_shape = [jax.ShapeDtypeStruct((t, D_MODEL), BF16), jax.ShapeDtypeStruct((t, D_MODEL), F32)]
        out_specs = [pl.BlockSpec((tm, D_MODEL), row)] * 2
        scratch = []
    else:
        nt = t // tm
        out_shape = [jax.ShapeDtypeStruct((t, D_MODEL), BF16),
                     jax.ShapeDtypeStruct((nt * V7X_SUBLANES, D_MODEL), F32)]
        out_specs = [pl.BlockSpec((tm, D_MODEL), row), pl.BlockSpec((V7X_SUBLANES, D_MODEL), row)]
        scratch = [pltpu.VMEM((tm + V7X_SUBLANES, D_MODEL), F32)]
    return pl.pallas_call(
        functools.partial(_mix1_kernel, per_row_state=state is not None),
        out_shape=out_shape,
        grid=(t // tm,),
        in_specs=in_specs,
        out_specs=out_specs,
        scratch_shapes=scratch,
        compiler_params=_params("arbitrary"),
        name="mix1",
    )(*args)


def kernel(x_prompt, x_sample, cache_k, cache_v, cache_logf, state_conformer_conv, state_short_conv, page_table, c_prompt, c_sample, l0_w_ada, l0_b_ada, l0_norm_mix, l0_norm_mlp, l0_w_in, l0_b_forget, l0_w_dw, l0_b_dw, l0_conv_ln_g, l0_conv_ln_b, l0_w_out, l0_w_up, l0_w_down, l1_w_ada, l1_b_ada, l1_norm_mix, l1_norm_mlp, l1_w_in, l1_w_dw, l1_w_out, l1_w_up, l1_w_down, final_norm):
    bp, t, d = x_prompt.shape
    bs = x_sample.shape[0]
    assert bp == 1 and d == D_MODEL and x_sample.shape[1] == 1

    def per_head_padded(wcols, scale):
        w3 = (wcols * scale).reshape(d, FOX_HEADS, HEAD_DIM)
        return jnp.pad(w3, ((0, 0), (0, 0), (0, AUG - HEAD_DIM))).reshape(d, FOX_HEADS * AUG).astype(BF16)

    w0 = {
        "q": per_head_padded(l0_w_in[:, :FOX_WIDTH], ATTN_SCALE * LOG2E),
        "k": per_head_padded(l0_w_in[:, FOX_WIDTH:2 * FOX_WIDTH], 1.0),
        "kv": l0_w_in[:, FOX_WIDTH:3 * FOX_WIDTH].astype(BF16),
        "f":jnp.pad(l0_w_in[:, 3 * FOX_WIDTH:3 * FOX_WIDTH + FOX_HEADS],
                     ((0, 0), (0, V7X_LANES - FOX_HEADS))).astype(BF16),
        "bf": jnp.pad(l0_b_forget, (0, V7X_LANES - FOX_HEADS)).reshape(1, V7X_LANES),
        "glu": l0_w_in[:, 3 * FOX_WIDTH + FOX_HEADS:].astype(BF16),
    }
    w0_out_attn = l0_w_out[:FOX_WIDTH].astype(BF16)
    w0_out_conv = l0_w_out[FOX_WIDTH:].astype(BF16)
    w0_up, w0_down = l0_w_up.astype(BF16), l0_w_down.astype(BF16)
    w1_in, w1_out = l1_w_in.astype(BF16), l1_w_out.astype(BF16)
    w1_up, w1_down = l1_w_up.astype(BF16), l1_w_down.astype(BF16)
    g0_mix, g0_mlp = l0_norm_mix.reshape(1, d), l0_norm_mlp.reshape(1, d)
    g1_mix, g1_mlp = l1_norm_mix.reshape(1, d), l1_norm_mlp.reshape(1, d)
    gfin = final_norm.reshape(1, d)

    c_all = jnp.concatenate([c_prompt, c_sample], axis=0)
    mod0 = _ada(c_all, l0_w_ada, l0_b_ada)
    mod1 = _ada(c_all, l1_w_ada, l1_b_ada)
    mod0_p, mod0_s = mod0[:1], mod0[1:]
    mod1_p, mod1_s = mod1[:1], mod1[1:]

    xp = x_prompt.reshape(t, d)
    qa_p, ka_p, vt_p, st_p, k_p, v_p, lf_p, u_p = _proj0(xp, mod0_p, g0_mix, w0, True)
    yc_p = _conv0_prompt(u_p, l0_w_dw, l0_b_dw, l0_conv_ln_g, l0_conv_ln_b)
    at_p = _attn_prompt(qa_p, ka_p, vt_p, st_p)

    xs = x_sample.reshape(bs, d)
    qp_s, k_s, v_s, lf_s, u_s = _proj0(xs, mod0_s, g0_mix, w0, False)
    q_s = qp_s.reshape(bs, FOX_HEADS, AUG)[:, :, :HEAD_DIM].reshape(bs, FOX_WIDTH)
    yc_s = _conv0_sample(state_conformer_conv.transpose(1, 0, 2), u_s,
                         l0_w_dw, l0_b_dw, l0_conv_ln_g, l0_conv_ln_b)

    x1_p, at_s = _post_decode(xp, at_p, w0_out_attn, yc_p, w0_out_conv, mod0_p, g0_mlp, w0_up, w0_down,
                              q_s, k_s, v_s, lf_s, cache_k, cache_v, cache_logf, page_table)

    by_p, tail_p = _mix1(x1_p, mod1_p, g1_mix, w1_in, l1_w_dw)
    y_p = _post(x1_p, [(by_p, w1_out)], mod1_p, g1_mlp, w1_up, w1_down, gfin)

    x1_s = _post(xs, [(at_s, w0_out_attn), (yc_s, w0_out_conv)], mod0_s, g0_mlp, w0_up, w0_down)
    by_s, cx_s = _mix1(x1_s, mod1_s, g1_mix, w1_in, l1_w_dw,
                       state=(state_short_conv[:, 0], state_short_conv[:, 1]))
    y_s = _post(x1_s, [(by_s, w1_out)], mod1_s, g1_mlp, w1_up, w1_down, gfin)

    hs = (FOX_HEADS, HEAD_DIM)
    return (y_p.reshape(1, t, d), y_s.reshape(bs, 1, d),
            k_p.reshape(1, t, *hs), v_p.reshape(1, t, *hs), lf_p.reshape(1, t, FOX_HEADS),
            u_p[t - (CONF_CONV_WIDTH - 1):].reshape(1, CONF_CONV_WIDTH - 1, CONF_CH),
            tail_p[-(SC_CONV_WIDTH - 1):].reshape(1, SC_CONV_WIDTH - 1, d),
            k_s.reshape(bs, 1, *hs), v_s.reshape(bs, 1, *hs), lf_s.reshape(bs, 1, FOX_HEADS),
            jnp.concatenate([state_conformer_conv[:, 1:], u_s[:, None, :]], axis=1),
            jnp.stack([state_short_conv[:, 1], cx_s], axis=1))
```

```python
import functools

import jax
import jax.numpy as jnp
import numpy as np
from jax import lax
from jax.experimental import pallas as pl
from jax.experimental.pallas import tpu as pltpu

F32 = jnp.float32
BF16 = jnp.bfloat16

D_MODEL = 1024
FOX_HEADS = 8
HEAD_DIM = 64
FOX_WIDTH = FOX_HEADS * HEAD_DIM
CONF_CH = D_MODEL - FOX_WIDTH
CONF_CONV_WIDTH = 31
SC_CONV_WIDTH = 3
D_FF = 4 * D_MODEL
PAGE_SIZE = 128
EPS = 1e-6
ATTN_SCALE = HEAD_DIM ** -0.5
LOG2E = 1.4426950408889634

V7X_LANES = 128
V7X_SUBLANES = 8
V7X_VMEM_BYTES = 64 * 1024 * 1024
VMEM_LIMIT = V7X_VMEM_BYTES - 12 * 1024 * 1024
FUSED_VMEM_LIMIT = V7X_VMEM_BYTES - 4 * 1024 * 1024

ROW_TILE = 512
CONV_TILE = 256
CONV_CHUNK = 64
CONV_HALO = 32
ATTN_TILE = 256
AUG = V7X_LANES
ATTN_HEADS = 4
ATTN_UNROLL = 2
PRUNE_LOG2 = 136.0
MIX_CHUNK = 256
NORM_SLACK = 1.02
PAGES_PER_STEP = 16


def _const_spec(shape):
    nd = len(shape)
    return pl.BlockSpec(shape, lambda *_: (0,) * nd, pipeline_mode=pl.Buffered(1))


def _params(*sem):
    return pltpu.CompilerParams(dimension_semantics=sem, vmem_limit_bytes=VMEM_LIMIT)


def _modnorm(x, g, shift, scale):
    ms = jnp.mean(x * x, axis=-1, keepdims=True)
    y = x * lax.rsqrt(ms + EPS)
    return (y * g) * (1.0 + scale) + shift


def _split3(x):
    hi = x.astype(BF16).astype(F32)
    r = x - hi
    mid = r.astype(BF16).astype(F32)
    lo = (r - mid).astype(BF16).astype(F32)
    return hi, mid, lo


def _ada_kernel(c_ref, w_ref, b_ref, o_ref):
    c = c_ref[...]
    s = (c * jax.nn.sigmoid(c)).astype(BF16)
    o_ref[...] = jnp.dot(s, w_ref[...].astype(BF16), preferred_element_type=F32) + b_ref[...]


def _ada(c_all, w_ada, b_ada):
    m = c_all.shape[0]
    n = w_ada.shape[1]
    bn = 1536
    return pl.pallas_call(
        _ada_kernel,
        out_shape=jax.ShapeDtypeStruct((m, n), F32),
        grid=(n // bn,),
        in_specs=[pl.BlockSpec((m, D_MODEL), lambda j: (0, 0)),
                  pl.BlockSpec((D_MODEL, bn), lambda j: (0, j)),
                  pl.BlockSpec((1, bn), lambda j: (0, j))],
        out_specs=pl.BlockSpec((m, bn), lambda j: (0, j)),
        compiler_params=_params("arbitrary"),
        name="ada",
    )(c_all, w_ada, b_ada.reshape(1, n))


def _bias_placement():
    pq = np.zeros((V7X_LANES, FOX_HEADS * AUG), np.float32)
    pk = np.zeros((V7X_LANES, FOX_HEADS * AUG), np.float32)
    for h in range(FOX_HEADS):
        for p in range(3):
            pq[p * FOX_HEADS + h, h * AUG + HEAD_DIM + p] = 1.0
            pk[p * FOX_HEADS + h, h * AUG + HEAD_DIM + 3 + p] = -1.0
            pq[3 * FOX_HEADS, h * AUG + HEAD_DIM + 3 + p] = 1.0
            pk[3 * FOX_HEADS, h * AUG + HEAD_DIM + p] = 1.0
    return jnp.asarray(pq, BF16), jnp.asarray(pk, BF16)


def _proj0_kernel(*refs, prompt):
    x_ref, sh_ref, sc_ref, g_ref, wq_ref, wkv_ref, wf_ref, bf_ref, wglu_ref = refs[:9]
    tm = x_ref.shape[0]
    if prompt:
        (wk_ref, pq_ref, pk_ref, qa_ref, ka_ref, vt_ref, st_ref, k_ref, v_ref, lf_ref, u_ref,
         tri_ref, carry_ref) = refs[9:]
        tk = vt_ref.shape[2]

        @pl.when(pl.program_id(0) == 0)
        def _():
            carry_ref[...] = jnp.zeros_like(carry_ref)
            r = lax.broadcasted_iota(jnp.int32, (tm, tm), 0)
            c = lax.broadcasted_iota(jnp.int32, (tm, tm), 1)
            tri_ref[...] = (r <= c).astype(BF16)
    else:
        q_ref, k_ref, v_ref, lf_ref, u_ref = refs[9:]

    x = x_ref[...]
    h = _modnorm(x, g_ref[...], sh_ref[...], sc_ref[...]).astype(BF16)
    fg = jnp.dot(h, wf_ref[...], preferred_element_type=F32) + bf_ref[...]
    lf = jnp.minimum(fg, 0.0) - jnp.log1p(jnp.exp(-jnp.abs(fg)))
    if prompt:
        lft = lf.T[:FOX_HEADS]
        parts = jnp.concatenate(_split3(lft), axis=0).astype(BF16)
        cs = jnp.dot(parts, tri_ref[...], preferred_element_type=F32)
        ft = cs[0:8] + cs[8:16] + cs[16:24] + carry_ref[...]
        carry_ref[...] = ft[:, tm - 1:tm]
        pieces_t = jnp.concatenate(
            list(_split3(ft * LOG2E)) + [jnp.ones((FOX_HEADS, tm), F32),
                                 jnp.zeros((V7X_LANES - 4 * FOX_HEADS, tm), F32)], axis=0)
        pieces = pieces_t.T.astype(BF16)

    kv = jnp.dot(h, wkv_ref[...], preferred_element_type=F32)
    k = kv[:, :FOX_WIDTH]
    v = kv[:, FOX_WIDTH:]
    glu = jnp.dot(h, wglu_ref[...], preferred_element_type=F32)
    u = glu[:, :CONF_CH] * jax.nn.sigmoid(glu[:, CONF_CH:])
    qp = jnp.dot(h, wq_ref[...], preferred_element_type=F32)

    if not prompt:
        q_ref[...] = qp
    else:
        qpt = (qp + jnp.dot(pieces, pq_ref[...], preferred_element_type=F32)).T
        qa_ref[...] = qpt.astype(BF16)
        kp = jnp.dot(h, wk_ref[...], preferred_element_type=F32)
        ka_ref[...] = (kp + jnp.dot(pieces, pk_ref[...], preferred_element_type=F32)).astype(BF16)
        vtt = v.T.astype(BF16)
        for c0 in range(tm // tk):
            vt_ref[c0] = vtt[:, c0 * tk:(c0 + 1) * tk]

        nb = tm // tk
        row = lax.broadcasted_iota(jnp.int32, (FOX_HEADS, V7X_LANES), 0)
        lane = lax.broadcasted_iota(jnp.int32, (FOX_HEADS, V7X_LANES), 1)
        stat = jnp.zeros((FOX_HEADS, V7X_LANES), F32)
        for c0 in range(nb):
            blk = ft[:, c0 * tk:(c0 + 1) * tk]
            stat = jnp.where(lane == c0, jnp.max(blk, axis=1, keepdims=True), stat)
            stat = jnp.where(lane == nb + c0, jnp.min(blk, axis=1, keepdims=True), stat)
        for hh in range(FOX_HEADS):
            qh = qpt[hh * AUG:hh * AUG + HEAD_DIM, :]
            qn2 = jnp.max(jnp.sum(qh * qh, axis=0, keepdims=True), axis=1, keepdims=True)
            kh = kp[:, hh * AUG:(hh + 1) * AUG]
            kn2 = jnp.max(jnp.sum(kh * kh, axis=1, keepdims=True), axis=0, keepdims=True)
            stat = jnp.where((row == hh) & (lane == 2 * nb), qn2, stat)
            stat = jnp.where((row == hh) & (lane == 2 * nb + 1), kn2, stat)
        st_ref[0] = stat

    k_ref[...] = k
    v_ref[...] = v
    lf_ref[...] = lf[:, :FOX_HEADS]
    u_ref[...] = u


def _proj0(x, mod, g, w, prompt):
    t = x.shape[0]
    tm = min(ROW_TILE, t)
    mm = mod.shape[0]
    mrow = (lambda i: (0, 0)) if mm == 1 else (lambda i: (i, 0))
    mrow1 = (lambda i: (0, 1)) if mm == 1 else (lambda i: (i, 1))
    bm = 1 if mm == 1 else tm
    row = lambda i: (i, 0)
    wide = FOX_HEADS * AUG
    in_specs = [pl.BlockSpec((tm, D_MODEL), row),
                pl.BlockSpec((bm, D_MODEL), mrow),
                pl.BlockSpec((bm, D_MODEL), mrow1),
                _const_spec((1, D_MODEL)),
                _const_spec(w["q"].shape),
                _const_spec(w["kv"].shape),
                _const_spec(w["f"].shape),
                _const_spec((1, V7X_LANES)),
                _const_spec(w["glu"].shape)]
    args = [x, mod, mod, g, w["q"], w["kv"], w["f"], w["bf"], w["glu"]]
    tail_shape = [jax.ShapeDtypeStruct((t, FOX_WIDTH), F32), jax.ShapeDtypeStruct((t, FOX_WIDTH), F32),
                  jax.ShapeDtypeStruct((t, FOX_HEADS), F32), jax.ShapeDtypeStruct((t, CONF_CH), F32)]
    tail_specs = [pl.BlockSpec((tm, FOX_WIDTH), row), pl.BlockSpec((tm, FOX_WIDTH), row),
                  pl.BlockSpec((tm, FOX_HEADS), row), pl.BlockSpec((tm, CONF_CH), row)]
    if prompt:
        tk = ATTN_TILE
        pq, pk = _bias_placement()
        in_specs += [_const_spec(w["k"].shape), _const_spec(pq.shape), _const_spec(pk.shape)]
        args += [w["k"], pq, pk]
        out_shape = [jax.ShapeDtypeStruct((wide, t), BF16), jax.ShapeDtypeStruct((t, wide), BF16),
                     jax.ShapeDtypeStruct((t // tk, FOX_WIDTH, tk), BF16),
                     jax.ShapeDtypeStruct((t // tm, FOX_HEADS, V7X_LANES), F32)] + tail_shape
        out_specs = [pl.BlockSpec((wide, tm), lambda i: (0, i)), pl.BlockSpec((tm, wide), row),
                     pl.BlockSpec((tm // tk, FOX_WIDTH, tk), lambda i: (i, 0, 0)),
                     pl.BlockSpec((1, FOX_HEADS, V7X_LANES), lambda i: (i, 0, 0))] + tail_specs
        scratch = [pltpu.VMEM((tm, tm), BF16), pltpu.VMEM((FOX_HEADS, 1), F32)]
    else:
        out_shape = [jax.ShapeDtypeStruct((t, wide), F32)] + tail_shape
        out_specs = [pl.BlockSpec((tm, wide), row)] + tail_specs
        scratch = []
    return pl.pallas_call(
        functools.partial(_proj0_kernel, prompt=prompt),
        out_shape=out_shape,
        grid=(t // tm,),
        in_specs=in_specs,
        out_specs=out_specs,
        scratch_shapes=scratch,
        compiler_params=_params("arbitrary"),
        name="proj0",
    )(*args)


def _ln_silu(y, ln_g, ln_b):
    mu = jnp.mean(y, axis=-1, keepdims=True)
    d = y - mu
    var = jnp.mean(d * d, axis=-1, keepdims=True)
    z = d * lax.rsqrt(var + EPS) * ln_g + ln_b
    return z * jax.nn.sigmoid(z)


def _conv0_kernel(u_ref, up_ref, wdw_ref, bdw_ref, lng_ref, lnb_ref, y_ref, buf_ref):
    i = pl.program_id(0)
    tm = u_ref.shape[0]
    buf_ref[0, 0:CONV_HALO, :] = jnp.where(i == 0, 0.0, up_ref[...])
    buf_ref[0, CONV_HALO:, :] = u_ref[...]
    span = tm + CONV_HALO - V7X_SUBLANES
    for r in range(1, V7X_SUBLANES):
        buf_ref[r, 0:span, :] = buf_ref[0, pl.ds(r, span), :]
    first = CONV_HALO - (CONF_CONV_WIDTH - 1)
    for c0 in range(0, tm, CONV_CHUNK):
        acc = jnp.zeros((CONV_CHUNK, CONF_CH), F32)
        for kk in range(CONF_CONV_WIDTH):
            r = (first + kk) % V7X_SUBLANES
            acc = acc + wdw_ref[kk:kk + 1, :] * buf_ref[r, pl.ds(first + kk - r + c0, CONV_CHUNK), :]
        z = _ln_silu(acc + bdw_ref[...], lng_ref[...], lnb_ref[...])
        y_ref[c0:c0 + CONV_CHUNK, :] = z.astype(BF16)


def _conv0_prompt(u, w_dw, b_dw, ln_g, ln_b):
    t = u.shape[0]
    tm = CONV_TILE
    per = tm // CONV_HALO
    return pl.pallas_call(
        _conv0_kernel,
        out_shape=jax.ShapeDtypeStruct((t, CONF_CH), BF16),
        grid=(t // tm,),
        in_specs=[pl.BlockSpec((tm, CONF_CH), lambda i: (i, 0)),
                  pl.BlockSpec((CONV_HALO, CONF_CH), lambda i: (jnp.maximum(i * per - 1, 0), 0)),
                  _const_spec((CONF_CONV_WIDTH, CONF_CH)),
                  _const_spec((1, CONF_CH)), _const_spec((1, CONF_CH)), _const_spec((1, CONF_CH))],
        out_specs=pl.BlockSpec((tm, CONF_CH), lambda i: (i, 0)),
        scratch_shapes=[pltpu.VMEM((V7X_SUBLANES, tm + CONV_HALO, CONF_CH), F32)],
        compiler_params=_params("arbitrary"),
        name="conv0",
    )(u, u, w_dw, b_dw.reshape(1, -1), ln_g.reshape(1, -1), ln_b.reshape(1, -1))


def _conv0s_kernel(st_ref, u_ref, wdw_ref, bdw_ref, lng_ref, lnb_ref, y_ref):
    acc = wdw_ref[CONF_CONV_WIDTH - 1:CONF_CONV_WIDTH, :] * u_ref[...]
    for kk in range(CONF_CONV_WIDTH - 1):
        acc = acc + wdw_ref[kk:kk + 1, :] * st_ref[kk]
    z = _ln_silu(acc + bdw_ref[...], lng_ref[...], lnb_ref[...])
    y_ref[...] = z.astype(BF16)


def _conv0_sample(state_t, u, w_dw, b_dw, ln_g, ln_b):
    b = u.shape[0]
    return pl.pallas_call(
        _conv0s_kernel,
        out_shape=jax.ShapeDtypeStruct((b, CONF_CH), BF16),
        name="conv0s",
    )(state_t, u, w_dw, b_dw.reshape(1, -1), ln_g.reshape(1, -1), ln_b.reshape(1, -1))


def _attn_kernel(g0_ref, qa_ref, ka_ref, vt_ref, o_ref, s_ref, acc_ref):
    i = pl.program_id(1)
    g0 = g0_ref[pl.program_id(0), i]
    tq = qa_ref.shape[1]
    heads = range(ATTN_HEADS)
    qa = [qa_ref[hh * AUG:(hh + 1) * AUG, :] for hh in heads]

    def scores(j):
        start = pl.multiple_of(j * tq, tq)
        return [jnp.dot(ka_ref[pl.ds(start, tq), hh * AUG:(hh + 1) * AUG], qa[hh],
                        preferred_element_type=F32)
                for hh in heads]

    def fold(j, slot, carry, masked):
        out = []
        for hh in heads:
            m_prev, l_prev = carry[hh]
            st = s_ref[slot, hh]
            if masked:
                key = j * tq + lax.broadcasted_iota(jnp.int32, st.shape, 0)
                qry = i * tq + lax.broadcasted_iota(jnp.int32, st.shape, 1)
                st = jnp.where(key <= qry, st, -jnp.inf)
            m_new = jnp.maximum(m_prev, jnp.max(st, axis=0, keepdims=True))
            alpha = jnp.exp2(m_prev - m_new)
            pt = jnp.exp2(st - m_new)
            l_new = alpha * l_prev + jnp.sum(pt, axis=0, keepdims=True)
            vt = vt_ref[jnp.minimum(j, last), hh * HEAD_DIM:(hh + 1) * HEAD_DIM, :]
            acc_ref[hh] = acc_ref[hh] * alpha + jnp.dot(vt, pt.astype(BF16), preferred_element_type=F32)
            out.append((m_new, l_new))
        return tuple(out)

    def step(j, slot, carry, masked):
        nxt = scores(jnp.minimum(j + 1, last))
        carry = fold(j, slot, carry, masked)
        for hh in heads:
            s_ref[1 - slot, hh] = nxt[hh]
        return carry

    def group(gi, carry):
        for u in range(ATTN_UNROLL):
            carry = step(ATTN_UNROLL * gi + u, u % 2, carry, False)
        return carry

    last = vt_ref.shape[0] - 1
    first = scores(ATTN_UNROLL * g0)
    for hh in heads:
        s_ref[0, hh] = first[hh]
    acc_ref[...] = jnp.zeros_like(acc_ref)
    init = tuple((jnp.full((1, tq), -jnp.inf, F32), jnp.zeros((1, tq), F32)) for _ in heads)
    ngroup = i // ATTN_UNROLL
    carry = lax.fori_loop(g0, ngroup, group, init)
    for u in range(ATTN_UNROLL - 1):
        carry = step(ATTN_UNROLL * ngroup + u, u % 2, carry, True)
    carry = fold(ATTN_UNROLL * ngroup + ATTN_UNROLL - 1, (ATTN_UNROLL - 1) % 2, carry, True)
    ot = jnp.concatenate([acc_ref[hh] / carry[hh][1] for hh in heads], axis=0)
    o_ref[...] = ot.T.astype(BF16)


def _first_live_group(stat, nblk):
    per = stat.shape[0]
    nb = nblk // per
    fmax = stat[:, :, 0:nb].transpose(1, 0, 2).reshape(FOX_HEADS, nblk)
    fmin = stat[:, :, nb:2 * nb].transpose(1, 0, 2).reshape(FOX_HEADS, nblk)
    qk = jnp.sqrt(jnp.max(stat[:, :, 2 * nb], axis=0) * jnp.max(stat[:, :, 2 * nb + 1], axis=0))
    bound = 2.0 * NORM_SLACK * qk[:, None, None] + LOG2E * (fmax[:, :, None] - fmin[:, None, :])
    dead = (bound <= -PRUNE_LOG2).reshape(FOX_HEADS // ATTN_HEADS, ATTN_HEADS, nblk, nblk).all(axis=1)
    key_blk = lax.broadcasted_iota(jnp.int32, dead.shape, 2)
    first_live = jnp.min(jnp.where(dead, nblk, key_blk), axis=2)
    return (first_live // ATTN_UNROLL).astype(jnp.int32)


def _attn_prompt(qa, ka, vt, stat):
    t = ka.shape[0]
    tq = ATTN_TILE
    nh = ATTN_HEADS
    nblk = t // tq
    once = pl.Buffered(1)
    grid_spec = pltpu.PrefetchScalarGridSpec(
        num_scalar_prefetch=1,
        grid=(FOX_HEADS // nh, nblk),
        in_specs=[pl.BlockSpec((nh * AUG, tq), lambda p, i, g0: (p, i)),
                  pl.BlockSpec((t, nh * AUG), lambda p, i, g0: (0, p), pipeline_mode=once),
                  pl.BlockSpec((nblk, nh * HEAD_DIM, tq), lambda p, i, g0: (0, p, 0), pipeline_mode=once)],
        out_specs=pl.BlockSpec((tq, nh * HEAD_DIM), lambda p, i, g0: (i, p)),
        scratch_shapes=[pltpu.VMEM((2, nh, tq, tq), F32),
                        pltpu.VMEM((nh, HEAD_DIM, tq), F32)])
    return pl.pallas_call(
        _attn_kernel,
        out_shape=jax.ShapeDtypeStruct((t, FOX_WIDTH), BF16),
        grid_spec=grid_spec,
        compiler_params=_params("arbitrary", "arbitrary"),
        name="attn_prompt",
    )(_first_live_group(stat, nblk), qa, ka, vt)


def _post_decode_kernel(pt_ref, x_ref, at_ref, wa_ref, yc_ref, wc_ref, gm_ref, shf_ref, scf_ref, gf_ref,
                        g_ref, wup_ref, wdn_ref, q_ref, kn_ref, vn_ref, lfn_ref, *rest):
    del pt_ref
    npg = PAGES_PER_STEP
    k_refs, v_refs, lf_refs = rest[:npg], rest[npg:2 * npg], rest[2 * npg:3 * npg]
    o_ref, od_ref, h_ref, x1_ref, mlp_ref = rest[3 * npg:3 * npg + 5]
    dec = rest[3 * npg + 5:]
    s = pl.program_id(1)
    nsteps = pl.num_programs(1)

    @pl.when(s == 0)
    def _():
        mix = (jnp.dot(at_ref[...], wa_ref[...], preferred_element_type=F32)
               + jnp.dot(yc_ref[...], wc_ref[...], preferred_element_type=F32))
        x1 = x_ref[...] + gm_ref[...] * mix
        x1_ref[...] = x1
        h_ref[...] = _modnorm(x1, g_ref[...], shf_ref[...], scf_ref[...]).astype(BF16)
        mlp_ref[...] = jnp.zeros_like(mlp_ref)
        _decode_init(q_ref, kn_ref, vn_ref, lfn_ref, *dec)

    a = jnp.maximum(jnp.dot(h_ref[...], wup_ref[s], preferred_element_type=F32), 0.0)
    mlp_ref[...] += jnp.dot((a * a).astype(BF16), wdn_ref[s], preferred_element_type=F32)
    _decode_pages(k_refs, v_refs, lf_refs, *dec)

    @pl.when(s == nsteps - 1)
    def _():
        o_ref[...] = x1_ref[...] + gf_ref[...] * mlp_ref[...]
        _decode_final(od_ref, dec[3], dec[5])


def _rows_to_tile(vals):
    row = lax.broadcasted_iota(jnp.int32, (FOX_HEADS, V7X_LANES), 0)
    out = jnp.zeros((FOX_HEADS, V7X_LANES), F32)
    for h in range(FOX_HEADS):
        out = jnp.where(row == h, jnp.broadcast_to(vals[h], (FOX_HEADS, V7X_LANES)), out)
    return out


def _decode_init(q_ref, kn_ref, vn_ref, lfn_ref, qb_ref, tri_ref, m_ref, l_ref, run_ref, acc_ref):
    lanes = V7X_LANES
    lane = lax.broadcasted_iota(jnp.int32, (1, lanes), 1)
    tr = lax.broadcasted_iota(jnp.int32, (PAGE_SIZE, 2 * lanes), 0)
    tc = lax.broadcasted_iota(jnp.int32, (PAGE_SIZE, 2 * lanes), 1)
    tri_ref[...] = ((tr > tc) | (tc >= lanes)).astype(BF16)
    run_ref[...] = jnp.broadcast_to(lfn_ref[0], (FOX_HEADS, lanes))
    s_new = []
    for h in range(FOX_HEADS):
        qb_ref[h] = jnp.broadcast_to(q_ref[0, h], (HEAD_DIM, lanes))
        s_new.append(jnp.sum(q_ref[0, h] * kn_ref[0, h], axis=0, keepdims=True))
        acc_ref[h] = jnp.where(lane == 0, jnp.broadcast_to(vn_ref[0, h], (HEAD_DIM, lanes)), 0.0)
    m_ref[...] = _rows_to_tile([jnp.broadcast_to(s, (1, lanes)) for s in s_new])
    l_ref[...] = jnp.ones_like(l_ref)


def _decode_final(o_ref, l_ref, acc_ref):
    for h in range(FOX_HEADS):
        o_ref[0, h] = jnp.sum(acc_ref[h], axis=1, keepdims=True) / l_ref[h:h + 1, 0:1]


def _decode_pages(k_refs, v_refs, lf_refs, qb_ref, tri_ref, m_ref, l_ref, run_ref, acc_ref):
    npg = len(k_refs)
    lanes = V7X_LANES
    heads = range(FOX_HEADS)
    rows_to_tile = _rows_to_tile

    parts = jnp.concatenate([p for i in range(npg) for p in _split3(lf_refs[i][0])], axis=0)
    cs = jnp.dot(parts.astype(BF16), tri_ref[...], preferred_element_type=F32)
    run = run_ref[...]
    scores = []
    for i in range(npg):
        c3 = cs[24 * i:24 * i + 8] + cs[24 * i + 8:24 * i + 16] + cs[24 * i + 16:24 * i + 24]
        qk = rows_to_tile([jnp.sum(k_refs[i][0, h] * qb_ref[h], axis=0, keepdims=True) for h in heads])
        scores.append(qk + (c3[:, :lanes] + run) * LOG2E)
        run = run + c3[:, lanes:]
    run_ref[...] = run

    m_prev = m_ref[...]
    m_blk = scores[0]
    for s in scores[1:]:
        m_blk = jnp.maximum(m_blk, s)
    m_new = jnp.maximum(m_prev, jnp.max(m_blk, axis=1, keepdims=True))
    alpha = jnp.exp2(m_prev - m_new)
    ps = [jnp.exp2(s - m_new) for s in scores]
    p_tot = ps[0]
    for p in ps[1:]:
        p_tot = p_tot + p
    l_ref[...] = alpha * l_ref[...] + jnp.sum(p_tot, axis=1, keepdims=True)
    m_ref[...] = m_new
    for h in heads:
        acc = acc_ref[h] * alpha[h:h + 1, :]
        for i in range(npg):
            acc = acc + v_refs[i][0, h] * ps[i][h:h + 1, :]
        acc_ref[h] = acc


def _post_decode(x, at, w_at, yc, w_yc, mod, g_mlp, w_up, w_down,
                 q, k_new, v_new, lf_new, cache_k, cache_v, cache_logf, page_table):
    t = x.shape[0]
    tm = ROW_TILE
    b = q.shape[0]
    n_pages = page_table.shape[1]
    npg = PAGES_PER_STEP
    nsteps = n_pages // npg
    assert t // tm == b, "one prompt row tile per sample sequence"
    fch = D_FF // nsteps
    col = (FOX_HEADS, HEAD_DIM, 1)
    w_up_c = w_up.reshape(D_MODEL, nsteps, fch).transpose(1, 0, 2)
    w_dn_c = w_down.reshape(nsteps, fch, D_MODEL)
    ck = cache_k.transpose(0, 2, 3, 1)
    cv = cache_v.transpose(0, 2, 3, 1)
    clf = cache_logf.transpose(0, 2, 1)

    def page_map(nd):
        def for_slot(slot):
            return lambda bi, g, pt: (pt[bi, n_pages - 1 - (g * npg + slot)],) + (0,) * (nd - 1)
        return for_slot

    def const(shape):
        nd = len(shape)
        return pl.BlockSpec(shape, lambda bi, g, pt: (0,) * nd, pipeline_mode=pl.Buffered(1))

    rows = lambda bi, g, pt: (bi, 0)
    mcol = lambda c: (lambda bi, g, pt: (0, c))
    seq4 = lambda bi, g, pt: (bi, 0, 0, 0)
    page_blk = (1, FOX_HEADS, HEAD_DIM, PAGE_SIZE)
    in_specs = [pl.BlockSpec((tm, D_MODEL), rows),
                pl.BlockSpec((tm, FOX_WIDTH), rows), const(w_at.shape),
                pl.BlockSpec((tm, CONF_CH), rows), const(w_yc.shape),
                pl.BlockSpec((1, D_MODEL), mcol(2)), pl.BlockSpec((1, D_MODEL), mcol(3)),
                pl.BlockSpec((1, D_MODEL), mcol(4)), pl.BlockSpec((1, D_MODEL), mcol(5)),
                const((1, D_MODEL)), const(w_up_c.shape), const(w_dn_c.shape)]
    in_specs += [pl.BlockSpec((1,) + col, seq4)] * 3 + [pl.BlockSpec((1, FOX_HEADS, 1), lambda bi, g, pt: (bi, 0, 0))]
    in_specs += [pl.BlockSpec(page_blk, page_map(4)(s)) for s in range(npg)]
    in_specs += [pl.BlockSpec(page_blk, page_map(4)(s)) for s in range(npg)]
    in_specs += [pl.BlockSpec((1, FOX_HEADS, PAGE_SIZE), page_map(3)(s)) for s in range(npg)]
    grid_spec = pltpu.PrefetchScalarGridSpec(
        num_scalar_prefetch=1,
        grid=(b, nsteps),
        in_specs=in_specs,
        out_specs=[pl.BlockSpec((tm, D_MODEL), rows), pl.BlockSpec((1,) + col, seq4)],
        scratch_shapes=[pltpu.VMEM((tm, D_MODEL), BF16),
                        pltpu.VMEM((tm, D_MODEL), F32),
                        pltpu.VMEM((tm, D_MODEL), F32),
                        pltpu.VMEM((FOX_HEADS, HEAD_DIM, V7X_LANES), F32),
                        pltpu.VMEM((PAGE_SIZE, 2 * V7X_LANES), BF16),
                        pltpu.VMEM((FOX_HEADS, V7X_LANES), F32),
                        pltpu.VMEM((FOX_HEADS, V7X_LANES), F32),
                        pltpu.VMEM((FOX_HEADS, V7X_LANES), F32),
                        pltpu.VMEM((FOX_HEADS, HEAD_DIM, V7X_LANES), F32)])
    y, dec = pl.pallas_call(
        _post_decode_kernel,
        out_shape=[jax.ShapeDtypeStruct((t, D_MODEL), F32), jax.ShapeDtypeStruct((b,) + col, F32)],
        grid_spec=grid_spec,
        compiler_params=pltpu.CompilerParams(dimension_semantics=("arbitrary", "arbitrary"),
                                             vmem_limit_bytes=FUSED_VMEM_LIMIT),
        name="post_decode",
    )(page_table, x, at, w_at, yc, w_yc, mod, mod, mod, mod, g_mlp, w_up_c, w_dn_c,
      q.reshape((b,) + col), k_new.reshape((b,) + col), v_new.reshape((b,) + col),
      lf_new.reshape(b, FOX_HEADS, 1), *([ck] * npg), *([cv] * npg), *([clf] * npg))
    return y, dec.reshape(b, FOX_WIDTH).astype(BF16)


def _post_kernel(*refs, n_mix, final):
    x_ref = refs[0]
    mix_refs = refs[1:1 + 2 * n_mix]
    gm_ref, shf_ref, scf_ref, gf_ref, g_ref, wup_ref, wdn_ref = refs[1 + 2 * n_mix:8 + 2 * n_mix]
    rest = refs[8 + 2 * n_mix:]
    if final:
        gfin_ref, o_ref = rest
    else:
        (o_ref,) = rest
    mix = None
    for a in range(n_mix):
        t = jnp.dot(mix_refs[2 * a][...], mix_refs[2 * a + 1][...], preferred_element_type=F32)
        mix = t if mix is None else mix + t
    x1 = x_ref[...] + gm_ref[...] * mix
    h = _modnorm(x1, g_ref[...], shf_ref[...], scf_ref[...]).astype(BF16)
    fchunk = 1024
    acc = None
    for f0 in range(0, D_FF, fchunk):
        a = jnp.maximum(jnp.dot(h, wup_ref[:, f0:f0 + fchunk], preferred_element_type=F32), 0.0)
        t = jnp.dot((a * a).astype(BF16), wdn_ref[f0:f0 + fchunk, :], preferred_element_type=F32)
        acc = t if acc is None else acc + t
    x2 = x1 + gf_ref[...] * acc
    if final:
        ms = jnp.mean(x2 * x2, axis=-1, keepdims=True)
        x2 = x2 * lax.rsqrt(ms + EPS) * gfin_ref[...]
    o_ref[...] = x2


def _post(x, mixes, mod, g_mlp, w_up, w_down, final_g=None):
    t = x.shape[0]
    tm = min(ROW_TILE, t)
    mm = mod.shape[0]
    bm = 1 if mm == 1 else tm
    row = lambda i: (i, 0)

    def mcol(c):
        return (lambda i: (0, c)) if mm == 1 else (lambda i: (i, c))

    in_specs = [pl.BlockSpec((tm, D_MODEL), row)]
    args = [x]
    for a, w in mixes:
        in_specs += [pl.BlockSpec((tm, a.shape[1]), row), _const_spec(w.shape)]
        args += [a, w]
    in_specs += [pl.BlockSpec((bm, D_MODEL), mcol(2)), pl.BlockSpec((bm, D_MODEL), mcol(3)),
                 pl.BlockSpec((bm, D_MODEL), mcol(4)), pl.BlockSpec((bm, D_MODEL), mcol(5)),
                 _const_spec((1, D_MODEL)), _const_spec(w_up.shape), _const_spec(w_down.shape)]
    args += [mod, mod, mod, mod, g_mlp, w_up, w_down]
    if final_g is not None:
        in_specs.append(_const_spec((1, D_MODEL)))
        args.append(final_g)
    return pl.pallas_call(
        functools.partial(_post_kernel, n_mix=len(mixes), final=final_g is not None),
        out_shape=jax.ShapeDtypeStruct((t, D_MODEL), F32),
        grid=(t // tm,),
        in_specs=in_specs,
        out_specs=pl.BlockSpec((tm, D_MODEL), row),
        compiler_params=_params("arbitrary"),
        name="post",
    )(*args)


def _mix1_kernel(x_ref, sh_ref, sc_ref, g_ref, win_ref, wdw_ref, *rest, per_row_state):
    x = x_ref[...]
    tm = x.shape[0]
    h = _modnorm(x, g_ref[...], sh_ref[...], sc_ref[...]).astype(BF16)
    pad = V7X_SUBLANES
    if not per_row_state:
        by_ref, tail_ref, buf_ref = rest

        @pl.when(pl.program_id(0) == 0)
        def _():
            buf_ref[0:pad, :] = jnp.zeros((pad, D_MODEL), F32)
    else:
        s0_ref, s1_ref, by_ref, cx_ref = rest

    for c0 in range(0, D_MODEL, MIX_CHUNK):
        cols = slice(c0, c0 + MIX_CHUNK)

        def proj(part):
            w = win_ref[:, part * D_MODEL + c0:part * D_MODEL + c0 + MIX_CHUNK]
            return jnp.dot(h, w, preferred_element_type=F32)

        b_gate = proj(0)
        cx = proj(1) * proj(2)
        w0, w1, w2 = wdw_ref[0:1, cols], wdw_ref[1:2, cols], wdw_ref[2:3, cols]
        if per_row_state:
            y = w0 * s0_ref[:, cols] + w1 * s1_ref[:, cols] + w2 * cx
            cx_ref[:, cols] = cx
        else:
            buf_ref[pad:, cols] = cx
            y = w0 * buf_ref[pl.ds(pad - 2, tm), cols] + w1 * buf_ref[pl.ds(pad - 1, tm), cols] + w2 * cx
            tail = cx[tm - pad:, :]
            buf_ref[0:pad, cols] = tail
            tail_ref[:, cols] = tail
        by_ref[:, cols] = (b_gate * y).astype(BF16)


def _mix1(x, mod, g, w_in, w_dw, state=None):
    t = x.shape[0]
    tm = min(ROW_TILE, t)
    mm = mod.shape[0]
    bm = 1 if mm == 1 else tm
    row = lambda i: (i, 0)
    mcol = lambda c: (lambda i: (0, c)) if mm == 1 else (lambda i: (i, c))
    in_specs = [pl.BlockSpec((tm, D_MODEL), row),
                pl.BlockSpec((bm, D_MODEL), mcol(0)), pl.BlockSpec((bm, D_MODEL), mcol(1)),
                _const_spec((1, D_MODEL)), _const_spec(w_in.shape), _const_spec(w_dw.shape)]
    args = [x, mod, mod, g, w_in, w_dw]
    if state is not None:
        in_specs += [pl.BlockSpec((tm, D_MODEL), row)] * 2
        args += [state[0], state[1]]
        out_shape = [jax.ShapeDtypeStruct((t, D_MODEL), BF16), jax.ShapeDtypeStruct((t, D_MODEL), F32)]
        out_specs = [pl.BlockSpec((tm, D_MODEL), row)] * 2
        scratch = []
    else:
        nt = t // tm
        out_shape = [jax.ShapeDtypeStruct((t, D_MODEL), BF16),
                     jax.ShapeDtypeStruct((nt * V7X_SUBLANES, D_MODEL), F32)]
        out_specs = [pl.BlockSpec((tm, D_MODEL), row), pl.BlockSpec((V7X_SUBLANES, D_MODEL), row)]
        scratch = [pltpu.VMEM((tm + V7X_SUBLANES, D_MODEL), F32)]
    return pl.pallas_call(
        functools.partial(_mix1_kernel, per_row_state=state is not None),
        out_shape=out_shape,
        grid=(t // tm,),
        in_specs=in_specs,
        out_specs=out_specs,
        scratch_shapes=scratch,
        compiler_params=_params("arbitrary"),
        name="mix1",
    )(*args)


def kernel(x_prompt, x_sample, cache_k, cache_v, cache_logf, state_conformer_conv, state_short_conv, page_table, c_prompt, c_sample, l0_w_ada, l0_b_ada, l0_norm_mix, l0_norm_mlp, l0_w_in, l0_b_forget, l0_w_dw, l0_b_dw, l0_conv_ln_g, l0_conv_ln_b, l0_w_out, l0_w_up, l0_w_down, l1_w_ada, l1_b_ada, l1_norm_mix, l1_norm_mlp, l1_w_in, l1_w_dw, l1_w_out, l1_w_up, l1_w_down, final_norm):
    bp, t, d = x_prompt.shape
    bs = x_sample.shape[0]
    assert bp == 1 and d == D_MODEL and x_sample.shape[1] == 1

    def per_head_padded(wcols, scale):
        w3 = (wcols * scale).reshape(d, FOX_HEADS, HEAD_DIM)
        return jnp.pad(w3, ((0, 0), (0, 0), (0, AUG - HEAD_DIM))).reshape(d, FOX_HEADS * AUG).astype(BF16)

    w0 = {
        "q": per_head_padded(l0_w_in[:, :FOX_WIDTH], ATTN_SCALE * LOG2E),
        "k": per_head_padded(l0_w_in[:, FOX_WIDTH:2 * FOX_WIDTH], 1.0),
        "kv": l0_w_in[:, FOX_WIDTH:3 * FOX_WIDTH].astype(BF16),
        "f":jnp.pad(l0_w_in[:, 3 * FOX_WIDTH:3 * FOX_WIDTH + FOX_HEADS],
                     ((0, 0), (0, V7X_LANES - FOX_HEADS))).astype(BF16),
        "bf": jnp.pad(l0_b_forget, (0, V7X_LANES - FOX_HEADS)).reshape(1, V7X_LANES),
        "glu": l0_w_in[:, 3 * FOX_WIDTH + FOX_HEADS:].astype(BF16),
    }
    w0_out_attn = l0_w_out[:FOX_WIDTH].astype(BF16)
    w0_out_conv = l0_w_out[FOX_WIDTH:].astype(BF16)
    w0_up, w0_down = l0_w_up.astype(BF16), l0_w_down.astype(BF16)
    w1_in, w1_out = l1_w_in.astype(BF16), l1_w_out.astype(BF16)
    w1_up, w1_down = l1_w_up.astype(BF16), l1_w_down.astype(BF16)
    g0_mix, g0_mlp = l0_norm_mix.reshape(1, d), l0_norm_mlp.reshape(1, d)
    g1_mix, g1_mlp = l1_norm_mix.reshape(1, d), l1_norm_mlp.reshape(1, d)
    gfin = final_norm.reshape(1, d)

    c_all = jnp.concatenate([c_prompt, c_sample], axis=0)
    mod0 = _ada(c_all, l0_w_ada, l0_b_ada)
    mod1 = _ada(c_all, l1_w_ada, l1_b_ada)
    mod0_p, mod0_s = mod0[:1], mod0[1:]
    mod1_p, mod1_s = mod1[:1], mod1[1:]

    xp = x_prompt.reshape(t, d)
    qa_p, ka_p, vt_p, st_p, k_p, v_p, lf_p, u_p = _proj0(xp, mod0_p, g0_mix, w0, True)
    yc_p = _conv0_prompt(u_p, l0_w_dw, l0_b_dw, l0_conv_ln_g, l0_conv_ln_b)
    at_p = _attn_prompt(qa_p, ka_p, vt_p, st_p)

    xs = x_sample.reshape(bs, d)
    qp_s, k_s, v_s, lf_s, u_s = _proj0(xs, mod0_s, g0_mix, w0, False)
    q_s = qp_s.reshape(bs, FOX_HEADS, AUG)[:, :, :HEAD_DIM].reshape(bs, FOX_WIDTH)
    yc_s = _conv0_sample(state_conformer_conv.transpose(1, 0, 2), u_s,
                         l0_w_dw, l0_b_dw, l0_conv_ln_g, l0_conv_ln_b)

    x1_p, at_s = _post_decode(xp, at_p, w0_out_attn, yc_p, w0_out_conv, mod0_p, g0_mlp, w0_up, w0_down,
                              q_s, k_s, v_s, lf_s, cache_k, cache_v, cache_logf, page_table)

    by_p, tail_p = _mix1(x1_p, mod1_p, g1_mix, w1_in, l1_w_dw)
    y_p = _post(x1_p, [(by_p, w1_out)], mod1_p, g1_mlp, w1_up, w1_down, gfin)

    x1_s = _post(xs, [(at_s, w0_out_attn), (yc_s, w0_out_conv)], mod0_s, g0_mlp, w0_up, w0_down)
    by_s, cx_s = _mix1(x1_s, mod1_s, g1_mix, w1_in, l1_w_dw,
                       state=(state_short_conv[:, 0], state_short_conv[:, 1]))
    y_s = _post(x1_s, [(by_s, w1_out)], mod1_s, g1_mlp, w1_up, w1_down, gfin)

    hs = (FOX_HEADS, HEAD_DIM)
    return (y_p.reshape(1, t, d), y_s.reshape(bs, 1, d),
            k_p.reshape(1, t, *hs), v_p.reshape(1, t, *hs), lf_p.reshape(1, t, FOX_HEADS),
            u_p[t - (CONF_CONV_WIDTH - 1):].reshape(1, CONF_CONV_WIDTH - 1, CONF_CH),
            tail_p[-(SC_CONV_WIDTH - 1):].reshape(1, SC_CONV_WIDTH - 1, d),
            k_s.reshape(bs, 1, *hs), v_s.reshape(bs, 1, *hs), lf_s.reshape(bs, 1, FOX_HEADS),
            jnp.concatenate([state_conformer_conv[:, 1:], u_s[:, None, :]], axis=1),
            jnp.stack([state_short_conv[:, 1], cx_s], axis=1))
```

```python
import functools

import jax
import jax.numpy as jnp
import numpy as np
from jax import lax
from jax.experimental import pallas as pl
from jax.experimental.pallas import tpu as pltpu

F32 = jnp.float32
BF16 = jnp.bfloat16

D_MODEL = 1024
FOX_HEADS = 8
HEAD_DIM = 64
FOX_WIDTH = FOX_HEADS * HEAD_DIM
CONF_CH = D_MODEL - FOX_WIDTH
CONF_CONV_WIDTH = 31
SC_CONV_WIDTH = 3
D_FF = 4 * D_MODEL
PAGE_SIZE = 128
EPS = 1e-6
ATTN_SCALE = HEAD_DIM ** -0.5
LOG2E = 1.4426950408889634

V7X_LANES = 128
V7X_SUBLANES = 8
V7X_VMEM_BYTES = 64 * 1024 * 1024
VMEM_LIMIT = V7X_VMEM_BYTES - 12 * 1024 * 1024
FUSED_VMEM_LIMIT = V7X_VMEM_BYTES - 4 * 1024 * 1024

ROW_TILE = 512
CONV_TILE = 256
CONV_CHUNK = 64
CONV_HALO = 32
ATTN_TILE = 256
AUG = V7X_LANES
ATTN_HEADS = 4
ATTN_UNROLL = 2
PRUNE_LOG2 = 136.0
MIX_CHUNK = 256
NORM_SLACK = 1.02
PAGES_PER_STEP = 16


def _const_spec(shape):
    nd = len(shape)
    return pl.BlockSpec(shape, lambda *_: (0,) * nd, pipeline_mode=pl.Buffered(1))


def _params(*sem):
    return pltpu.CompilerParams(dimension_semantics=sem, vmem_limit_bytes=VMEM_LIMIT)


def _modnorm(x, g, shift, scale):
    ms = jnp.mean(x * x, axis=-1, keepdims=True)
    y = x * lax.rsqrt(ms + EPS)
    return (y * g) * (1.0 + scale) + shift


def _split3(x):
    hi = x.astype(BF16).astype(F32)
    r = x - hi
    mid = r.astype(BF16).astype(F32)
    lo = (r - mid).astype(BF16).astype(F32)
    return hi, mid, lo


def _ada_kernel(c_ref, w_ref, b_ref, o_ref):
    c = c_ref[...]
    s = (c * jax.nn.sigmoid(c)).astype(BF16)
    o_ref[...] = jnp.dot(s, w_ref[...].astype(BF16), preferred_element_type=F32) + b_ref[...]


def _ada(c_all, w_ada, b_ada):
    m = c_all.shape[0]
    n = w_ada.shape[1]
    bn = 1536
    return pl.pallas_call(
        _ada_kernel,
        out_shape=jax.ShapeDtypeStruct((m, n), F32),
        grid=(n // bn,),
        in_specs=[pl.BlockSpec((m, D_MODEL), lambda j: (0, 0)),
                  pl.BlockSpec((D_MODEL, bn), lambda j: (0, j)),
                  pl.BlockSpec((1, bn), lambda j: (0, j))],
        out_specs=pl.BlockSpec((m, bn), lambda j: (0, j)),
        compiler_params=_params("arbitrary"),
        name="ada",
    )(c_all, w_ada, b_ada.reshape(1, n))


def _bias_placement():
    pq = np.zeros((V7X_LANES, FOX_HEADS * AUG), np.float32)
    pk = np.zeros((V7X_LANES, FOX_HEADS * AUG), np.float32)
    for h in range(FOX_HEADS):
        for p in range(3):
            pq[p * FOX_HEADS + h, h * AUG + HEAD_DIM + p] = 1.0
            pk[p * FOX_HEADS + h, h * AUG + HEAD_DIM + 3 + p] = -1.0
            pq[3 * FOX_HEADS, h * AUG + HEAD_DIM + 3 + p] = 1.0
            pk[3 * FOX_HEADS, h * AUG + HEAD_DIM + p] = 1.0
    return jnp.asarray(pq, BF16), jnp.asarray(pk, BF16)


def _proj0_kernel(*refs, prompt):
    x_ref, sh_ref, sc_ref, g_ref, wq_ref, wkv_ref, wf_ref, bf_ref, wglu_ref = refs[:9]
    tm = x_ref.shape[0]
    if prompt:
        (wk_ref, pq_ref, pk_ref, qa_ref, ka_ref, vt_ref, st_ref, k_ref, v_ref, lf_ref, u_ref,
         tri_ref, carry_ref) = refs[9:]
        tk = vt_ref.shape[2]

        @pl.when(pl.program_id(0) == 0)
        def _():
            carry_ref[...] = jnp.zeros_like(carry_ref)
            r = lax.broadcasted_iota(jnp.int32, (tm, tm), 0)
            c = lax.broadcasted_iota(jnp.int32, (tm, tm), 1)
            tri_ref[...] = (r <= c).astype(BF16)
    else:
        q_ref, k_ref, v_ref, lf_ref, u_ref = refs[9:]

    x = x_ref[...]
    h = _modnorm(x, g_ref[...], sh_ref[...], sc_ref[...]).astype(BF16)
    fg = jnp.dot(h, wf_ref[...], preferred_element_type=F32) + bf_ref[...]
    lf = jnp.minimum(fg, 0.0) - jnp.log1p(jnp.exp(-jnp.abs(fg)))
    if prompt:
        lft = lf.T[:FOX_HEADS]
        parts = jnp.concatenate(_split3(lft), axis=0).astype(BF16)
        cs = jnp.dot(parts, tri_ref[...], preferred_element_type=F32)
        ft = cs[0:8] + cs[8:16] + cs[16:24] + carry_ref[...]
        carry_ref[...] = ft[:, tm - 1:tm]
        pieces_t = jnp.concatenate(
            list(_split3(ft * LOG2E)) + [jnp.ones((FOX_HEADS, tm), F32),
                                 jnp.zeros((V7X_LANES - 4 * FOX_HEADS, tm), F32)], axis=0)
        pieces = pieces_t.T.astype(BF16)

    kv = jnp.dot(h, wkv_ref[...], preferred_element_type=F32)
    k = kv[:, :FOX_WIDTH]
    v = kv[:, FOX_WIDTH:]
    glu = jnp.dot(h, wglu_ref[...], preferred_element_type=F32)
    u = glu[:, :CONF_CH] * jax.nn.sigmoid(glu[:, CONF_CH:])
    qp = jnp.dot(h, wq_ref[...], preferred_element_type=F32)

    if not prompt:
        q_ref[...] = qp
    else:
        qpt = (qp + jnp.dot(pieces, pq_ref[...], preferred_element_type=F32)).T
        qa_ref[...] = qpt.astype(BF16)
        kp = jnp.dot(h, wk_ref[...], preferred_element_type=F32)
        ka_ref[...] = (kp + jnp.dot(pieces, pk_ref[...], preferred_element_type=F32)).astype(BF16)
        vtt = v.T.astype(BF16)
        for c0 in range(tm // tk):
            vt_ref[c0] = vtt[:, c0 * tk:(c0 + 1) * tk]

        nb = tm // tk
        row = lax.broadcasted_iota(jnp.int32, (FOX_HEADS, V7X_LANES), 0)
        lane = lax.broadcasted_iota(jnp.int32, (FOX_HEADS, V7X_LANES), 1)
        stat = jnp.zeros((FOX_HEADS, V7X_LANES), F32)
        for c0 in range(nb):
            blk = ft[:, c0 * tk:(c0 + 1) * tk]
            stat = jnp.where(lane == c0, jnp.max(blk, axis=1, keepdims=True), stat)
            stat = jnp.where(lane == nb + c0, jnp.min(blk, axis=1, keepdims=True), stat)
        for hh in range(FOX_HEADS):
            qh = qpt[hh * AUG:hh * AUG + HEAD_DIM, :]
            qn2 = jnp.max(jnp.sum(qh * qh, axis=0, keepdims=True), axis=1, keepdims=True)
            kh = kp[:, hh * AUG:(hh + 1) * AUG]
            kn2 = jnp.max(jnp.sum(kh * kh, axis=1, keepdims=True), axis=0, keepdims=True)
            stat = jnp.where((row == hh) & (lane == 2 * nb), qn2, stat)
            stat = jnp.where((row == hh) & (lane == 2 * nb + 1), kn2, stat)
        st_ref[0] = stat

    k_ref[...] = k
    v_ref[...] = v
    lf_ref[...] = lf[:, :FOX_HEADS]
    u_ref[...] = u


def _proj0(x, mod, g, w, prompt):
    t = x.shape[0]
    tm = min(ROW_TILE, t)
    mm = mod.shape[0]
    mrow = (lambda i: (0, 0)) if mm == 1 else (lambda i: (i, 0))
    mrow1 = (lambda i: (0, 1)) if mm == 1 else (lambda i: (i, 1))
    bm = 1 if mm == 1 else tm
    row = lambda i: (i, 0)
    wide = FOX_HEADS * AUG
    in_specs = [pl.BlockSpec((tm, D_MODEL), row),
                pl.BlockSpec((bm, D_MODEL), mrow),
                pl.BlockSpec((bm, D_MODEL), mrow1),
                _const_spec((1, D_MODEL)),
                _const_spec(w["q"].shape),
                _const_spec(w["kv"].shape),
                _const_spec(w["f"].shape),
                _const_spec((1, V7X_LANES)),
                _const_spec(w["glu"].shape)]
    args = [x, mod, mod, g, w["q"], w["kv"], w["f"], w["bf"], w["glu"]]
    tail_shape = [jax.ShapeDtypeStruct((t, FOX_WIDTH), F32), jax.ShapeDtypeStruct((t, FOX_WIDTH), F32),
                  jax.ShapeDtypeStruct((t, FOX_HEADS), F32), jax.ShapeDtypeStruct((t, CONF_CH), F32)]
    tail_specs = [pl.BlockSpec((tm, FOX_WIDTH), row), pl.BlockSpec((tm, FOX_WIDTH), row),
                  pl.BlockSpec((tm, FOX_HEADS), row), pl.BlockSpec((tm, CONF_CH), row)]
    if prompt:
        tk = ATTN_TILE
        pq, pk = _bias_placement()
        in_specs += [_const_spec(w["k"].shape), _const_spec(pq.shape), _const_spec(pk.shape)]
        args += [w["k"], pq, pk]
        out_shape = [jax.ShapeDtypeStruct((wide, t), BF16), jax.ShapeDtypeStruct((t, wide), BF16),
                     jax.ShapeDtypeStruct((t // tk, FOX_WIDTH, tk), BF16),
                     jax.ShapeDtypeStruct((t // tm, FOX_HEADS, V7X_LANES), F32)] + tail_shape
        out_specs = [pl.BlockSpec((wide, tm), lambda i: (0, i)), pl.BlockSpec((tm, wide), row),
                     pl.BlockSpec((tm // tk, FOX_WIDTH, tk), lambda i: (i, 0, 0)),
                     pl.BlockSpec((1, FOX_HEADS, V7X_LANES), lambda i: (i, 0, 0))] + tail_specs
        scratch = [pltpu.VMEM((tm, tm), BF16), pltpu.VMEM((FOX_HEADS, 1), F32)]
    else:
        out_shape = [jax.ShapeDtypeStruct((t, wide), F32)] + tail_shape
        out_specs = [pl.BlockSpec((tm, wide), row)] + tail_specs
        scratch = []
    return pl.pallas_call(
        functools.partial(_proj0_kernel, prompt=prompt),
        out_shape=out_shape,
        grid=(t // tm,),
        in_specs=in_specs,
        out_specs=out_specs,
        scratch_shapes=scratch,
        compiler_params=_params("arbitrary"),
        name="proj0",
    )(*args)


def _ln_silu(y, ln_g, ln_b):
    mu = jnp.mean(y, axis=-1, keepdims=True)
    d = y - mu
    var = jnp.mean(d * d, axis=-1, keepdims=True)
    z = d * lax.rsqrt(var + EPS) * ln_g + ln_b
    return z * jax.nn.sigmoid(z)


def _conv0_kernel(u_ref, up_ref, wdw_ref, bdw_ref, lng_ref, lnb_ref, y_ref, buf_ref):
    i = pl.program_id(0)
    tm = u_ref.shape[0]
    buf_ref[0, 0:CONV_HALO, :] = jnp.where(i == 0, 0.0, up_ref[...])
    buf_ref[0, CONV_HALO:, :] = u_ref[...]
    span = tm + CONV_HALO - V7X_SUBLANES
    for r in range(1, V7X_SUBLANES):
        buf_ref[r, 0:span, :] = buf_ref[0, pl.ds(r, span), :]
    first = CONV_HALO - (CONF_CONV_WIDTH - 1)
    for c0 in range(0, tm, CONV_CHUNK):
        acc = jnp.zeros((CONV_CHUNK, CONF_CH), F32)
        for kk in range(CONF_CONV_WIDTH):
            r = (first + kk) % V7X_SUBLANES
            acc = acc + wdw_ref[kk:kk + 1, :] * buf_ref[r, pl.ds(first + kk - r + c0, CONV_CHUNK), :]
        z = _ln_silu(acc + bdw_ref[...], lng_ref[...], lnb_ref[...])
        y_ref[c0:c0 + CONV_CHUNK, :] = z.astype(BF16)


def _conv0_prompt(u, w_dw, b_dw, ln_g, ln_b):
    t = u.shape[0]
    tm = CONV_TILE
    per = tm // CONV_HALO
    return pl.pallas_call(
        _conv0_kernel,
        out_shape=jax.ShapeDtypeStruct((t, CONF_CH), BF16),
        grid=(t // tm,),
        in_specs=[pl.BlockSpec((tm, CONF_CH), lambda i: (i, 0)),
                  pl.BlockSpec((CONV_HALO, CONF_CH), lambda i: (jnp.maximum(i * per - 1, 0), 0)),
                  _const_spec((CONF_CONV_WIDTH, CONF_CH)),
                  _const_spec((1, CONF_CH)), _const_spec((1, CONF_CH)), _const_spec((1, CONF_CH))],
        out_specs=pl.BlockSpec((tm, CONF_CH), lambda i: (i, 0)),
        scratch_shapes=[pltpu.VMEM((V7X_SUBLANES, tm + CONV_HALO, CONF_CH), F32)],
        compiler_params=_params("arbitrary"),
        name="conv0",
    )(u, u, w_dw, b_dw.reshape(1, -1), ln_g.reshape(1, -1), ln_b.reshape(1, -1))


def _conv0s_kernel(st_ref, u_ref, wdw_ref, bdw_ref, lng_ref, lnb_ref, y_ref):
    acc = wdw_ref[CONF_CONV_WIDTH - 1:CONF_CONV_WIDTH, :] * u_ref[...]
    for kk in range(CONF_CONV_WIDTH - 1):
        acc = acc + wdw_ref[kk:kk + 1, :] * st_ref[kk]
    z = _ln_silu(acc + bdw_ref[...], lng_ref[...], lnb_ref[...])
    y_ref[...] = z.astype(BF16)


def _conv0_sample(state_t, u, w_dw, b_dw, ln_g, ln_b):
    b = u.shape[0]
    return pl.pallas_call(
        _conv0s_kernel,
        out_shape=jax.ShapeDtypeStruct((b, CONF_CH), BF16),
        name="conv0s",
    )(state_t, u, w_dw, b_dw.reshape(1, -1), ln_g.reshape(1, -1), ln_b.reshape(1, -1))


def _attn_kernel(g0_ref, qa_ref, ka_ref, vt_ref, o_ref, s_ref, acc_ref):
    i = pl.program_id(1)
    g0 = g0_ref[pl.program_id(0), i]
    tq = qa_ref.shape[1]
    heads = range(ATTN_HEADS)
    qa = [qa_ref[hh * AUG:(hh + 1) * AUG, :] for hh in heads]

    def scores(j):
        start = pl.multiple_of(j * tq, tq)
        return [jnp.dot(ka_ref[pl.ds(start, tq), hh * AUG:(hh + 1) * AUG], qa[hh],
                        preferred_element_type=F32)
                for hh in heads]

    def fold(j, slot, carry, masked):
        out = []
        for hh in heads:
            m_prev, l_prev = carry[hh]
            st = s_ref[slot, hh]
            if masked:
                key = j * tq + lax.broadcasted_iota(jnp.int32, st.shape, 0)
                qry = i * tq + lax.broadcasted_iota(jnp.int32, st.shape, 1)
                st = jnp.where(key <= qry, st, -jnp.inf)
            m_new = jnp.maximum(m_prev, jnp.max(st, axis=0, keepdims=True))
            alpha = jnp.exp2(m_prev - m_new)
            pt = jnp.exp2(st - m_new)
            l_new = alpha * l_prev + jnp.sum(pt, axis=0, keepdims=True)
            vt = vt_ref[jnp.minimum(j, last), hh * HEAD_DIM:(hh + 1) * HEAD_DIM, :]
            acc_ref[hh] = acc_ref[hh] * alpha + jnp.dot(vt, pt.astype(BF16), preferred_element_type=F32)
            out.append((m_new, l_new))
        return tuple(out)

    def step(j, slot, carry, masked):
        nxt = scores(jnp.minimum(j + 1, last))
        carry = fold(j, slot, carry, masked)
        for hh in heads:
            s_ref[1 - slot, hh] = nxt[hh]
        return carry

    def group(gi, carry):
        for u in range(ATTN_UNROLL):
            carry = step(ATTN_UNROLL * gi + u, u % 2, carry, False)
        return carry

    last = vt_ref.shape[0] - 1
    first = scores(ATTN_UNROLL * g0)
    for hh in heads:
        s_ref[0, hh] = first[hh]
    acc_ref[...] = jnp.zeros_like(acc_ref)
    init = tuple((jnp.full((1, tq), -jnp.inf, F32), jnp.zeros((1, tq), F32)) for _ in heads)
    ngroup = i // ATTN_UNROLL
    carry = lax.fori_loop(g0, ngroup, group, init)
    for u in range(ATTN_UNROLL - 1):
        carry = step(ATTN_UNROLL * ngroup + u, u % 2, carry, True)
    carry = fold(ATTN_UNROLL * ngroup + ATTN_UNROLL - 1, (ATTN_UNROLL - 1) % 2, carry, True)
    ot = jnp.concatenate([acc_ref[hh] / carry[hh][1] for hh in heads], axis=0)
    o_ref[...] = ot.T.astype(BF16)


def _first_live_group(stat, nblk):
    per = stat.shape[0]
    nb = nblk // per
    fmax = stat[:, :, 0:nb].transpose(1, 0, 2).reshape(FOX_HEADS, nblk)
    fmin = stat[:, :, nb:2 * nb].transpose(1, 0, 2).reshape(FOX_HEADS, nblk)
    qk = jnp.sqrt(jnp.max(stat[:, :, 2 * nb], axis=0) * jnp.max(stat[:, :, 2 * nb + 1], axis=0))
    bound = 2.0 * NORM_SLACK * qk[:, None, None] + LOG2E * (fmax[:, :, None] - fmin[:, None, :])
    dead = (bound <= -PRUNE_LOG2).reshape(FOX_HEADS // ATTN_HEADS, ATTN_HEADS, nblk, nblk).all(axis=1)
    key_blk = lax.broadcasted_iota(jnp.int32, dead.shape, 2)
    first_live = jnp.min(jnp.where(dead, nblk, key_blk), axis=2)
    return (first_live // ATTN_UNROLL).astype(jnp.int32)


def _attn_prompt(qa, ka, vt, stat):
    t = ka.shape[0]
    tq = ATTN_TILE
    nh = ATTN_HEADS
    nblk = t // tq
    once = pl.Buffered(1)
    grid_spec = pltpu.PrefetchScalarGridSpec(
        num_scalar_prefetch=1,
        grid=(FOX_HEADS // nh, nblk),
        in_specs=[pl.BlockSpec((nh * AUG, tq), lambda p, i, g0: (p, i)),
                  pl.BlockSpec((t, nh * AUG), lambda p, i, g0: (0, p), pipeline_mode=once),
                  pl.BlockSpec((nblk, nh * HEAD_DIM, tq), lambda p, i, g0: (0, p, 0), pipeline_mode=once)],
        out_specs=pl.BlockSpec((tq, nh * HEAD_DIM), lambda p, i, g0: (i, p)),
        scratch_shapes=[pltpu.VMEM((2, nh, tq, tq), F32),
                        pltpu.VMEM((nh, HEAD_DIM, tq), F32)])
    return pl.pallas_call(
        _attn_kernel,
        out_shape=jax.ShapeDtypeStruct((t, FOX_WIDTH), BF16),
        grid_spec=grid_spec,
        compiler_params=_params("arbitrary", "arbitrary"),
        name="attn_prompt",
    )(_first_live_group(stat, nblk), qa, ka, vt)


def _post_decode_kernel(pt_ref, x_ref, at_ref, wa_ref, yc_ref, wc_ref, gm_ref, shf_ref, scf_ref, gf_ref,
                        g_ref, wup_ref, wdn_ref, q_ref, kn_ref, vn_ref, lfn_ref, *rest):
    del pt_ref
    npg = PAGES_PER_STEP
    k_refs, v_refs, lf_refs = rest[:npg], rest[npg:2 * npg], rest[2 * npg:3 * npg]
    o_ref, od_ref, h_ref, x1_ref, mlp_ref = rest[3 * npg:3 * npg + 5]
    dec = rest[3 * npg + 5:]
    s = pl.program_id(1)
    nsteps = pl.num_programs(1)

    @pl.when(s == 0)
    def _():
        mix = (jnp.dot(at_ref[...], wa_ref[...], preferred_element_type=F32)
               + jnp.dot(yc_ref[...], wc_ref[...], preferred_element_type=F32))
        x1 = x_ref[...] + gm_ref[...] * mix
        x1_ref[...] = x1
        h_ref[...] = _modnorm(x1, g_ref[...], shf_ref[...], scf_ref[...]).astype(BF16)
        mlp_ref[...] = jnp.zeros_like(mlp_ref)
        _decode_init(q_ref, kn_ref, vn_ref, lfn_ref, *dec)

    _decode_pages(k_refs, v_refs, lf_refs, *dec)
    a = jnp.maximum(jnp.dot(h_ref[...], wup_ref[s], preferred_element_type=F32), 0.0)
    mlp_ref[...] += jnp.dot((a * a).astype(BF16), wdn_ref[s], preferred_element_type=F32)

    @pl.when(s == nsteps - 1)
    def _():
        o_ref[...] = x1_ref[...] + gf_ref[...] * mlp_ref[...]
        _decode_final(od_ref, dec[3], dec[5])


def _rows_to_tile(vals):
    row = lax.broadcasted_iota(jnp.int32, (FOX_HEADS, V7X_LANES), 0)
    out = jnp.zeros((FOX_HEADS, V7X_LANES), F32)
    for h in range(FOX_HEADS):
        out = jnp.where(row == h, jnp.broadcast_to(vals[h], (FOX_HEADS, V7X_LANES)), out)
    return out


def _decode_init(q_ref, kn_ref, vn_ref, lfn_ref, qb_ref, tri_ref, m_ref, l_ref, run_ref, acc_ref):
    lanes = V7X_LANES
    lane = lax.broadcasted_iota(jnp.int32, (1, lanes), 1)
    tr = lax.broadcasted_iota(jnp.int32, (PAGE_SIZE, 2 * lanes), 0)
    tc = lax.broadcasted_iota(jnp.int32, (PAGE_SIZE, 2 * lanes), 1)
    tri_ref[...] = ((tr > tc) | (tc >= lanes)).astype(BF16)
    run_ref[...] = jnp.broadcast_to(lfn_ref[0], (FOX_HEADS, lanes))
    s_new = []
    for h in range(FOX_HEADS):
        qb_ref[h] = jnp.broadcast_to(q_ref[0, h], (HEAD_DIM, lanes))
        s_new.append(jnp.sum(q_ref[0, h] * kn_ref[0, h], axis=0, keepdims=True))
        acc_ref[h] = jnp.where(lane == 0, jnp.broadcast_to(vn_ref[0, h], (HEAD_DIM, lanes)), 0.0)
    m_ref[...] = _rows_to_tile([jnp.broadcast_to(s, (1, lanes)) for s in s_new])
    l_ref[...] = jnp.ones_like(l_ref)


def _decode_final(o_ref, l_ref, acc_ref):
    for h in range(FOX_HEADS):
        o_ref[0, h] = jnp.sum(acc_ref[h], axis=1, keepdims=True) / l_ref[h:h + 1, 0:1]


def _decode_pages(k_refs, v_refs, lf_refs, qb_ref, tri_ref, m_ref, l_ref, run_ref, acc_ref):
    npg = len(k_refs)
    lanes = V7X_LANES
    heads = range(FOX_HEADS)
    rows_to_tile = _rows_to_tile

    parts = jnp.concatenate([p for i in range(npg) for p in _split3(lf_refs[i][0])], axis=0)
    cs = jnp.dot(parts.astype(BF16), tri_ref[...], preferred_element_type=F32)
    run = run_ref[...]
    scores = []
    for i in range(npg):
        c3 = cs[24 * i:24 * i + 8] + cs[24 * i + 8:24 * i + 16] + cs[24 * i + 16:24 * i + 24]
        qk = rows_to_tile([jnp.sum(k_refs[i][0, h] * qb_ref[h], axis=0, keepdims=True) for h in heads])
        scores.append(qk + (c3[:, :lanes] + run) * LOG2E)
        run = run + c3[:, lanes:]
    run_ref[...] = run

    m_prev = m_ref[...]
    m_blk = scores[0]
    for s in scores[1:]:
        m_blk = jnp.maximum(m_blk, s)
    m_new = jnp.maximum(m_prev, jnp.max(m_blk, axis=1, keepdims=True))
    alpha = jnp.exp2(m_prev - m_new)
    ps = [jnp.exp2(s - m_new) for s in scores]
    p_tot = ps[0]
    for p in ps[1:]:
        p_tot = p_tot + p
    l_ref[...] = alpha * l_ref[...] + jnp.sum(p_tot, axis=1, keepdims=True)
    m_ref[...] = m_new
    for h in heads:
        acc = acc_ref[h] * alpha[h:h + 1, :]
        for i in range(npg):
            acc = acc + v_refs[i][0, h] * ps[i][h:h + 1, :]
        acc_ref[h] = acc


def _post_decode(x, at, w_at, yc, w_yc, mod, g_mlp, w_up, w_down,
                 q, k_new, v_new, lf_new, cache_k, cache_v, cache_logf, page_table):
    t = x.shape[0]
    tm = ROW_TILE
    b = q.shape[0]
    n_pages = page_table.shape[1]
    npg = PAGES_PER_STEP
    nsteps = n_pages // npg
    assert t // tm == b, "one prompt row tile per sample sequence"
    fch = D_FF // nsteps
    col = (FOX_HEADS, HEAD_DIM, 1)
    w_up_c = w_up.reshape(D_MODEL, nsteps, fch).transpose(1, 0, 2)
    w_dn_c = w_down.reshape(nsteps, fch, D_MODEL)
    ck = cache_k.transpose(0, 2, 3, 1)
    cv = cache_v.transpose(0, 2, 3, 1)
    clf = cache_logf.transpose(0, 2, 1)

    def page_map(nd):
        def for_slot(slot):
            return lambda bi, g, pt: (pt[bi, n_pages - 1 - (g * npg + slot)],) + (0,) * (nd - 1)
        return for_slot

    def const(shape):
        nd = len(shape)
        return pl.BlockSpec(shape, lambda bi, g, pt: (0,) * nd, pipeline_mode=pl.Buffered(1))

    rows = lambda bi, g, pt: (bi, 0)
    mcol = lambda c: (lambda bi, g, pt: (0, c))
    seq4 = lambda bi, g, pt: (bi, 0, 0, 0)
    page_blk = (1, FOX_HEADS, HEAD_DIM, PAGE_SIZE)
    in_specs = [pl.BlockSpec((tm, D_MODEL), rows),
                pl.BlockSpec((tm, FOX_WIDTH), rows), const(w_at.shape),
                pl.BlockSpec((tm, CONF_CH), rows), const(w_yc.shape),
                pl.BlockSpec((1, D_MODEL), mcol(2)), pl.BlockSpec((1, D_MODEL), mcol(3)),
                pl.BlockSpec((1, D_MODEL), mcol(4)), pl.BlockSpec((1, D_MODEL), mcol(5)),
                const((1, D_MODEL)), const(w_up_c.shape), const(w_dn_c.shape)]
    in_specs += [pl.BlockSpec((1,) + col, seq4)] * 3 + [pl.BlockSpec((1, FOX_HEADS, 1), lambda bi, g, pt: (bi, 0, 0))]
    in_specs += [pl.BlockSpec(page_blk, page_map(4)(s)) for s in range(npg)]
    in_specs += [pl.BlockSpec(page_blk, page_map(4)(s)) for s in range(npg)]
    in_specs += [pl.BlockSpec((1, FOX_HEADS, PAGE_SIZE), page_map(3)(s)) for s in range(npg)]
    grid_spec = pltpu.PrefetchScalarGridSpec(
        num_scalar_prefetch=1,
        grid=(b, nsteps),
        in_specs=in_specs,
        out_specs=[pl.BlockSpec((tm, D_MODEL), rows), pl.BlockSpec((1,) + col, seq4)],
        scratch_shapes=[pltpu.VMEM((tm, D_MODEL), BF16),
                        pltpu.VMEM((tm, D_MODEL), F32),
                        pltpu.VMEM((tm, D_MODEL), F32),
                        pltpu.VMEM((FOX_HEADS, HEAD_DIM, V7X_LANES), F32),
                        pltpu.VMEM((PAGE_SIZE, 2 * V7X_LANES), BF16),
                        pltpu.VMEM((FOX_HEADS, V7X_LANES), F32),
                        pltpu.VMEM((FOX_HEADS, V7X_LANES), F32),
                        pltpu.VMEM((FOX_HEADS, V7X_LANES), F32),
                        pltpu.VMEM((FOX_HEADS, HEAD_DIM, V7X_LANES), F32)])
    y, dec = pl.pallas_call(
        _post_decode_kernel,
        out_shape=[jax.ShapeDtypeStruct((t, D_MODEL), F32), jax.ShapeDtypeStruct((b,) + col, F32)],
        grid_spec=grid_spec,
        compiler_params=pltpu.CompilerParams(dimension_semantics=("arbitrary", "arbitrary"),
                                             vmem_limit_bytes=FUSED_VMEM_LIMIT),
        name="post_decode",
    )(page_table, x, at, w_at, yc, w_yc, mod, mod, mod, mod, g_mlp, w_up_c, w_dn_c,
      q.reshape((b,) + col), k_new.reshape((b,) + col), v_new.reshape((b,) + col),
      lf_new.reshape(b, FOX_HEADS, 1), *([ck] * npg), *([cv] * npg), *([clf] * npg))
    return y, dec.reshape(b, FOX_WIDTH).astype(BF16)


def _post_kernel(*refs, n_mix, final):
    x_ref = refs[0]
    mix_refs = refs[1:1 + 2 * n_mix]
    gm_ref, shf_ref, scf_ref, gf_ref, g_ref, wup_ref, wdn_ref = refs[1 + 2 * n_mix:8 + 2 * n_mix]
    rest = refs[8 + 2 * n_mix:]
    if final:
        gfin_ref, o_ref = rest
    else:
        (o_ref,) = rest
    mix = None
    for a in range(n_mix):
        t = jnp.dot(mix_refs[2 * a][...], mix_refs[2 * a + 1][...], preferred_element_type=F32)
        mix = t if mix is None else mix + t
    x1 = x_ref[...] + gm_ref[...] * mix
    h = _modnorm(x1, g_ref[...], shf_ref[...], scf_ref[...]).astype(BF16)
    fchunk = 1024
    acc = None
    for f0 in range(0, D_FF, fchunk):
        a = jnp.maximum(jnp.dot(h, wup_ref[:, f0:f0 + fchunk], preferred_element_type=F32), 0.0)
        t = jnp.dot((a * a).astype(BF16), wdn_ref[f0:f0 + fchunk, :], preferred_element_type=F32)
        acc = t if acc is None else acc + t
    x2 = x1 + gf_ref[...] * acc
    if final:
        ms = jnp.mean(x2 * x2, axis=-1, keepdims=True)
        x2 = x2 * lax.rsqrt(ms + EPS) * gfin_ref[...]
    o_ref[...] = x2


def _post(x, mixes, mod, g_mlp, w_up, w_down, final_g=None):
    t = x.shape[0]
    tm = min(ROW_TILE, t)
    mm = mod.shape[0]
    bm = 1 if mm == 1 else tm
    row = lambda i: (i, 0)

    def mcol(c):
        return (lambda i: (0, c)) if mm == 1 else (lambda i: (i, c))

    in_specs = [pl.BlockSpec((tm, D_MODEL), row)]
    args = [x]
    for a, w in mixes:
        in_specs += [pl.BlockSpec((tm, a.shape[1]), row), _const_spec(w.shape)]
        args += [a, w]
    in_specs += [pl.BlockSpec((bm, D_MODEL), mcol(2)), pl.BlockSpec((bm, D_MODEL), mcol(3)),
                 pl.BlockSpec((bm, D_MODEL), mcol(4)), pl.BlockSpec((bm, D_MODEL), mcol(5)),
                 _const_spec((1, D_MODEL)), _const_spec(w_up.shape), _const_spec(w_down.shape)]
    args += [mod, mod, mod, mod, g_mlp, w_up, w_down]
    if final_g is not None:
        in_specs.append(_const_spec((1, D_MODEL)))
        args.append(final_g)
    return pl.pallas_call(
        functools.partial(_post_kernel, n_mix=len(mixes), final=final_g is not None),
        out_shape=jax.ShapeDtypeStruct((t, D_MODEL), F32),
        grid=(t // tm,),
        in_specs=in_specs,
        out_specs=pl.BlockSpec((tm, D_MODEL), row),
        compiler_params=_params("arbitrary"),
        name="post",
    )(*args)


def _mix1_kernel(x_ref, sh_ref, sc_ref, g_ref, win_ref, wdw_ref, *rest, per_row_state):
    x = x_ref[...]
    tm = x.shape[0]
    h = _modnorm(x, g_ref[...], sh_ref[...], sc_ref[...]).astype(BF16)
    pad = V7X_SUBLANES
    if not per_row_state:
        by_ref, tail_ref, buf_ref = rest

        @pl.when(pl.program_id(0) == 0)
        def _():
            buf_ref[0:pad, :] = jnp.zeros((pad, D_MODEL), F32)
    else:
        s0_ref, s1_ref, by_ref, cx_ref = rest

    for c0 in range(0, D_MODEL, MIX_CHUNK):
        cols = slice(c0, c0 + MIX_CHUNK)

        def proj(part):
            w = win_ref[:, part * D_MODEL + c0:part * D_MODEL + c0 + MIX_CHUNK]
            return jnp.dot(h, w, preferred_element_type=F32)

        b_gate = proj(0)
        cx = proj(1) * proj(2)
        w0, w1, w2 = wdw_ref[0:1, cols], wdw_ref[1:2, cols], wdw_ref[2:3, cols]
        if per_row_state:
            y = w0 * s0_ref[:, cols] + w1 * s1_ref[:, cols] + w2 * cx
            cx_ref[:, cols] = cx
        else:
            buf_ref[pad:, cols] = cx
            y = w0 * buf_ref[pl.ds(pad - 2, tm), cols] + w1 * buf_ref[pl.ds(pad - 1, tm), cols] + w2 * cx
            tail = cx[tm - pad:, :]
            buf_ref[0:pad, cols] = tail
            tail_ref[:, cols] = tail
        by_ref[:, cols] = (b_gate * y).astype(BF16)


def _mix1(x, mod, g, w_in, w_dw, state=None):
    t = x.shape[0]
    tm = min(ROW_TILE, t)
    mm = mod.shape[0]
    bm = 1 if mm == 1 else tm
    row = lambda i: (i, 0)
    mcol = lambda c: (lambda i: (0, c)) if mm == 1 else (lambda i: (i, c))
    in_specs = [pl.BlockSpec((tm, D_MODEL), row),
                pl.BlockSpec((bm, D_MODEL), mcol(0)), pl.BlockSpec((bm, D_MODEL), mcol(1)),
                _const_spec((1, D_MODEL)), _const_spec(w_in.shape), _const_spec(w_dw.shape)]
    args = [x, mod, mod, g, w_in, w_dw]
    if state is not None:
        in_specs += [pl.BlockSpec((tm, D_MODEL), row)] * 2
        args += [state[0], state[1]]
        out_shape = [jax.ShapeDtypeStruct((t, D_MODEL), BF16), jax.ShapeDtypeStruct((t, D_MODEL), F32)]
        out_specs = [pl.BlockSpec((tm, D_MODEL), row)] * 2
        scratch = []
    else:
        nt = t // tm
        out_shape = [jax.ShapeDtypeStruct((t, D_MODEL), BF16),
                     jax.ShapeDtypeStruct((nt * V7X_SUBLANES, D_MODEL), F32)]
        out_specs = [pl.BlockSpec((tm, D_MODEL), row), pl.BlockSpec((V7X_SUBLANES, D_MODEL), row)]
        scratch = [pltpu.VMEM((tm + V7X_SUBLANES, D_MODEL), F32)]
    return pl.pallas_call(
        functools.partial(_mix1_kernel, per_row_state=state is not None),
        out_shape=out_shape,
        grid=(t // tm,),
        in_specs=in_specs,
        out_specs=out_specs,
        scratch_shapes=scratch,
        compiler_params=_params("arbitrary"),
        name="mix1",
    )(*args)


def kernel(x_prompt, x_sample, cache_k, cache_v, cache_logf, state_conformer_conv, state_short_conv, page_table, c_prompt, c_sample, l0_w_ada, l0_b_ada, l0_norm_mix, l0_norm_mlp, l0_w_in, l0_b_forget, l0_w_dw, l0_b_dw, l0_conv_ln_g, l0_conv_ln_b, l0_w_out, l0_w_up, l0_w_down, l1_w_ada, l1_b_ada, l1_norm_mix, l1_norm_mlp, l1_w_in, l1_w_dw, l1_w_out, l1_w_up, l1_w_down, final_norm):
    bp, t, d = x_prompt.shape
    bs = x_sample.shape[0]
    assert bp == 1 and d == D_MODEL and x_sample.shape[1] == 1

    def per_head_padded(wcols, scale):
        w3 = (wcols * scale).reshape(d, FOX_HEADS, HEAD_DIM)
        return jnp.pad(w3, ((0, 0), (0, 0), (0, AUG - HEAD_DIM))).reshape(d, FOX_HEADS * AUG).astype(BF16)

    w0 = {
        "q": per_head_padded(l0_w_in[:, :FOX_WIDTH], ATTN_SCALE * LOG2E),
        "k": per_head_padded(l0_w_in[:, FOX_WIDTH:2 * FOX_WIDTH], 1.0),
        "kv": l0_w_in[:, FOX_WIDTH:3 * FOX_WIDTH].astype(BF16),
        "f":jnp.pad(l0_w_in[:, 3 * FOX_WIDTH:3 * FOX_WIDTH + FOX_HEADS],
                     ((0, 0), (0, V7X_LANES - FOX_HEADS))).astype(BF16),
        "bf": jnp.pad(l0_b_forget, (0, V7X_LANES - FOX_HEADS)).reshape(1, V7X_LANES),
        "glu": l0_w_in[:, 3 * FOX_WIDTH + FOX_HEADS:].astype(BF16),
    }
    w0_out_attn = l0_w_out[:FOX_WIDTH].astype(BF16)
    w0_out_conv = l0_w_out[FOX_WIDTH:].astype(BF16)
    w0_up, w0_down = l0_w_up.astype(BF16), l0_w_down.astype(BF16)
    w1_in, w1_out = l1_w_in.astype(BF16), l1_w_out.astype(BF16)
    w1_up, w1_down = l1_w_up.astype(BF16), l1_w_down.astype(BF16)
    g0_mix, g0_mlp = l0_norm_mix.reshape(1, d), l0_norm_mlp.reshape(1, d)
    g1_mix, g1_mlp = l1_norm_mix.reshape(1, d), l1_norm_mlp.reshape(1, d)
    gfin = final_norm.reshape(1, d)

    c_all = jnp.concatenate([c_prompt, c_sample], axis=0)
    mod0 = _ada(c_all, l0_w_ada, l0_b_ada)
    mod1 = _ada(c_all, l1_w_ada, l1_b_ada)
    mod0_p, mod0_s = mod0[:1], mod0[1:]
    mod1_p, mod1_s = mod1[:1], mod1[1:]

    xp = x_prompt.reshape(t, d)
    qa_p, ka_p, vt_p, st_p, k_p, v_p, lf_p, u_p = _proj0(xp, mod0_p, g0_mix, w0, True)
    yc_p = _conv0_prompt(u_p, l0_w_dw, l0_b_dw, l0_conv_ln_g, l0_conv_ln_b)
    at_p = _attn_prompt(qa_p, ka_p, vt_p, st_p)

    xs = x_sample.reshape(bs, d)
    qp_s, k_s, v_s, lf_s, u_s = _proj0(xs, mod0_s, g0_mix, w0, False)
    q_s = qp_s.reshape(bs, FOX_HEADS, AUG)[:, :, :HEAD_DIM].reshape(bs, FOX_WIDTH)
    yc_s = _conv0_sample(state_conformer_conv.transpose(1, 0, 2), u_s,
                         l0_w_dw, l0_b_dw, l0_conv_ln_g, l0_conv_ln_b)

    x1_p, at_s = _post_decode(xp, at_p, w0_out_attn, yc_p, w0_out_conv, mod0_p, g0_mlp, w0_up, w0_down,
                              q_s, k_s, v_s, lf_s, cache_k, cache_v, cache_logf, page_table)

    by_p, tail_p = _mix1(x1_p, mod1_p, g1_mix, w1_in, l1_w_dw)
    y_p = _post(x1_p, [(by_p, w1_out)], mod1_p, g1_mlp, w1_up, w1_down, gfin)

    x1_s = _post(xs, [(at_s, w0_out_attn), (yc_s, w0_out_conv)], mod0_s, g0_mlp, w0_up, w0_down)
    by_s, cx_s = _mix1(x1_s, mod1_s, g1_mix, w1_in, l1_w_dw,
                       state=(state_short_conv[:, 0], state_short_conv[:, 1]))
    y_s = _post(x1_s, [(by_s, w1_out)], mod1_s, g1_mlp, w1_up, w1_down, gfin)

    hs = (FOX_HEADS, HEAD_DIM)
    return (y_p.reshape(1, t, d), y_s.reshape(bs, 1, d),
            k_p.reshape(1, t, *hs), v_p.reshape(1, t, *hs), lf_p.reshape(1, t, FOX_HEADS),
            u_p[t - (CONF_CONV_WIDTH - 1):].reshape(1, CONF_CONV_WIDTH - 1, CONF_CH),
            tail_p[-(SC_CONV_WIDTH - 1):].reshape(1, SC_CONV_WIDTH - 1, d),
            k_s.reshape(bs, 1, *hs), v_s.reshape(bs, 1, *hs), lf_s.reshape(bs, 1, FOX_HEADS),
            jnp.concatenate([state_conformer_conv[:, 1:], u_s[:, None, :]], axis=1),
            jnp.stack([state_short_conv[:, 1], cx_s], axis=1))
```

```python
import functools

import jax
import jax.numpy as jnp
import numpy as np
from jax import lax
from jax.experimental import pallas as pl
from jax.experimental.pallas import tpu as pltpu

F32 = jnp.float32
BF16 = jnp.bfloat16

D_MODEL = 1024
FOX_HEADS = 8
HEAD_DIM = 64
FOX_WIDTH = FOX_HEADS * HEAD_DIM
CONF_CH = D_MODEL - FOX_WIDTH
CONF_CONV_WIDTH = 31
SC_CONV_WIDTH = 3
D_FF = 4 * D_MODEL
PAGE_SIZE = 128
EPS = 1e-6
ATTN_SCALE = HEAD_DIM ** -0.5
LOG2E = 1.4426950408889634

V7X_LANES = 128
V7X_SUBLANES = 8
V7X_VMEM_BYTES = 64 * 1024 * 1024
VMEM_LIMIT = V7X_VMEM_BYTES - 12 * 1024 * 1024
FUSED_VMEM_LIMIT = V7X_VMEM_BYTES - 4 * 1024 * 1024

ROW_TILE = 512
CONV_CHUNK = 64
CONV_HALO = 32
ATTN_TILE = 256
AUG = V7X_LANES
ATTN_HEADS = 4
ATTN_UNROLL = 2
PRUNE_LOG2 = 136.0
MIX_CHUNK = 256
NORM_SLACK = 1.02
PAGES_PER_STEP = 16


def _const_spec(shape):
    nd = len(shape)
    return pl.BlockSpec(shape, lambda *_: (0,) * nd, pipeline_mode=pl.Buffered(1))


def _params(*sem):
    return pltpu.CompilerParams(dimension_semantics=sem, vmem_limit_bytes=VMEM_LIMIT)


def _modnorm(x, g, shift, scale):
    ms = jnp.mean(x * x, axis=-1, keepdims=True)
    y = x * lax.rsqrt(ms + EPS)
    return (y * g) * (1.0 + scale) + shift


def _split3(x):
    hi = x.astype(BF16).astype(F32)
    r = x - hi
    mid = r.astype(BF16).astype(F32)
    lo = (r - mid).astype(BF16).astype(F32)
    return hi, mid, lo


def _ada_kernel(c_ref, w_ref, b_ref, o_ref):
    c = c_ref[...]
    s = (c * jax.nn.sigmoid(c)).astype(BF16)
    o_ref[...] = jnp.dot(s, w_ref[...].astype(BF16), preferred_element_type=F32) + b_ref[...]


def _ada(c_all, w_ada, b_ada):
    m = c_all.shape[0]
    n = w_ada.shape[1]
    bn = 1536
    return pl.pallas_call(
        _ada_kernel,
        out_shape=jax.ShapeDtypeStruct((m, n), F32),
        grid=(n // bn,),
        in_specs=[pl.BlockSpec((m, D_MODEL), lambda j: (0, 0)),
                  pl.BlockSpec((D_MODEL, bn), lambda j: (0, j)),
                  pl.BlockSpec((1, bn), lambda j: (0, j))],
        out_specs=pl.BlockSpec((m, bn), lambda j: (0, j)),
        compiler_params=_params("arbitrary"),
        name="ada",
    )(c_all, w_ada, b_ada.reshape(1, n))


def _bias_placement():
    pq = np.zeros((V7X_LANES, FOX_HEADS * AUG), np.float32)
    pk = np.zeros((V7X_LANES, FOX_HEADS * AUG), np.float32)
    for h in range(FOX_HEADS):
        for p in range(3):
            pq[p * FOX_HEADS + h, h * AUG + HEAD_DIM + p] = 1.0
            pk[p * FOX_HEADS + h, h * AUG + HEAD_DIM + 3 + p] = -1.0
            pq[3 * FOX_HEADS, h * AUG + HEAD_DIM + 3 + p] = 1.0
            pk[3 * FOX_HEADS, h * AUG + HEAD_DIM + p] = 1.0
    return jnp.asarray(pq, BF16), jnp.asarray(pk, BF16)


def _proj0_kernel(*refs, prompt):
    x_ref, sh_ref, sc_ref, g_ref, wq_ref, wkv_ref, wf_ref, bf_ref, wglu_ref = refs[:9]
    tm = x_ref.shape[0]
    if prompt:
        (wk_ref, pq_ref, pk_ref, wdw_ref, bdw_ref, lng_ref, lnb_ref,
         qa_ref, ka_ref, vt_ref, st_ref, yc_ref, k_ref, v_ref, lf_ref, u_ref,
         tri_ref, carry_ref, buf_ref) = refs[9:]
        tk = vt_ref.shape[2]

        @pl.when(pl.program_id(0) == 0)
        def _():
            carry_ref[...] = jnp.zeros_like(carry_ref)
            buf_ref[0, 0:CONV_HALO, :] = jnp.zeros((CONV_HALO, CONF_CH), F32)
            r = lax.broadcasted_iota(jnp.int32, (tm, tm), 0)
            c = lax.broadcasted_iota(jnp.int32, (tm, tm), 1)
            tri_ref[...] = (r <= c).astype(BF16)
    else:
        q_ref, k_ref, v_ref, lf_ref, u_ref = refs[9:]

    x = x_ref[...]
    h = _modnorm(x, g_ref[...], sh_ref[...], sc_ref[...]).astype(BF16)
    glu = jnp.dot(h, wglu_ref[...], preferred_element_type=F32)
    u = glu[:, :CONF_CH] * jax.nn.sigmoid(glu[:, CONF_CH:])
    if prompt:
        buf_ref[0, CONV_HALO:, :] = u
        _conv_taps(buf_ref, wdw_ref, bdw_ref, lng_ref, lnb_ref, yc_ref, tm)
        buf_ref[0, 0:CONV_HALO, :] = u[tm - CONV_HALO:, :]
    fg =jnp.dot(h, wf_ref[...], preferred_element_type=F32) + bf_ref[...]
    lf = jnp.minimum(fg, 0.0) - jnp.log1p(jnp.exp(-jnp.abs(fg)))
    if prompt:
        lft = lf.T[:FOX_HEADS]
        parts = jnp.concatenate(_split3(lft), axis=0).astype(BF16)
        cs = jnp.dot(parts, tri_ref[...], preferred_element_type=F32)
        ft = cs[0:8] + cs[8:16] + cs[16:24] + carry_ref[...]
        carry_ref[...] = ft[:, tm - 1:tm]
        pieces_t = jnp.concatenate(
            list(_split3(ft * LOG2E)) + [jnp.ones((FOX_HEADS, tm), F32),
                                 jnp.zeros((V7X_LANES - 4 * FOX_HEADS, tm), F32)], axis=0)
        pieces = pieces_t.T.astype(BF16)

    kv = jnp.dot(h, wkv_ref[...], preferred_element_type=F32)
    k = kv[:, :FOX_WIDTH]
    v = kv[:, FOX_WIDTH:]
    qp = jnp.dot(h, wq_ref[...], preferred_element_type=F32)

    if not prompt:
        q_ref[...] = qp
    else:
        qpt = (qp + jnp.dot(pieces, pq_ref[...], preferred_element_type=F32)).T
        qa_ref[...] = qpt.astype(BF16)
        kp = jnp.dot(h, wk_ref[...], preferred_element_type=F32)
        ka_ref[...] = (kp + jnp.dot(pieces, pk_ref[...], preferred_element_type=F32)).astype(BF16)
        vtt = v.T.astype(BF16)
        for c0 in range(tm // tk):
            vt_ref[c0] = vtt[:, c0 * tk:(c0 + 1) * tk]

        nb = tm // tk
        row = lax.broadcasted_iota(jnp.int32, (FOX_HEADS, V7X_LANES), 0)
        lane = lax.broadcasted_iota(jnp.int32, (FOX_HEADS, V7X_LANES), 1)
        stat = jnp.zeros((FOX_HEADS, V7X_LANES), F32)
        for c0 in range(nb):
            blk = ft[:, c0 * tk:(c0 + 1) * tk]
            stat = jnp.where(lane == c0, jnp.max(blk, axis=1, keepdims=True), stat)
            stat = jnp.where(lane == nb + c0, jnp.min(blk, axis=1, keepdims=True), stat)
        for hh in range(FOX_HEADS):
            qh = qpt[hh * AUG:hh * AUG + HEAD_DIM, :]
            qn2 = jnp.max(jnp.sum(qh * qh, axis=0, keepdims=True), axis=1, keepdims=True)
            kh = kp[:, hh * AUG:(hh + 1) * AUG]
            kn2 = jnp.max(jnp.sum(kh * kh, axis=1, keepdims=True), axis=0, keepdims=True)
            stat = jnp.where((row == hh) & (lane == 2 * nb), qn2, stat)
            stat = jnp.where((row == hh) & (lane == 2 * nb + 1), kn2, stat)
        st_ref[0] = stat

    k_ref[...] = k
    v_ref[...] = v
    lf_ref[...] = lf[:, :FOX_HEADS]
    u_ref[...] = u


def _proj0(x, mod, g, w, prompt):
    t = x.shape[0]
    tm = min(ROW_TILE, t)
    mm = mod.shape[0]
    mrow = (lambda i: (0, 0)) if mm == 1 else (lambda i: (i, 0))
    mrow1 = (lambda i: (0, 1)) if mm == 1 else (lambda i: (i, 1))
    bm = 1 if mm == 1 else tm
    row = lambda i: (i, 0)
    wide = FOX_HEADS * AUG
    in_specs = [pl.BlockSpec((tm, D_MODEL), row),
                pl.BlockSpec((bm, D_MODEL), mrow),
                pl.BlockSpec((bm, D_MODEL), mrow1),
                _const_spec((1, D_MODEL)),
                _const_spec(w["q"].shape),
                _const_spec(w["kv"].shape),
                _const_spec(w["f"].shape),
                _const_spec((1, V7X_LANES)),
                _const_spec(w["glu"].shape)]
    args = [x, mod, mod, g, w["q"], w["kv"], w["f"], w["bf"], w["glu"]]
    tail_shape = [jax.ShapeDtypeStruct((t, FOX_WIDTH), F32), jax.ShapeDtypeStruct((t, FOX_WIDTH), F32),
                  jax.ShapeDtypeStruct((t, FOX_HEADS), F32), jax.ShapeDtypeStruct((t, CONF_CH), F32)]
    tail_specs = [pl.BlockSpec((tm, FOX_WIDTH), row), pl.BlockSpec((tm, FOX_WIDTH), row),
                  pl.BlockSpec((tm, FOX_HEADS), row), pl.BlockSpec((tm, CONF_CH), row)]
    if prompt:
        tk = ATTN_TILE
        pq, pk = _bias_placement()
        in_specs += [_const_spec(w["k"].shape), _const_spec(pq.shape), _const_spec(pk.shape),
                     _const_spec((CONF_CONV_WIDTH, CONF_CH)),
                     _const_spec((1, CONF_CH)), _const_spec((1, CONF_CH)), _const_spec((1, CONF_CH))]
        args += [w["k"], pq, pk, w["dw"], w["b_dw"], w["ln_g"], w["ln_b"]]
        out_shape = [jax.ShapeDtypeStruct((wide, t), BF16), jax.ShapeDtypeStruct((t, wide), BF16),
                     jax.ShapeDtypeStruct((t // tk, FOX_WIDTH, tk), BF16),
                     jax.ShapeDtypeStruct((t // tm, FOX_HEADS, V7X_LANES), F32),
                     jax.ShapeDtypeStruct((t, CONF_CH), BF16)] + tail_shape
        out_specs = [pl.BlockSpec((wide, tm), lambda i: (0, i)), pl.BlockSpec((tm, wide), row),
                     pl.BlockSpec((tm // tk, FOX_WIDTH, tk), lambda i: (i, 0, 0)),
                     pl.BlockSpec((1, FOX_HEADS, V7X_LANES), lambda i: (i, 0, 0)),
                     pl.BlockSpec((tm, CONF_CH), row)] + tail_specs
        scratch = [pltpu.VMEM((tm, tm), BF16), pltpu.VMEM((FOX_HEADS, 1), F32),
                   pltpu.VMEM((V7X_SUBLANES, tm + CONV_HALO, CONF_CH), F32)]
    else:
        out_shape = [jax.ShapeDtypeStruct((t, wide), F32)] + tail_shape
        out_specs = [pl.BlockSpec((tm, wide), row)] + tail_specs
        scratch = []
    return pl.pallas_call(
        functools.partial(_proj0_kernel, prompt=prompt),
        out_shape=out_shape,
        grid=(t // tm,),
        in_specs=in_specs,
        out_specs=out_specs,
        scratch_shapes=scratch,
        compiler_params=_params("arbitrary"),
        name="proj0",
    )(*args)


def _ln_silu(y, ln_g, ln_b):
    mu = jnp.mean(y, axis=-1, keepdims=True)
    d = y - mu
    var = jnp.mean(d * d, axis=-1, keepdims=True)
    z = d * lax.rsqrt(var + EPS) * ln_g + ln_b
    return z * jax.nn.sigmoid(z)


def _conv_taps(buf_ref, wdw_ref, bdw_ref, lng_ref, lnb_ref, y_ref, tm):
    span = tm + CONV_HALO - V7X_SUBLANES
    for r in range(1, V7X_SUBLANES):
        buf_ref[r, 0:span, :] = buf_ref[0, pl.ds(r, span), :]
    first = CONV_HALO - (CONF_CONV_WIDTH - 1)
    for c0 in range(0, tm, CONV_CHUNK):
        acc = jnp.zeros((CONV_CHUNK, CONF_CH), F32)
        for kk in range(CONF_CONV_WIDTH):
            r = (first + kk) % V7X_SUBLANES
            acc = acc + wdw_ref[kk:kk + 1, :] * buf_ref[r, pl.ds(first + kk - r + c0, CONV_CHUNK), :]
        z = _ln_silu(acc + bdw_ref[...], lng_ref[...], lnb_ref[...])
        y_ref[c0:c0 + CONV_CHUNK, :] = z.astype(BF16)


def _conv0s_kernel(st_ref, u_ref, wdw_ref, bdw_ref, lng_ref, lnb_ref, y_ref):
    acc = wdw_ref[CONF_CONV_WIDTH - 1:CONF_CONV_WIDTH, :] * u_ref[...]
    for kk in range(CONF_CONV_WIDTH - 1):
        acc = acc + wdw_ref[kk:kk + 1, :] * st_ref[kk]
    z = _ln_silu(acc + bdw_ref[...], lng_ref[...], lnb_ref[...])
    y_ref[...] = z.astype(BF16)


def _conv0_sample(state_t, u, w_dw, b_dw, ln_g, ln_b):
    b = u.shape[0]
    return pl.pallas_call(
        _conv0s_kernel,
        out_shape=jax.ShapeDtypeStruct((b, CONF_CH), BF16),
        name="conv0s",
    )(state_t, u, w_dw, b_dw.reshape(1, -1), ln_g.reshape(1, -1), ln_b.reshape(1, -1))


def _attn_kernel(g0_ref, qa_ref, ka_ref, vt_ref, o_ref, s_ref, acc_ref):
    i = pl.program_id(1)
    g0 = g0_ref[pl.program_id(0), i]
    tq = qa_ref.shape[1]
    heads = range(ATTN_HEADS)
    qa = [qa_ref[hh * AUG:(hh + 1) * AUG, :] for hh in heads]

    def scores(j):
        start = pl.multiple_of(j * tq, tq)
        return [jnp.dot(ka_ref[pl.ds(start, tq), hh * AUG:(hh + 1) * AUG], qa[hh],
                        preferred_element_type=F32)
                for hh in heads]

    def fold(j, slot, carry, masked):
        out = []
        for hh in heads:
            m_prev, l_prev = carry[hh]
            st = s_ref[slot, hh]
            if masked:
                key = j * tq + lax.broadcasted_iota(jnp.int32, st.shape, 0)
                qry = i * tq + lax.broadcasted_iota(jnp.int32, st.shape, 1)
                st = jnp.where(key <= qry, st, -jnp.inf)
            m_new = jnp.maximum(m_prev, jnp.max(st, axis=0, keepdims=True))
            alpha = jnp.exp2(m_prev - m_new)
            pt = jnp.exp2(st - m_new)
            l_new = alpha * l_prev + jnp.sum(pt, axis=0, keepdims=True)
            vt = vt_ref[jnp.minimum(j, last), hh * HEAD_DIM:(hh + 1) * HEAD_DIM, :]
            acc_ref[hh] = acc_ref[hh] * alpha + jnp.dot(vt, pt.astype(BF16), preferred_element_type=F32)
            out.append((m_new, l_new))
        return tuple(out)

    def step(j, slot, carry, masked):
        nxt = scores(jnp.minimum(j + 1, last))
        carry = fold(j, slot, carry, masked)
        for hh in heads:
            s_ref[1 - slot, hh] = nxt[hh]
        return carry

    def group(gi, carry):
        for u in range(ATTN_UNROLL):
            carry = step(ATTN_UNROLL * gi + u, u % 2, carry, False)
        return carry

    last = vt_ref.shape[0] - 1
    first = scores(ATTN_UNROLL * g0)
    for hh in heads:
        s_ref[0, hh] = first[hh]
    acc_ref[...] = jnp.zeros_like(acc_ref)
    init = tuple((jnp.full((1, tq), -jnp.inf, F32), jnp.zeros((1, tq), F32)) for _ in heads)
    ngroup = i // ATTN_UNROLL
    carry = lax.fori_loop(g0, ngroup, group, init)
    for u in range(ATTN_UNROLL - 1):
        carry = step(ATTN_UNROLL * ngroup + u, u % 2, carry, True)
    carry = fold(ATTN_UNROLL * ngroup + ATTN_UNROLL - 1, (ATTN_UNROLL - 1) % 2, carry, True)
    ot = jnp.concatenate([acc_ref[hh] / carry[hh][1] for hh in heads], axis=0)
    o_ref[...] = ot.T.astype(BF16)


def _first_live_group(stat, nblk):
    per = stat.shape[0]
    nb = nblk // per
    fmax = stat[:, :, 0:nb].transpose(1, 0, 2).reshape(FOX_HEADS, nblk)
    fmin = stat[:, :, nb:2 * nb].transpose(1, 0, 2).reshape(FOX_HEADS, nblk)
    qk = jnp.sqrt(jnp.max(stat[:, :, 2 * nb], axis=0) * jnp.max(stat[:, :, 2 * nb + 1], axis=0))
    bound = 2.0 * NORM_SLACK * qk[:, None, None] + LOG2E * (fmax[:, :, None] - fmin[:, None, :])
    dead = (bound <= -PRUNE_LOG2).reshape(FOX_HEADS // ATTN_HEADS, ATTN_HEADS, nblk, nblk).all(axis=1)
    key_blk = lax.broadcasted_iota(jnp.int32, dead.shape, 2)
    first_live = jnp.min(jnp.where(dead, nblk, key_blk), axis=2)
    return (first_live // ATTN_UNROLL).astype(jnp.int32)


def _attn_prompt(qa, ka, vt, stat):
    t = ka.shape[0]
    tq = ATTN_TILE
    nh = ATTN_HEADS
    nblk = t // tq
    once = pl.Buffered(1)
    grid_spec = pltpu.PrefetchScalarGridSpec(
        num_scalar_prefetch=1,
        grid=(FOX_HEADS // nh, nblk),
        in_specs=[pl.BlockSpec((nh * AUG, tq), lambda p, i, g0: (p, i)),
                  pl.BlockSpec((t, nh * AUG), lambda p, i, g0: (0, p), pipeline_mode=once),
                  pl.BlockSpec((nblk, nh * HEAD_DIM, tq), lambda p, i, g0: (0, p, 0), pipeline_mode=once)],
        out_specs=pl.BlockSpec((tq, nh * HEAD_DIM), lambda p, i, g0: (i, p)),
        scratch_shapes=[pltpu.VMEM((2, nh, tq, tq), F32),
                        pltpu.VMEM((nh, HEAD_DIM, tq), F32)])
    return pl.pallas_call(
        _attn_kernel,
        out_shape=jax.ShapeDtypeStruct((t, FOX_WIDTH), BF16),
        grid_spec=grid_spec,
        compiler_params=_params("arbitrary", "arbitrary"),
        name="attn_prompt",
    )(_first_live_group(stat, nblk), qa, ka, vt)


def _post_decode_kernel(pt_ref, x_ref, at_ref, wa_ref, yc_ref, wc_ref, gm_ref, shf_ref, scf_ref, gf_ref,
                        g_ref, wup_ref, wdn_ref, q_ref, kn_ref, vn_ref, lfn_ref, *rest):
    del pt_ref
    npg = PAGES_PER_STEP
    k_refs, v_refs, lf_refs = rest[:npg], rest[npg:2 * npg], rest[2 * npg:3 * npg]
    o_ref, od_ref, h_ref, x1_ref, mlp_ref = rest[3 * npg:3 * npg + 5]
    dec = rest[3 * npg + 5:]
    s = pl.program_id(1)
    nsteps = pl.num_programs(1)

    @pl.when(s == 0)
    def _():
        mix = (jnp.dot(at_ref[...], wa_ref[...], preferred_element_type=F32)
               + jnp.dot(yc_ref[...], wc_ref[...], preferred_element_type=F32))
        x1 = x_ref[...] + gm_ref[...] * mix
        x1_ref[...] = x1
        h_ref[...] = _modnorm(x1, g_ref[...], shf_ref[...], scf_ref[...]).astype(BF16)
        mlp_ref[...] = jnp.zeros_like(mlp_ref)
        _decode_init(q_ref, kn_ref, vn_ref, lfn_ref, *dec)

    a = jnp.maximum(jnp.dot(h_ref[...], wup_ref[s], preferred_element_type=F32), 0.0)
    mlp_ref[...] += jnp.dot((a * a).astype(BF16), wdn_ref[s], preferred_element_type=F32)
    _decode_pages(k_refs, v_refs, lf_refs, *dec)

    @pl.when(s == nsteps - 1)
    def _():
        o_ref[...] = x1_ref[...] + gf_ref[...] * mlp_ref[...]
        _decode_final(od_ref, dec[3], dec[5])


def _rows_to_tile(vals):
    row = lax.broadcasted_iota(jnp.int32, (FOX_HEADS, V7X_LANES), 0)
    out = jnp.zeros((FOX_HEADS, V7X_LANES), F32)
    for h in range(FOX_HEADS):
        out = jnp.where(row == h, jnp.broadcast_to(vals[h], (FOX_HEADS, V7X_LANES)), out)
    return out


def _decode_init(q_ref, kn_ref, vn_ref, lfn_ref, qb_ref, tri_ref, m_ref, l_ref, run_ref, acc_ref):
    lanes = V7X_LANES
    lane = lax.broadcasted_iota(jnp.int32, (1, lanes), 1)
    tr = lax.broadcasted_iota(jnp.int32, (PAGE_SIZE, 2 * lanes), 0)
    tc = lax.broadcasted_iota(jnp.int32, (PAGE_SIZE, 2 * lanes), 1)
    tri_ref[...] = ((tr > tc) | (tc >= lanes)).astype(BF16)
    run_ref[...] = jnp.broadcast_to(lfn_ref[0], (FOX_HEADS, lanes))
    s_new = []
    for h in range(FOX_HEADS):
        qb_ref[h] = jnp.broadcast_to(q_ref[0, h], (HEAD_DIM, lanes))
        s_new.append(jnp.sum(q_ref[0, h] * kn_ref[0, h], axis=0, keepdims=True))
        acc_ref[h] = jnp.where(lane == 0, jnp.broadcast_to(vn_ref[0, h], (HEAD_DIM, lanes)), 0.0)
    m_ref[...] = _rows_to_tile([jnp.broadcast_to(s, (1, lanes)) for s in s_new])
    l_ref[...] = jnp.ones_like(l_ref)


def _decode_final(o_ref, l_ref, acc_ref):
    for h in range(FOX_HEADS):
        o_ref[0, h] = jnp.sum(acc_ref[h], axis=1, keepdims=True) / l_ref[h:h + 1, 0:1]


def _decode_pages(k_refs, v_refs, lf_refs, qb_ref, tri_ref, m_ref, l_ref, run_ref, acc_ref):
    npg = len(k_refs)
    lanes = V7X_LANES
    heads = range(FOX_HEADS)
    rows_to_tile = _rows_to_tile

    parts = jnp.concatenate([p for i in range(npg) for p in _split3(lf_refs[i][0])], axis=0)
    cs = jnp.dot(parts.astype(BF16), tri_ref[...], preferred_element_type=F32)
    run = run_ref[...]
    scores = []
    for i in range(npg):
        c3 = cs[24 * i:24 * i + 8] + cs[24 * i + 8:24 * i + 16] + cs[24 * i + 16:24 * i + 24]
        qk = rows_to_tile([jnp.sum(k_refs[i][0, h] * qb_ref[h], axis=0, keepdims=True) for h in heads])
        scores.append(qk + (c3[:, :lanes] + run) * LOG2E)
        run = run + c3[:, lanes:]
    run_ref[...] = run

    m_prev = m_ref[...]
    m_blk = scores[0]
    for s in scores[1:]:
        m_blk = jnp.maximum(m_blk, s)
    m_new = jnp.maximum(m_prev, jnp.max(m_blk, axis=1, keepdims=True))
    alpha = jnp.exp2(m_prev - m_new)
    ps = [jnp.exp2(s - m_new) for s in scores]
    p_tot = ps[0]
    for p in ps[1:]:
        p_tot = p_tot + p
    l_ref[...] = alpha * l_ref[...] + jnp.sum(p_tot, axis=1, keepdims=True)
    m_ref[...] = m_new
    for h in heads:
        acc = acc_ref[h] * alpha[h:h + 1, :]
        for i in range(npg):
            acc = acc + v_refs[i][0, h] * ps[i][h:h + 1, :]
        acc_ref[h] = acc


def _post_decode(x, at, w_at, yc, w_yc, mod, g_mlp, w_up, w_down,
                 q, k_new, v_new, lf_new, cache_k, cache_v, cache_logf, page_table):
    t = x.shape[0]
    tm = ROW_TILE
    b = q.shape[0]
    n_pages = page_table.shape[1]
    npg = PAGES_PER_STEP
    nsteps = n_pages // npg
    assert t // tm == b, "one prompt row tile per sample sequence"
    fch = D_FF // nsteps
    col = (FOX_HEADS, HEAD_DIM, 1)
    w_up_c = w_up.reshape(D_MODEL, nsteps, fch).transpose(1, 0, 2)
    w_dn_c = w_down.reshape(nsteps, fch, D_MODEL)
    ck = cache_k.transpose(0, 2, 3, 1)
    cv = cache_v.transpose(0, 2, 3, 1)
    clf = cache_logf.transpose(0, 2, 1)

    def page_map(nd):
        def for_slot(slot):
            return lambda bi, g, pt: (pt[bi, n_pages - 1 - (g * npg + slot)],) + (0,) * (nd - 1)
        return for_slot

    def const(shape):
        nd = len(shape)
        return pl.BlockSpec(shape, lambda bi, g, pt: (0,) * nd, pipeline_mode=pl.Buffered(1))

    rows = lambda bi, g, pt: (bi, 0)
    mcol = lambda c: (lambda bi, g, pt: (0, c))
    seq4 = lambda bi, g, pt: (bi, 0, 0, 0)
    page_blk = (1, FOX_HEADS, HEAD_DIM, PAGE_SIZE)
    in_specs = [pl.BlockSpec((tm, D_MODEL), rows),
                pl.BlockSpec((tm, FOX_WIDTH), rows), const(w_at.shape),
                pl.BlockSpec((tm, CONF_CH), rows), const(w_yc.shape),
                pl.BlockSpec((1, D_MODEL), mcol(2)), pl.BlockSpec((1, D_MODEL), mcol(3)),
                pl.BlockSpec((1, D_MODEL), mcol(4)), pl.BlockSpec((1, D_MODEL), mcol(5)),
                const((1, D_MODEL)), const(w_up_c.shape), const(w_dn_c.shape)]
    in_specs += [pl.BlockSpec((1,) + col, seq4)] * 3 + [pl.BlockSpec((1, FOX_HEADS, 1), lambda bi, g, pt: (bi, 0, 0))]
    in_specs += [pl.BlockSpec(page_blk, page_map(4)(s)) for s in range(npg)]
    in_specs += [pl.BlockSpec(page_blk, page_map(4)(s)) for s in range(npg)]
    in_specs += [pl.BlockSpec((1, FOX_HEADS, PAGE_SIZE), page_map(3)(s)) for s in range(npg)]
    grid_spec = pltpu.PrefetchScalarGridSpec(
        num_scalar_prefetch=1,
        grid=(b, nsteps),
        in_specs=in_specs,
        out_specs=[pl.BlockSpec((tm, D_MODEL), rows), pl.BlockSpec((1,) + col, seq4)],
        scratch_shapes=[pltpu.VMEM((tm, D_MODEL), BF16),
                        pltpu.VMEM((tm, D_MODEL), F32),
                        pltpu.VMEM((tm, D_MODEL), F32),
                        pltpu.VMEM((FOX_HEADS, HEAD_DIM, V7X_LANES), F32),
                        pltpu.VMEM((PAGE_SIZE, 2 * V7X_LANES), BF16),
                        pltpu.VMEM((FOX_HEADS, V7X_LANES), F32),
                        pltpu.VMEM((FOX_HEADS, V7X_LANES), F32),
                        pltpu.VMEM((FOX_HEADS, V7X_LANES), F32),
                        pltpu.VMEM((FOX_HEADS, HEAD_DIM, V7X_LANES), F32)])
    y, dec = pl.pallas_call(
        _post_decode_kernel,
        out_shape=[jax.ShapeDtypeStruct((t, D_MODEL), F32), jax.ShapeDtypeStruct((b,) + col, F32)],
        grid_spec=grid_spec,
        compiler_params=pltpu.CompilerParams(dimension_semantics=("arbitrary", "arbitrary"),
                                             vmem_limit_bytes=FUSED_VMEM_LIMIT),
        name="post_decode",
    )(page_table, x, at, w_at, yc, w_yc, mod, mod, mod, mod, g_mlp, w_up_c, w_dn_c,
      q.reshape((b,) + col), k_new.reshape((b,) + col), v_new.reshape((b,) + col),
      lf_new.reshape(b, FOX_HEADS, 1), *([ck] * npg), *([cv] * npg), *([clf] * npg))
    return y, dec.reshape(b, FOX_WIDTH).astype(BF16)


def _post_kernel(*refs, n_mix, final):
    x_ref = refs[0]
    mix_refs = refs[1:1 + 2 * n_mix]
    gm_ref, shf_ref, scf_ref, gf_ref, g_ref, wup_ref, wdn_ref = refs[1 + 2 * n_mix:8 + 2 * n_mix]
    rest = refs[8 + 2 * n_mix:]
    if final:
        gfin_ref, o_ref = rest
    else:
        (o_ref,) = rest
    mix = None
    for a in range(n_mix):
        t = jnp.dot(mix_refs[2 * a][...], mix_refs[2 * a + 1][...], preferred_element_type=F32)
        mix = t if mix is None else mix + t
    x1 = x_ref[...] + gm_ref[...] * mix
    h = _modnorm(x1, g_ref[...], shf_ref[...], scf_ref[...]).astype(BF16)
    fchunk = 1024
    acc = None
    for f0 in range(0, D_FF, fchunk):
        a = jnp.maximum(jnp.dot(h, wup_ref[:, f0:f0 + fchunk], preferred_element_type=F32), 0.0)
        t = jnp.dot((a * a).astype(BF16), wdn_ref[f0:f0 + fchunk, :], preferred_element_type=F32)
        acc = t if acc is None else acc + t
    x2 = x1 + gf_ref[...] * acc
    if final:
        ms = jnp.mean(x2 * x2, axis=-1, keepdims=True)
        x2 = x2 * lax.rsqrt(ms + EPS) * gfin_ref[...]
    o_ref[...] = x2


def _post(x, mixes, mod, g_mlp, w_up, w_down, final_g=None):
    t = x.shape[0]
    tm = min(ROW_TILE, t)
    mm = mod.shape[0]
    bm = 1 if mm == 1 else tm
    row = lambda i: (i, 0)

    def mcol(c):
        return (lambda i: (0, c)) if mm == 1 else (lambda i: (i, c))

    in_specs = [pl.BlockSpec((tm, D_MODEL), row)]
    args = [x]
    for a, w in mixes:
        in_specs += [pl.BlockSpec((tm, a.shape[1]), row), _const_spec(w.shape)]
        args += [a, w]
    in_specs += [pl.BlockSpec((bm, D_MODEL), mcol(2)), pl.BlockSpec((bm, D_MODEL), mcol(3)),
                 pl.BlockSpec((bm, D_MODEL), mcol(4)), pl.BlockSpec((bm, D_MODEL), mcol(5)),
                 _const_spec((1, D_MODEL)), _const_spec(w_up.shape), _const_spec(w_down.shape)]
    args += [mod, mod, mod, mod, g_mlp, w_up, w_down]
    if final_g is not None:
        in_specs.append(_const_spec((1, D_MODEL)))
        args.append(final_g)
    return pl.pallas_call(
        functools.partial(_post_kernel, n_mix=len(mixes), final=final_g is not None),
        out_shape=jax.ShapeDtypeStruct((t, D_MODEL), F32),
        grid=(t // tm,),
        in_specs=in_specs,
        out_specs=pl.BlockSpec((tm, D_MODEL), row),
        compiler_params=_params("arbitrary"),
        name="post",
    )(*args)


def _mix1_kernel(x_ref, sh_ref, sc_ref, g_ref, win_ref, wdw_ref, *rest, per_row_state):
    x = x_ref[...]
    tm = x.shape[0]
    h = _modnorm(x, g_ref[...], sh_ref[...], sc_ref[...]).astype(BF16)
    pad = V7X_SUBLANES
    if not per_row_state:
        by_ref, tail_ref, buf_ref = rest

        @pl.when(pl.program_id(0) == 0)
        def _():
            buf_ref[0:pad, :] = jnp.zeros((pad, D_MODEL), F32)
    else:
        s0_ref, s1_ref, by_ref, cx_ref = rest

    for c0 in range(0, D_MODEL, MIX_CHUNK):
        cols = slice(c0, c0 + MIX_CHUNK)

        def proj(part):
            w = win_ref[:, part * D_MODEL + c0:part * D_MODEL + c0 + MIX_CHUNK]
            return jnp.dot(h, w, preferred_element_type=F32)

        b_gate = proj(0)
        cx = proj(1) * proj(2)
        w0, w1, w2 = wdw_ref[0:1, cols], wdw_ref[1:2, cols], wdw_ref[2:3, cols]
        if per_row_state:
            y = w0 * s0_ref[:, cols] + w1 * s1_ref[:, cols] + w2 * cx
            cx_ref[:, cols] = cx
        else:
            buf_ref[pad:, cols] = cx
            y = w0 * buf_ref[pl.ds(pad - 2, tm), cols] + w1 * buf_ref[pl.ds(pad - 1, tm), cols] + w2 * cx
            tail = cx[tm - pad:, :]
            buf_ref[0:pad, cols] = tail
            tail_ref[:, cols] = tail
        by_ref[:, cols] = (b_gate * y).astype(BF16)


def _mix1(x, mod, g, w_in, w_dw, state=None):
    t = x.shape[0]
    tm = min(ROW_TILE, t)
    mm = mod.shape[0]
    bm = 1 if mm == 1 else tm
    row = lambda i: (i, 0)
    mcol = lambda c: (lambda i: (0, c)) if mm == 1 else (lambda i: (i, c))
    in_specs = [pl.BlockSpec((tm, D_MODEL), row),
                pl.BlockSpec((bm, D_MODEL), mcol(0)), pl.BlockSpec((bm, D_MODEL), mcol(1)),
                _const_spec((1, D_MODEL)), _const_spec(w_in.shape), _const_spec(w_dw.shape)]
    args = [x, mod, mod, g, w_in, w_dw]
    if state is not None:
        in_specs += [pl.BlockSpec((tm, D_MODEL), row)] * 2
        args += [state[0], state[1]]
        out_shape = [jax.ShapeDtypeStruct((t, D_MODEL), BF16), jax.ShapeDtypeStruct((t, D_MODEL), F32)]
        out_specs = [pl.BlockSpec((tm, D_MODEL), row)] * 2
        scratch = []
    else:
        nt = t // tm
        out_shape = [jax.ShapeDtypeStruct((t, D_MODEL), BF16),
                     jax.ShapeDtypeStruct((nt * V7X_SUBLANES, D_MODEL), F32)]
        out_specs = [pl.BlockSpec((tm, D_MODEL), row), pl.BlockSpec((V7X_SUBLANES, D_MODEL), row)]
        scratch = [pltpu.VMEM((tm + V7X_SUBLANES, D_MODEL), F32)]
    return pl.pallas_call(
        functools.partial(_mix1_kernel, per_row_state=state is not None),
        out_shape=out_shape,
        grid=(t // tm,),
        in_specs=in_specs,
        out_specs=out_specs,
        scratch_shapes=scratch,
        compiler_params=_params("arbitrary"),
        name="mix1",
    )(*args)


def kernel(x_prompt, x_sample, cache_k, cache_v, cache_logf, state_conformer_conv, state_short_conv, page_table, c_prompt, c_sample, l0_w_ada, l0_b_ada, l0_norm_mix, l0_norm_mlp, l0_w_in, l0_b_forget, l0_w_dw, l0_b_dw, l0_conv_ln_g, l0_conv_ln_b, l0_w_out, l0_w_up, l0_w_down, l1_w_ada, l1_b_ada, l1_norm_mix, l1_norm_mlp, l1_w_in, l1_w_dw, l1_w_out, l1_w_up, l1_w_down, final_norm):
    bp, t, d = x_prompt.shape
    bs = x_sample.shape[0]
    assert bp == 1 and d == D_MODEL and x_sample.shape[1] == 1

    def per_head_padded(wcols, scale):
        w3 = (wcols * scale).reshape(d, FOX_HEADS, HEAD_DIM)
        return jnp.pad(w3, ((0, 0), (0, 0), (0, AUG - HEAD_DIM))).reshape(d, FOX_HEADS * AUG).astype(BF16)

    w0 = {
        "q": per_head_padded(l0_w_in[:, :FOX_WIDTH], ATTN_SCALE * LOG2E),
        "k": per_head_padded(l0_w_in[:, FOX_WIDTH:2 * FOX_WIDTH], 1.0),
        "kv": l0_w_in[:, FOX_WIDTH:3 * FOX_WIDTH].astype(BF16),
        "dw": l0_w_dw, "b_dw": l0_b_dw.reshape(1, -1),
        "ln_g": l0_conv_ln_g.reshape(1, -1), "ln_b": l0_conv_ln_b.reshape(1, -1),
        "f":jnp.pad(l0_w_in[:, 3 * FOX_WIDTH:3 * FOX_WIDTH + FOX_HEADS],
                     ((0, 0), (0, V7X_LANES - FOX_HEADS))).astype(BF16),
        "bf": jnp.pad(l0_b_forget, (0, V7X_LANES - FOX_HEADS)).reshape(1, V7X_LANES),
        "glu": l0_w_in[:, 3 * FOX_WIDTH + FOX_HEADS:].astype(BF16),
    }
    w0_out_attn = l0_w_out[:FOX_WIDTH].astype(BF16)
    w0_out_conv = l0_w_out[FOX_WIDTH:].astype(BF16)
    w0_up, w0_down = l0_w_up.astype(BF16), l0_w_down.astype(BF16)
    w1_in, w1_out = l1_w_in.astype(BF16), l1_w_out.astype(BF16)
    w1_up, w1_down = l1_w_up.astype(BF16), l1_w_down.astype(BF16)
    g0_mix, g0_mlp = l0_norm_mix.reshape(1, d), l0_norm_mlp.reshape(1, d)
    g1_mix, g1_mlp = l1_norm_mix.reshape(1, d), l1_norm_mlp.reshape(1, d)
    gfin = final_norm.reshape(1, d)

    c_all = jnp.concatenate([c_prompt, c_sample], axis=0)
    mod0 = _ada(c_all, l0_w_ada, l0_b_ada)
    mod1 = _ada(c_all, l1_w_ada, l1_b_ada)
    mod0_p, mod0_s = mod0[:1], mod0[1:]
    mod1_p, mod1_s = mod1[:1], mod1[1:]

    xp = x_prompt.reshape(t, d)
    qa_p, ka_p, vt_p, st_p, yc_p, k_p, v_p, lf_p, u_p = _proj0(xp, mod0_p, g0_mix, w0, True)
    at_p =_attn_prompt(qa_p, ka_p, vt_p, st_p)

    xs = x_sample.reshape(bs, d)
    qp_s, k_s, v_s, lf_s, u_s = _proj0(xs, mod0_s, g0_mix, w0, False)
    q_s = qp_s.reshape(bs, FOX_HEADS, AUG)[:, :, :HEAD_DIM].reshape(bs, FOX_WIDTH)
    yc_s = _conv0_sample(state_conformer_conv.transpose(1, 0, 2), u_s,
                         l0_w_dw, l0_b_dw, l0_conv_ln_g, l0_conv_ln_b)

    x1_p, at_s = _post_decode(xp, at_p, w0_out_attn, yc_p, w0_out_conv, mod0_p, g0_mlp, w0_up, w0_down,
                              q_s, k_s, v_s, lf_s, cache_k, cache_v, cache_logf, page_table)

    by_p, tail_p = _mix1(x1_p, mod1_p, g1_mix, w1_in, l1_w_dw)
    y_p = _post(x1_p, [(by_p, w1_out)], mod1_p, g1_mlp, w1_up, w1_down, gfin)

    x1_s = _post(xs, [(at_s, w0_out_attn), (yc_s, w0_out_conv)], mod0_s, g0_mlp, w0_up, w0_down)
    by_s, cx_s = _mix1(x1_s, mod1_s, g1_mix, w1_in, l1_w_dw,
                       state=(state_short_conv[:, 0], state_short_conv[:, 1]))
    y_s = _post(x1_s, [(by_s, w1_out)], mod1_s, g1_mlp, w1_up, w1_down, gfin)

    hs = (FOX_HEADS, HEAD_DIM)
    return (y_p.reshape(1, t, d), y_s.reshape(bs, 1, d),
            k_p.reshape(1, t, *hs), v_p.reshape(1, t, *hs), lf_p.reshape(1, t, FOX_HEADS),
            u_p[t - (CONF_CONV_WIDTH - 1):].reshape(1, CONF_CONV_WIDTH - 1, CONF_CH),
            tail_p[-(SC_CONV_WIDTH - 1):].reshape(1, SC_CONV_WIDTH - 1, d),
            k_s.reshape(bs, 1, *hs), v_s.reshape(bs, 1, *hs), lf_s.reshape(bs, 1, FOX_HEADS),
            jnp.concatenate([state_conformer_conv[:, 1:], u_s[:, None, :]], axis=1),
            jnp.stack([state_short_conv[:, 1], cx_s], axis=1))
```

```python
import functools

import jax
import jax.numpy as jnp
import numpy as np
from jax import lax
from jax.experimental import pallas as pl
from jax.experimental.pallas import tpu as pltpu

F32 = jnp.float32
BF16 = jnp.bfloat16

D_MODEL = 1024
FOX_HEADS = 8
HEAD_DIM = 64
FOX_WIDTH = FOX_HEADS * HEAD_DIM
CONF_CH = D_MODEL - FOX_WIDTH
CONF_CONV_WIDTH = 31
SC_CONV_WIDTH = 3
D_FF = 4 * D_MODEL
PAGE_SIZE = 128
EPS = 1e-6
ATTN_SCALE = HEAD_DIM ** -0.5
LOG2E = 1.4426950408889634

V7X_LANES = 128
V7X_SUBLANES = 8
V7X_VMEM_BYTES = 64 * 1024 * 1024
VMEM_LIMIT = V7X_VMEM_BYTES - 12 * 1024 * 1024
FUSED_VMEM_LIMIT = V7X_VMEM_BYTES - 4 * 1024 * 1024

ROW_TILE = 512
CONV_CHUNK = 64
CONV_HALO = 32
ATTN_TILE = 256
AUG = V7X_LANES
ATTN_HEADS = 4
ATTN_UNROLL = 2
PRUNE_LOG2 = 136.0
MIX_CHUNK = 256
NORM_SLACK = 1.02
PAGES_PER_STEP = 16


def _const_spec(shape):
    nd = len(shape)
    return pl.BlockSpec(shape, lambda *_: (0,) * nd, pipeline_mode=pl.Buffered(1))


def _params(*sem):
    return pltpu.CompilerParams(dimension_semantics=sem, vmem_limit_bytes=VMEM_LIMIT)


def _modnorm(x, g, shift, scale):
    ms = jnp.mean(x * x, axis=-1, keepdims=True)
    y = x * lax.rsqrt(ms + EPS)
    return (y * g) * (1.0 + scale) + shift


def _split3(x):
    hi = x.astype(BF16).astype(F32)
    r = x - hi
    mid = r.astype(BF16).astype(F32)
    lo = (r - mid).astype(BF16).astype(F32)
    return hi, mid, lo


def _ada_kernel(c_ref, w_ref, b_ref, o_ref):
    c = c_ref[...]
    s = (c * jax.nn.sigmoid(c)).astype(BF16)
    o_ref[...] = jnp.dot(s, w_ref[...].astype(BF16), preferred_element_type=F32) + b_ref[...]


def _ada(c_all, w_ada, b_ada):
    m = c_all.shape[0]
    n = w_ada.shape[1]
    bn = 1536
    return pl.pallas_call(
        _ada_kernel,
        out_shape=jax.ShapeDtypeStruct((m, n), F32),
        grid=(n // bn,),
        in_specs=[pl.BlockSpec((m, D_MODEL), lambda j: (0, 0)),
                  pl.BlockSpec((D_MODEL, bn), lambda j: (0, j)),
                  pl.BlockSpec((1, bn), lambda j: (0, j))],
        out_specs=pl.BlockSpec((m, bn), lambda j: (0, j)),
        compiler_params=_params("arbitrary"),
        name="ada",
    )(c_all, w_ada, b_ada.reshape(1, n))


def _bias_placement():
    pq = np.zeros((V7X_LANES, FOX_HEADS * AUG), np.float32)
    pk = np.zeros((V7X_LANES, FOX_HEADS * AUG), np.float32)
    for h in range(FOX_HEADS):
        for p in range(3):
            pq[p * FOX_HEADS + h, h * AUG + HEAD_DIM + p] = 1.0
            pk[p * FOX_HEADS + h, h * AUG + HEAD_DIM + 3 + p] = -1.0
            pq[3 * FOX_HEADS, h * AUG + HEAD_DIM + 3 + p] = 1.0
            pk[3 * FOX_HEADS, h * AUG + HEAD_DIM + p] = 1.0
    return jnp.asarray(pq, BF16), jnp.asarray(pk, BF16)


def _proj0_kernel(*refs, prompt):
    x_ref, sh_ref, sc_ref, g_ref, wq_ref, wkv_ref, wf_ref, bf_ref, wglu_ref = refs[:9]
    tm = x_ref.shape[0]
    if prompt:
        (wk_ref, pq_ref, pk_ref, wdw_ref, bdw_ref, lng_ref, lnb_ref,
         qa_ref, ka_ref, vt_ref, st_ref, yc_ref, k_ref, v_ref, lf_ref, u_ref,
         tri_ref, carry_ref, buf_ref) = refs[9:]
        tk = vt_ref.shape[2]

        @pl.when(pl.program_id(0) == 0)
        def _():
            carry_ref[...] = jnp.zeros_like(carry_ref)
            buf_ref[0, 0:CONV_HALO, :] = jnp.zeros((CONV_HALO, CONF_CH), F32)
            r = lax.broadcasted_iota(jnp.int32, (tm, tm), 0)
            c = lax.broadcasted_iota(jnp.int32, (tm, tm), 1)
            tri_ref[...] = (r <= c).astype(BF16)
    else:
        q_ref, k_ref, v_ref, lf_ref, u_ref = refs[9:]

    x = x_ref[...]
    h = _modnorm(x, g_ref[...], sh_ref[...], sc_ref[...]).astype(BF16)
    glu = jnp.dot(h, wglu_ref[...], preferred_element_type=F32)
    u = glu[:, :CONF_CH] * jax.nn.sigmoid(glu[:, CONF_CH:])
    if prompt:
        buf_ref[0, CONV_HALO:, :] = u
        _conv_taps(buf_ref, wdw_ref, bdw_ref, lng_ref, lnb_ref, yc_ref, tm)
        buf_ref[0, 0:CONV_HALO, :] = u[tm - CONV_HALO:, :]
    fg =jnp.dot(h, wf_ref[...], preferred_element_type=F32) + bf_ref[...]
    lf = jnp.minimum(fg, 0.0) - jnp.log1p(jnp.exp(-jnp.abs(fg)))
    if prompt:
        lft = lf.T[:FOX_HEADS]
        parts = jnp.concatenate(_split3(lft), axis=0).astype(BF16)
        cs = jnp.dot(parts, tri_ref[...], preferred_element_type=F32)
        ft = cs[0:8] + cs[8:16] + cs[16:24] + carry_ref[...]
        carry_ref[...] = ft[:, tm - 1:tm]
        pieces_t = jnp.concatenate(
            list(_split3(ft * LOG2E)) + [jnp.ones((FOX_HEADS, tm), F32),
                                 jnp.zeros((V7X_LANES - 4 * FOX_HEADS, tm), F32)], axis=0)
        pieces = pieces_t.T.astype(BF16)

    kv = jnp.dot(h, wkv_ref[...], preferred_element_type=F32)
    k = kv[:, :FOX_WIDTH]
    v = kv[:, FOX_WIDTH:]
    qp = jnp.dot(h, wq_ref[...], preferred_element_type=F32)

    if not prompt:
        q_ref[...] = qp
    else:
        qpt = (qp + jnp.dot(pieces, pq_ref[...], preferred_element_type=F32)).T
        qa_ref[...] = qpt.astype(BF16)
        kp = jnp.dot(h, wk_ref[...], preferred_element_type=F32)
        ka_ref[...] = (kp + jnp.dot(pieces, pk_ref[...], preferred_element_type=F32)).astype(BF16)
        vtt = v.T.astype(BF16)
        for c0 in range(tm // tk):
            vt_ref[c0] = vtt[:, c0 * tk:(c0 + 1) * tk]

        nb = tm // tk
        row = lax.broadcasted_iota(jnp.int32, (FOX_HEADS, V7X_LANES), 0)
        lane = lax.broadcasted_iota(jnp.int32, (FOX_HEADS, V7X_LANES), 1)
        stat = jnp.zeros((FOX_HEADS, V7X_LANES), F32)
        for c0 in range(nb):
            blk = ft[:, c0 * tk:(c0 + 1) * tk]
            stat = jnp.where(lane == c0, jnp.max(blk, axis=1, keepdims=True), stat)
            stat = jnp.where(lane == nb + c0, jnp.min(blk, axis=1, keepdims=True), stat)
        for hh in range(FOX_HEADS):
            qh = qpt[hh * AUG:hh * AUG + HEAD_DIM, :]
            qn2 = jnp.max(jnp.sum(qh * qh, axis=0, keepdims=True), axis=1, keepdims=True)
            kh = kp[:, hh * AUG:(hh + 1) * AUG]
            kn2 = jnp.max(jnp.sum(kh * kh, axis=1, keepdims=True), axis=0, keepdims=True)
            stat = jnp.where((row == hh) & (lane == 2 * nb), qn2, stat)
            stat = jnp.where((row == hh) & (lane == 2 * nb + 1), kn2, stat)
        st_ref[0] = stat

    k_ref[...] = k
    v_ref[...] = v
    lf_ref[...] = lf[:, :FOX_HEADS]
    u_ref[...] = u


def _proj0(x, mod, g, w, prompt):
    t = x.shape[0]
    tm = min(ROW_TILE, t)
    mm = mod.shape[0]
    mrow = (lambda i: (0, 0)) if mm == 1 else (lambda i: (i, 0))
    mrow1 = (lambda i: (0, 1)) if mm == 1 else (lambda i: (i, 1))
    bm = 1 if mm == 1 else tm
    row = lambda i: (i, 0)
    wide = FOX_HEADS * AUG
    in_specs = [pl.BlockSpec((tm, D_MODEL), row),
                pl.BlockSpec((bm, D_MODEL), mrow),
                pl.BlockSpec((bm, D_MODEL), mrow1),
                _const_spec((1, D_MODEL)),
                _const_spec(w["q"].shape),
                _const_spec(w["kv"].shape),
                _const_spec(w["f"].shape),
                _const_spec((1, V7X_LANES)),
                _const_spec(w["glu"].shape)]
    args = [x, mod, mod, g, w["q"], w["kv"], w["f"], w["bf"], w["glu"]]
    tail_shape = [jax.ShapeDtypeStruct((t, FOX_WIDTH), F32), jax.ShapeDtypeStruct((t, FOX_WIDTH), F32),
                  jax.ShapeDtypeStruct((t, FOX_HEADS), F32), jax.ShapeDtypeStruct((t, CONF_CH), F32)]
    tail_specs = [pl.BlockSpec((tm, FOX_WIDTH), row), pl.BlockSpec((tm, FOX_WIDTH), row),
                  pl.BlockSpec((tm, FOX_HEADS), row), pl.BlockSpec((tm, CONF_CH), row)]
    if prompt:
        tk = ATTN_TILE
        pq, pk = _bias_placement()
        in_specs += [_const_spec(w["k"].shape), _const_spec(pq.shape), _const_spec(pk.shape),
                     _const_spec((CONF_CONV_WIDTH, CONF_CH)),
                     _const_spec((1, CONF_CH)), _const_spec((1, CONF_CH)), _const_spec((1, CONF_CH))]
        args += [w["k"], pq, pk, w["dw"], w["b_dw"], w["ln_g"], w["ln_b"]]
        out_shape = [jax.ShapeDtypeStruct((wide, t), BF16), jax.ShapeDtypeStruct((t, wide), BF16),
                     jax.ShapeDtypeStruct((t // tk, FOX_WIDTH, tk), BF16),
                     jax.ShapeDtypeStruct((t // tm, FOX_HEADS, V7X_LANES), F32),
                     jax.ShapeDtypeStruct((t, CONF_CH), BF16)] + tail_shape
        out_specs = [pl.BlockSpec((wide, tm), lambda i: (0, i)), pl.BlockSpec((tm, wide), row),
                     pl.BlockSpec((tm // tk, FOX_WIDTH, tk), lambda i: (i, 0, 0)),
                     pl.BlockSpec((1, FOX_HEADS, V7X_LANES), lambda i: (i, 0, 0)),
                     pl.BlockSpec((tm, CONF_CH), row)] + tail_specs
        scratch = [pltpu.VMEM((tm, tm), BF16), pltpu.VMEM((FOX_HEADS, 1), F32),
                   pltpu.VMEM((V7X_SUBLANES, tm + CONV_HALO, CONF_CH), F32)]
    else:
        out_shape = [jax.ShapeDtypeStruct((t, wide), F32)] + tail_shape
        out_specs = [pl.BlockSpec((tm, wide), row)] + tail_specs
        scratch = []
    return pl.pallas_call(
        functools.partial(_proj0_kernel, prompt=prompt),
        out_shape=out_shape,
        grid=(t // tm,),
        in_specs=in_specs,
        out_specs=out_specs,
        scratch_shapes=scratch,
        compiler_params=_params("arbitrary"),
        name="proj0",
    )(*args)


def _ln_silu(y, ln_g, ln_b):
    mu = jnp.mean(y, axis=-1, keepdims=True)
    d = y - mu
    var = jnp.mean(d * d, axis=-1, keepdims=True)
    z = d * lax.rsqrt(var + EPS) * ln_g + ln_b
    return z * jax.nn.sigmoid(z)


def _conv_taps(buf_ref, wdw_ref, bdw_ref, lng_ref, lnb_ref, y_ref, tm):
    span = tm + CONV_HALO - V7X_SUBLANES
    for r in range(1, V7X_SUBLANES):
        buf_ref[r, 0:span, :] = buf_ref[0, pl.ds(r, span), :]
    first = CONV_HALO - (CONF_CONV_WIDTH - 1)
    for c0 in range(0, tm, CONV_CHUNK):
        acc = jnp.zeros((CONV_CHUNK, CONF_CH), F32)
        for kk in range(CONF_CONV_WIDTH):
            r = (first + kk) % V7X_SUBLANES
            acc = acc + wdw_ref[kk:kk + 1, :] * buf_ref[r, pl.ds(first + kk - r + c0, CONV_CHUNK), :]
        z = _ln_silu(acc + bdw_ref[...], lng_ref[...], lnb_ref[...])
        y_ref[c0:c0 + CONV_CHUNK, :] = z.astype(BF16)


def _conv0s_kernel(st_ref, u_ref, wdw_ref, bdw_ref, lng_ref, lnb_ref, y_ref):
    acc = wdw_ref[CONF_CONV_WIDTH - 1:CONF_CONV_WIDTH, :] * u_ref[...]
    for kk in range(CONF_CONV_WIDTH - 1):
        acc = acc + wdw_ref[kk:kk + 1, :] * st_ref[kk]
    z = _ln_silu(acc + bdw_ref[...], lng_ref[...], lnb_ref[...])
    y_ref[...] = z.astype(BF16)


def _conv0_sample(state_t, u, w_dw, b_dw, ln_g, ln_b):
    b = u.shape[0]
    return pl.pallas_call(
        _conv0s_kernel,
        out_shape=jax.ShapeDtypeStruct((b, CONF_CH), BF16),
        name="conv0s",
    )(state_t, u, w_dw, b_dw.reshape(1, -1), ln_g.reshape(1, -1), ln_b.reshape(1, -1))


def _attn_kernel(g0_ref, qa_ref, ka_ref, vt_ref, o_ref, s_ref, acc_ref):
    i = pl.program_id(1)
    g0 = g0_ref[pl.program_id(0), i]
    tq = qa_ref.shape[1]
    heads = range(ATTN_HEADS)
    qa = [qa_ref[hh * AUG:(hh + 1) * AUG, :] for hh in heads]

    def scores(j):
        start = pl.multiple_of(j * tq, tq)
        return [jnp.dot(ka_ref[pl.ds(start, tq), hh * AUG:(hh + 1) * AUG], qa[hh],
                        preferred_element_type=F32)
                for hh in heads]

    def fold(j, slot, carry, masked):
        out = []
        for hh in heads:
            m_prev, l_prev = carry[hh]
            st = s_ref[slot, hh]
            if masked:
                key = j * tq + lax.broadcasted_iota(jnp.int32, st.shape, 0)
                qry = i * tq + lax.broadcasted_iota(jnp.int32, st.shape, 1)
                st = jnp.where(key <= qry, st, -jnp.inf)
            m_new = jnp.maximum(m_prev, jnp.max(st, axis=0, keepdims=True))
            alpha = jnp.exp2(m_prev - m_new)
            pt = jnp.exp2(st - m_new)
            l_new = alpha * l_prev + jnp.sum(pt, axis=0, keepdims=True)
            vt = vt_ref[jnp.minimum(j, last), hh * HEAD_DIM:(hh + 1) * HEAD_DIM, :]
            acc_ref[hh] = acc_ref[hh] * alpha + jnp.dot(vt, pt.astype(BF16), preferred_element_type=F32)
            out.append((m_new, l_new))
        return tuple(out)

    def step(j, slot, carry, masked):
        nxt = scores(jnp.minimum(j + 1, last))
        carry = fold(j, slot, carry, masked)
        for hh in heads:
            s_ref[1 - slot, hh] = nxt[hh]
        return carry

    def group(gi, carry):
        for u in range(ATTN_UNROLL):
            carry = step(ATTN_UNROLL * gi + u, u % 2, carry, False)
        return carry

    last = vt_ref.shape[0] - 1
    first = scores(ATTN_UNROLL * g0)
    for hh in heads:
        s_ref[0, hh] = first[hh]
    acc_ref[...] = jnp.zeros_like(acc_ref)
    init = tuple((jnp.full((1, tq), -jnp.inf, F32), jnp.zeros((1, tq), F32)) for _ in heads)
    ngroup = i // ATTN_UNROLL
    carry = lax.fori_loop(g0, ngroup, group, init)
    for u in range(ATTN_UNROLL - 1):
        carry = step(ATTN_UNROLL * ngroup + u, u % 2, carry, True)
    carry = fold(ATTN_UNROLL * ngroup + ATTN_UNROLL - 1, (ATTN_UNROLL - 1) % 2, carry, True)
    ot = jnp.concatenate([acc_ref[hh] / carry[hh][1] for hh in heads], axis=0)
    o_ref[...] = ot.T.astype(BF16)


def _first_live_group(stat, nblk):
    per = stat.shape[0]
    nb = nblk // per
    fmax = stat[:, :, 0:nb].transpose(1, 0, 2).reshape(FOX_HEADS, nblk)
    fmin = stat[:, :, nb:2 * nb].transpose(1, 0, 2).reshape(FOX_HEADS, nblk)
    qn = jnp.repeat(jnp.sqrt(stat[:, :, 2 * nb]).T, nb, axis=1)
    kn = jnp.repeat(jnp.sqrt(stat[:, :, 2 * nb + 1]).T, nb, axis=1)
    qk = qn[:, :, None] * kn[:, None, :] + (qn * kn)[:, :, None]
    bound = NORM_SLACK * qk + LOG2E * (fmax[:, :, None] - fmin[:, None, :])
    dead = (bound <= -PRUNE_LOG2).reshape(FOX_HEADS // ATTN_HEADS, ATTN_HEADS, nblk, nblk).all(axis=1)
    key_blk = lax.broadcasted_iota(jnp.int32, dead.shape, 2)
    first_live = jnp.min(jnp.where(dead, nblk, key_blk), axis=2)
    return (first_live // ATTN_UNROLL).astype(jnp.int32)


def _attn_prompt(qa, ka, vt, stat):
    t = ka.shape[0]
    tq = ATTN_TILE
    nh = ATTN_HEADS
    nblk = t // tq
    once = pl.Buffered(1)
    grid_spec = pltpu.PrefetchScalarGridSpec(
        num_scalar_prefetch=1,
        grid=(FOX_HEADS // nh, nblk),
        in_specs=[pl.BlockSpec((nh * AUG, tq), lambda p, i, g0: (p, i)),
                  pl.BlockSpec((t, nh * AUG), lambda p, i, g0: (0, p), pipeline_mode=once),
                  pl.BlockSpec((nblk, nh * HEAD_DIM, tq), lambda p, i, g0: (0, p, 0), pipeline_mode=once)],
        out_specs=pl.BlockSpec((tq, nh * HEAD_DIM), lambda p, i, g0: (i, p)),
        scratch_shapes=[pltpu.VMEM((2, nh, tq, tq), F32),
                        pltpu.VMEM((nh, HEAD_DIM, tq), F32)])
    return pl.pallas_call(
        _attn_kernel,
        out_shape=jax.ShapeDtypeStruct((t, FOX_WIDTH), BF16),
        grid_spec=grid_spec,
        compiler_params=_params("arbitrary", "arbitrary"),
        name="attn_prompt",
    )(_first_live_group(stat, nblk), qa, ka, vt)


def _post_decode_kernel(pt_ref, x_ref, at_ref, wa_ref, yc_ref, wc_ref, gm_ref, shf_ref, scf_ref, gf_ref,
                        g_ref, wup_ref, wdn_ref, q_ref, kn_ref, vn_ref, lfn_ref, *rest):
    del pt_ref
    npg = PAGES_PER_STEP
    k_refs, v_refs, lf_refs = rest[:npg], rest[npg:2 * npg], rest[2 * npg:3 * npg]
    o_ref, od_ref, h_ref, x1_ref, mlp_ref = rest[3 * npg:3 * npg + 5]
    dec = rest[3 * npg + 5:]
    s = pl.program_id(1)
    nsteps = pl.num_programs(1)

    @pl.when(s == 0)
    def _():
        mix = (jnp.dot(at_ref[...], wa_ref[...], preferred_element_type=F32)
               + jnp.dot(yc_ref[...], wc_ref[...], preferred_element_type=F32))
        x1 = x_ref[...] + gm_ref[...] * mix
        x1_ref[...] = x1
        h_ref[...] = _modnorm(x1, g_ref[...], shf_ref[...], scf_ref[...]).astype(BF16)
        mlp_ref[...] = jnp.zeros_like(mlp_ref)
        _decode_init(q_ref, kn_ref, vn_ref, lfn_ref, *dec)

    a = jnp.maximum(jnp.dot(h_ref[...], wup_ref[s], preferred_element_type=F32), 0.0)
    mlp_ref[...] += jnp.dot((a * a).astype(BF16), wdn_ref[s], preferred_element_type=F32)
    _decode_pages(k_refs, v_refs, lf_refs, *dec)

    @pl.when(s == nsteps - 1)
    def _():
        o_ref[...] = x1_ref[...] + gf_ref[...] * mlp_ref[...]
        _decode_final(od_ref, dec[3], dec[5])


def _rows_to_tile(vals):
    row = lax.broadcasted_iota(jnp.int32, (FOX_HEADS, V7X_LANES), 0)
    out = jnp.zeros((FOX_HEADS, V7X_LANES), F32)
    for h in range(FOX_HEADS):
        out = jnp.where(row == h, jnp.broadcast_to(vals[h], (FOX_HEADS, V7X_LANES)), out)
    return out


def _decode_init(q_ref, kn_ref, vn_ref, lfn_ref, qb_ref, tri_ref, m_ref, l_ref, run_ref, acc_ref):
    lanes = V7X_LANES
    lane = lax.broadcasted_iota(jnp.int32, (1, lanes), 1)
    tr = lax.broadcasted_iota(jnp.int32, (PAGE_SIZE, 2 * lanes), 0)
    tc = lax.broadcasted_iota(jnp.int32, (PAGE_SIZE, 2 * lanes), 1)
    tri_ref[...] = ((tr > tc) | (tc >= lanes)).astype(BF16)
    run_ref[...] = jnp.broadcast_to(lfn_ref[0], (FOX_HEADS, lanes))
    s_new = []
    for h in range(FOX_HEADS):
        qb_ref[h] = jnp.broadcast_to(q_ref[0, h], (HEAD_DIM, lanes))
        s_new.append(jnp.sum(q_ref[0, h] * kn_ref[0, h], axis=0, keepdims=True))
        acc_ref[h] = jnp.where(lane == 0, jnp.broadcast_to(vn_ref[0, h], (HEAD_DIM, lanes)), 0.0)
    m_ref[...] = _rows_to_tile([jnp.broadcast_to(s, (1, lanes)) for s in s_new])
    l_ref[...] = jnp.ones_like(l_ref)


def _decode_final(o_ref, l_ref, acc_ref):
    for h in range(FOX_HEADS):
        o_ref[0, h] = jnp.sum(acc_ref[h], axis=1, keepdims=True) / l_ref[h:h + 1, 0:1]


def _decode_pages(k_refs, v_refs, lf_refs, qb_ref, tri_ref, m_ref, l_ref, run_ref, acc_ref):
    npg = len(k_refs)
    lanes = V7X_LANES
    heads = range(FOX_HEADS)
    rows_to_tile = _rows_to_tile

    parts = jnp.concatenate([p for i in range(npg) for p in _split3(lf_refs[i][0])], axis=0)
    cs = jnp.dot(parts.astype(BF16), tri_ref[...], preferred_element_type=F32)
    run = run_ref[...]
    scores = []
    for i in range(npg):
        c3 = cs[24 * i:24 * i + 8] + cs[24 * i + 8:24 * i + 16] + cs[24 * i + 16:24 * i + 24]
        qk = rows_to_tile([jnp.sum(k_refs[i][0, h] * qb_ref[h], axis=0, keepdims=True) for h in heads])
        scores.append(qk + (c3[:, :lanes] + run) * LOG2E)
        run = run + c3[:, lanes:]
    run_ref[...] = run

    m_prev = m_ref[...]
    m_blk = scores[0]
    for s in scores[1:]:
        m_blk = jnp.maximum(m_blk, s)
    m_new = jnp.maximum(m_prev, jnp.max(m_blk, axis=1, keepdims=True))
    alpha = jnp.exp2(m_prev - m_new)
    ps = [jnp.exp2(s - m_new) for s in scores]
    p_tot = ps[0]
    for p in ps[1:]:
        p_tot = p_tot + p
    l_ref[...] = alpha * l_ref[...] + jnp.sum(p_tot, axis=1, keepdims=True)
    m_ref[...] = m_new
    for h in heads:
        acc = acc_ref[h] * alpha[h:h + 1, :]
        for i in range(npg):
            acc = acc + v_refs[i][0, h] * ps[i][h:h + 1, :]
        acc_ref[h] = acc


def _post_decode(x, at, w_at, yc, w_yc, mod, g_mlp, w_up, w_down,
                 q, k_new, v_new, lf_new, cache_k, cache_v, cache_logf, page_table):
    t = x.shape[0]
    tm = ROW_TILE
    b = q.shape[0]
    n_pages = page_table.shape[1]
    npg = PAGES_PER_STEP
    nsteps = n_pages // npg
    assert t // tm == b, "one prompt row tile per sample sequence"
    fch = D_FF // nsteps
    col = (FOX_HEADS, HEAD_DIM, 1)
    w_up_c = w_up.reshape(D_MODEL, nsteps, fch).transpose(1, 0, 2)
    w_dn_c = w_down.reshape(nsteps, fch, D_MODEL)
    ck = cache_k.transpose(0, 2, 3, 1)
    cv = cache_v.transpose(0, 2, 3, 1)
    clf = cache_logf.transpose(0, 2, 1)

    def page_map(nd):
        def for_slot(slot):
            return lambda bi, g, pt: (pt[bi, n_pages - 1 - (g * npg + slot)],) + (0,) * (nd - 1)
        return for_slot

    def const(shape):
        nd = len(shape)
        return pl.BlockSpec(shape, lambda bi, g, pt: (0,) * nd, pipeline_mode=pl.Buffered(1))

    rows = lambda bi, g, pt: (bi, 0)
    mcol = lambda c: (lambda bi, g, pt: (0, c))
    seq4 = lambda bi, g, pt: (bi, 0, 0, 0)
    page_blk = (1, FOX_HEADS, HEAD_DIM, PAGE_SIZE)
    in_specs = [pl.BlockSpec((tm, D_MODEL), rows),
                pl.BlockSpec((tm, FOX_WIDTH), rows), const(w_at.shape),
                pl.BlockSpec((tm, CONF_CH), rows), const(w_yc.shape),
                pl.BlockSpec((1, D_MODEL), mcol(2)), pl.BlockSpec((1, D_MODEL), mcol(3)),
                pl.BlockSpec((1, D_MODEL), mcol(4)), pl.BlockSpec((1, D_MODEL), mcol(5)),
                const((1, D_MODEL)), const(w_up_c.shape), const(w_dn_c.shape)]
    in_specs += [pl.BlockSpec((1,) + col, seq4)] * 3 + [pl.BlockSpec((1, FOX_HEADS, 1), lambda bi, g, pt: (bi, 0, 0))]
    in_specs += [pl.BlockSpec(page_blk, page_map(4)(s)) for s in range(npg)]
    in_specs += [pl.BlockSpec(page_blk, page_map(4)(s)) for s in range(npg)]
    in_specs += [pl.BlockSpec((1, FOX_HEADS, PAGE_SIZE), page_map(3)(s)) for s in range(npg)]
    grid_spec = pltpu.PrefetchScalarGridSpec(
        num_scalar_prefetch=1,
        grid=(b, nsteps),
        in_specs=in_specs,
        out_specs=[pl.BlockSpec((tm, D_MODEL), rows), pl.BlockSpec((1,) + col, seq4)],
        scratch_shapes=[pltpu.VMEM((tm, D_MODEL), BF16),
                        pltpu.VMEM((tm, D_MODEL), F32),
                        pltpu.VMEM((tm, D_MODEL), F32),
                        pltpu.VMEM((FOX_HEADS, HEAD_DIM, V7X_LANES), F32),
                        pltpu.VMEM((PAGE_SIZE, 2 * V7X_LANES), BF16),
                        pltpu.VMEM((FOX_HEADS, V7X_LANES), F32),
                        pltpu.VMEM((FOX_HEADS, V7X_LANES), F32),
                        pltpu.VMEM((FOX_HEADS, V7X_LANES), F32),
                        pltpu.VMEM((FOX_HEADS, HEAD_DIM, V7X_LANES), F32)])
    y, dec = pl.pallas_call(
        _post_decode_kernel,
        out_shape=[jax.ShapeDtypeStruct((t, D_MODEL), F32), jax.ShapeDtypeStruct((b,) + col, F32)],
        grid_spec=grid_spec,
        compiler_params=pltpu.CompilerParams(dimension_semantics=("arbitrary", "arbitrary"),
                                             vmem_limit_bytes=FUSED_VMEM_LIMIT),
        name="post_decode",
    )(page_table, x, at, w_at, yc, w_yc, mod, mod, mod, mod, g_mlp, w_up_c, w_dn_c,
      q.reshape((b,) + col), k_new.reshape((b,) + col), v_new.reshape((b,) + col),
      lf_new.reshape(b, FOX_HEADS, 1), *([ck] * npg), *([cv] * npg), *([clf] * npg))
    return y, dec.reshape(b, FOX_WIDTH).astype(BF16)


def _post_kernel(*refs, n_mix, final):
    x_ref = refs[0]
    mix_refs = refs[1:1 + 2 * n_mix]
    gm_ref, shf_ref, scf_ref, gf_ref, g_ref, wup_ref, wdn_ref = refs[1 + 2 * n_mix:8 + 2 * n_mix]
    rest = refs[8 + 2 * n_mix:]
    if final:
        gfin_ref, o_ref = rest
    else:
        (o_ref,) = rest
    mix = None
    for a in range(n_mix):
        t = jnp.dot(mix_refs[2 * a][...], mix_refs[2 * a + 1][...], preferred_element_type=F32)
        mix = t if mix is None else mix + t
    x1 = x_ref[...] + gm_ref[...] * mix
    h = _modnorm(x1, g_ref[...], shf_ref[...], scf_ref[...]).astype(BF16)
    fchunk = 1024
    acc = None
    for f0 in range(0, D_FF, fchunk):
        a = jnp.maximum(jnp.dot(h, wup_ref[:, f0:f0 + fchunk], preferred_element_type=F32), 0.0)
        t = jnp.dot((a * a).astype(BF16), wdn_ref[f0:f0 + fchunk, :], preferred_element_type=F32)
        acc = t if acc is None else acc + t
    x2 = x1 + gf_ref[...] * acc
    if final:
        ms = jnp.mean(x2 * x2, axis=-1, keepdims=True)
        x2 = x2 * lax.rsqrt(ms + EPS) * gfin_ref[...]
    o_ref[...] = x2


def _post(x, mixes, mod, g_mlp, w_up, w_down, final_g=None):
    t = x.shape[0]
    tm = min(ROW_TILE, t)
    mm = mod.shape[0]
    bm = 1 if mm == 1 else tm
    row = lambda i: (i, 0)

    def mcol(c):
        return (lambda i: (0, c)) if mm == 1 else (lambda i: (i, c))

    in_specs = [pl.BlockSpec((tm, D_MODEL), row)]
    args = [x]
    for a, w in mixes:
        in_specs += [pl.BlockSpec((tm, a.shape[1]), row), _const_spec(w.shape)]
        args += [a, w]
    in_specs += [pl.BlockSpec((bm, D_MODEL), mcol(2)), pl.BlockSpec((bm, D_MODEL), mcol(3)),
                 pl.BlockSpec((bm, D_MODEL), mcol(4)), pl.BlockSpec((bm, D_MODEL), mcol(5)),
                 _const_spec((1, D_MODEL)), _const_spec(w_up.shape), _const_spec(w_down.shape)]
    args += [mod, mod, mod, mod, g_mlp, w_up, w_down]
    if final_g is not None:
        in_specs.append(_const_spec((1, D_MODEL)))
        args.append(final_g)
    return pl.pallas_call(
        functools.partial(_post_kernel, n_mix=len(mixes), final=final_g is not None),
        out_shape=jax.ShapeDtypeStruct((t, D_MODEL), F32),
        grid=(t // tm,),
        in_specs=in_specs,
        out_specs=pl.BlockSpec((tm, D_MODEL), row),
        compiler_params=_params("arbitrary"),
        name="post",
    )(*args)


def _mix1_kernel(x_ref, sh_ref, sc_ref, g_ref, win_ref, wdw_ref, *rest, per_row_state):
    x = x_ref[...]
    tm = x.shape[0]
    h = _modnorm(x, g_ref[...], sh_ref[...], sc_ref[...]).astype(BF16)
    pad = V7X_SUBLANES
    if not per_row_state:
        by_ref, tail_ref, buf_ref = rest

        @pl.when(pl.program_id(0) == 0)
        def _():
            buf_ref[0:pad, :] = jnp.zeros((pad, D_MODEL), F32)
    else:
        s0_ref, s1_ref, by_ref, cx_ref = rest

    for c0 in range(0, D_MODEL, MIX_CHUNK):
        cols = slice(c0, c0 + MIX_CHUNK)

        def proj(part):
            w = win_ref[:, part * D_MODEL + c0:part * D_MODEL + c0 + MIX_CHUNK]
            return jnp.dot(h, w, preferred_element_type=F32)

        b_gate = proj(0)
        cx = proj(1) * proj(2)
        w0, w1, w2 = wdw_ref[0:1, cols], wdw_ref[1:2, cols], wdw_ref[2:3, cols]
        if per_row_state:
            y = w0 * s0_ref[:, cols] + w1 * s1_ref[:, cols] + w2 * cx
            cx_ref[:, cols] = cx
        else:
            buf_ref[pad:, cols] = cx
            y = w0 * buf_ref[pl.ds(pad - 2, tm), cols] + w1 * buf_ref[pl.ds(pad - 1, tm), cols] + w2 * cx
            tail = cx[tm - pad:, :]
            buf_ref[0:pad, cols] = tail
            tail_ref[:, cols] = tail
        by_ref[:, cols] = (b_gate * y).astype(BF16)


def _mix1(x, mod, g, w_in, w_dw, state=None):
    t = x.shape[0]
    tm = min(ROW_TILE, t)
    mm = mod.shape[0]
    bm = 1 if mm == 1 else tm
    row = lambda i: (i, 0)
    mcol = lambda c: (lambda i: (0, c)) if mm == 1 else (lambda i: (i, c))
    in_specs = [pl.BlockSpec((tm, D_MODEL), row),
                pl.BlockSpec((bm, D_MODEL), mcol(0)), pl.BlockSpec((bm, D_MODEL), mcol(1)),
                _const_spec((1, D_MODEL)), _const_spec(w_in.shape), _const_spec(w_dw.shape)]
    args = [x, mod, mod, g, w_in, w_dw]
    if state is not None:
        in_specs += [pl.BlockSpec((tm, D_MODEL), row)] * 2
        args += [state[0], state[1]]
        out_shape = [jax.ShapeDtypeStruct((t, D_MODEL), BF16), jax.ShapeDtypeStruct((t, D_MODEL), F32)]
        out_specs = [pl.BlockSpec((tm, D_MODEL), row)] * 2
        scratch = []
    else:
        nt = t // tm
        out_shape = [jax.ShapeDtypeStruct((t, D_MODEL), BF16),
                     jax.ShapeDtypeStruct((nt * V7X_SUBLANES, D_MODEL), F32)]
        out_specs = [pl.BlockSpec((tm, D_MODEL), row), pl.BlockSpec((V7X_SUBLANES, D_MODEL), row)]
        scratch = [pltpu.VMEM((tm + V7X_SUBLANES, D_MODEL), F32)]
    return pl.pallas_call(
        functools.partial(_mix1_kernel, per_row_state=state is not None),
        out_shape=out_shape,
        grid=(t // tm,),
        in_specs=in_specs,
        out_specs=out_specs,
        scratch_shapes=scratch,
        compiler_params=_params("arbitrary"),
        name="mix1",
    )(*args)


def kernel(x_prompt, x_sample, cache_k, cache_v, cache_logf, state_conformer_conv, state_short_conv, page_table, c_prompt, c_sample, l0_w_ada, l0_b_ada, l0_norm_mix, l0_norm_mlp, l0_w_in, l0_b_forget, l0_w_dw, l0_b_dw, l0_conv_ln_g, l0_conv_ln_b, l0_w_out, l0_w_up, l0_w_down, l1_w_ada, l1_b_ada, l1_norm_mix, l1_norm_mlp, l1_w_in, l1_w_dw, l1_w_out, l1_w_up, l1_w_down, final_norm):
    bp, t, d = x_prompt.shape
    bs = x_sample.shape[0]
    assert bp == 1 and d == D_MODEL and x_sample.shape[1] == 1

    def per_head_padded(wcols, scale):
        w3 = (wcols * scale).reshape(d, FOX_HEADS, HEAD_DIM)
        return jnp.pad(w3, ((0, 0), (0, 0), (0, AUG - HEAD_DIM))).reshape(d, FOX_HEADS * AUG).astype(BF16)

    w0 = {
        "q": per_head_padded(l0_w_in[:, :FOX_WIDTH], ATTN_SCALE * LOG2E),
        "k": per_head_padded(l0_w_in[:, FOX_WIDTH:2 * FOX_WIDTH], 1.0),
        "kv": l0_w_in[:, FOX_WIDTH:3 * FOX_WIDTH].astype(BF16),
        "dw": l0_w_dw, "b_dw": l0_b_dw.reshape(1, -1),
        "ln_g": l0_conv_ln_g.reshape(1, -1), "ln_b": l0_conv_ln_b.reshape(1, -1),
        "f":jnp.pad(l0_w_in[:, 3 * FOX_WIDTH:3 * FOX_WIDTH + FOX_HEADS],
                     ((0, 0), (0, V7X_LANES - FOX_HEADS))).astype(BF16),
        "bf": jnp.pad(l0_b_forget, (0, V7X_LANES - FOX_HEADS)).reshape(1, V7X_LANES),
        "glu": l0_w_in[:, 3 * FOX_WIDTH + FOX_HEADS:].astype(BF16),
    }
    w0_out_attn = l0_w_out[:FOX_WIDTH].astype(BF16)
    w0_out_conv = l0_w_out[FOX_WIDTH:].astype(BF16)
    w0_up, w0_down = l0_w_up.astype(BF16), l0_w_down.astype(BF16)
    w1_in, w1_out = l1_w_in.astype(BF16), l1_w_out.astype(BF16)
    w1_up, w1_down = l1_w_up.astype(BF16), l1_w_down.astype(BF16)
    g0_mix, g0_mlp = l0_norm_mix.reshape(1, d), l0_norm_mlp.reshape(1, d)
    g1_mix, g1_mlp = l1_norm_mix.reshape(1, d), l1_norm_mlp.reshape(1, d)
    gfin = final_norm.reshape(1, d)

    c_all = jnp.concatenate([c_prompt, c_sample], axis=0)
    mod0 = _ada(c_all, l0_w_ada, l0_b_ada)
    mod1 = _ada(c_all, l1_w_ada, l1_b_ada)
    mod0_p, mod0_s = mod0[:1], mod0[1:]
    mod1_p, mod1_s = mod1[:1], mod1[1:]

    xp = x_prompt.reshape(t, d)
    qa_p, ka_p, vt_p, st_p, yc_p, k_p, v_p, lf_p, u_p = _proj0(xp, mod0_p, g0_mix, w0, True)
    at_p =_attn_prompt(qa_p, ka_p, vt_p, st_p)

    xs = x_sample.reshape(bs, d)
    qp_s, k_s, v_s, lf_s, u_s = _proj0(xs, mod0_s, g0_mix, w0, False)
    q_s = qp_s.reshape(bs, FOX_HEADS, AUG)[:, :, :HEAD_DIM].reshape(bs, FOX_WIDTH)
    yc_s = _conv0_sample(state_conformer_conv.transpose(1, 0, 2), u_s,
                         l0_w_dw, l0_b_dw, l0_conv_ln_g, l0_conv_ln_b)

    x1_p, at_s = _post_decode(xp, at_p, w0_out_attn, yc_p, w0_out_conv, mod0_p, g0_mlp, w0_up, w0_down,
                              q_s, k_s, v_s, lf_s, cache_k, cache_v, cache_logf, page_table)

    by_p, tail_p = _mix1(x1_p, mod1_p, g1_mix, w1_in, l1_w_dw)
    y_p = _post(x1_p, [(by_p, w1_out)], mod1_p, g1_mlp, w1_up, w1_down, gfin)

    x1_s = _post(xs, [(at_s, w0_out_attn), (yc_s, w0_out_conv)], mod0_s, g0_mlp, w0_up, w0_down)
    by_s, cx_s = _mix1(x1_s, mod1_s, g1_mix, w1_in, l1_w_dw,
                       state=(state_short_conv[:, 0], state_short_conv[:, 1]))
    y_s = _post(x1_s, [(by_s, w1_out)], mod1_s, g1_mlp, w1_up, w1_down, gfin)

    hs = (FOX_HEADS, HEAD_DIM)
    return (y_p.reshape(1, t, d), y_s.reshape(bs, 1, d),
            k_p.reshape(1, t, *hs), v_p.reshape(1, t, *hs), lf_p.reshape(1, t, FOX_HEADS),
            u_p[t - (CONF_CONV_WIDTH - 1):].reshape(1, CONF_CONV_WIDTH - 1, CONF_CH),
            tail_p[-(SC_CONV_WIDTH - 1):].reshape(1, SC_CONV_WIDTH - 1, d),
            k_s.reshape(bs, 1, *hs), v_s.reshape(bs, 1, *hs), lf_s.reshape(bs, 1, FOX_HEADS),
            jnp.concatenate([state_conformer_conv[:, 1:], u_s[:, None, :]], axis=1),
            jnp.stack([state_short_conv[:, 1], cx_s], axis=1))
```
